```python
import jax
import jax.numpy as jnp
from jax import lax
import numpy as np

D_MODEL = 1024
BATCH = 4
SEQ = 4096
DEPTH = 2
DEC_BATCH = 128
DEC_SEQ = 4
PAST_LEN = 2048
PAGE_SIZE = 128

NSA_HEADS = 8
NSA_KV_HEADS = 2
NSA_GROUP = NSA_HEADS // NSA_KV_HEADS
NSA_HEAD_DIM = 64
NSA_WIDTH = NSA_HEADS * NSA_HEAD_DIM
NSA_KV_WIDTH = NSA_KV_HEADS * NSA_HEAD_DIM
CMP_BLOCK = 32
CMP_STRIDE = 16
CMP_HIDDEN = 2 * NSA_HEAD_DIM
SEL_BLOCK = 64
SEL_TOPN = 16
WINDOW = 512
GLA_HEADS = 4
GLA_KEY_WIDTH = D_MODEL // 2
GLA_VAL_WIDTH = D_MODEL
GLA_DK = GLA_KEY_WIDTH // GLA_HEADS
GLA_DV = GLA_VAL_WIDTH // GLA_HEADS
GLA_GATE_RANK = 16
GLA_GATE_TAU = 16.0
GLA_CHUNK = 32
Q_BLOCK = 128
NORM_EPS = 1e-6
NEG_INF = -1e30
TINY = 1e-30
FORCE_BONUS = 1e4
IN_SIZES = (NSA_WIDTH, 6 * NSA_KV_WIDTH, 3 * NSA_HEADS, NSA_WIDTH, GLA_KEY_WIDTH, GLA_KEY_WIDTH, GLA_VAL_WIDTH, GLA_GATE_RANK, GLA_VAL_WIDTH, 2 * D_MODEL)
D_IN = NSA_WIDTH * 2 + 6 * NSA_KV_WIDTH + 3 * NSA_HEADS + 2 * GLA_KEY_WIDTH + 2 * GLA_VAL_WIDTH + GLA_GATE_RANK + 2 * D_MODEL

kernel_name = 'hybrid_nsa_gla_gated_merge_step'

F32 = jnp.float32


def _split_offsets():
    return tuple(int(o) for o in np.cumsum(np.array(IN_SIZES))[:-1])


def rms_norm(x, g):
    xf = x.astype(F32)
    xf = xf * lax.rsqrt(jnp.mean(xf * xf, axis=-1, keepdims=True) + NORM_EPS)
    return xf.astype(x.dtype) * g


def alibi_slopes():
    h = jnp.arange(1, NSA_HEADS + 1, dtype=F32)
    return jnp.exp2(-8.0 * h / NSA_HEADS).reshape(NSA_KV_HEADS, NSA_GROUP)


def alibi_bias(slopes, dist):
    return -slopes[:, :, None, None] * dist[..., None, None, :, :].astype(F32)


def masked_softmax(s, mask):
    s = jnp.where(mask, s, NEG_INF)
    p = jnp.where(mask, jnp.exp(s - jnp.max(s, axis=-1, keepdims=True)), 0.0)
    return p / jnp.maximum(jnp.sum(p, axis=-1, keepdims=True), TINY)


def attend(q, k, v, bias, mask):
    scale = q.shape[-1] ** -0.5
    s = jnp.einsum('...qgrd,...kgd->...grqk', q.astype(F32), k.astype(F32)) * scale + bias
    p = masked_softmax(s, mask)
    o = jnp.einsum('...grqk,...kgd->...qgrd', p, v.astype(F32))
    return o, p


def compress_blocks(x, pe, w1, w2):
    B, L, G, hd = x.shape
    n_cmp = (L - CMP_BLOCK) // CMP_STRIDE + 1
    idx = np.arange(n_cmp)[:, None] * CMP_STRIDE + np.arange(CMP_BLOCK)[None, :]
    blocks = x[:, idx] + pe[:, None, :]
    flat = blocks.transpose(0, 1, 3, 2, 4).reshape(B, n_cmp, G, CMP_BLOCK * hd)
    return jax.nn.silu(flat @ w1) @ w2


def cmp_block_end(L):
    n_cmp = (L - CMP_BLOCK) // CMP_STRIDE + 1
    return np.arange(n_cmp) * CMP_STRIDE + CMP_BLOCK - 1


def sel_overlap(n_cmp, n_sel):
    cs = np.arange(n_cmp)[:, None] * CMP_STRIDE
    ss = np.arange(n_sel)[None, :] * SEL_BLOCK
    ov = np.clip(np.minimum(cs + CMP_BLOCK, ss + SEL_BLOCK) - np.maximum(cs, ss), 0, None)
    return jnp.asarray((ov / CMP_BLOCK).astype(np.float32))


def to_sel_blocks(x):
    B, L, G, hd = x.shape
    n_sel = -(-L // SEL_BLOCK)
    xp = jnp.pad(x, ((0, 0), (0, n_sel * SEL_BLOCK - L), (0, 0), (0, 0)))
    return xp.reshape(B, n_sel, SEL_BLOCK, G, hd)


def nsa_cmp_slc(q, q_pos, kc, vc, cmp_end, ks_blocks, vs_blocks, slopes):
    B, Tq, G, R, hd = q.shape
    dist_c = q_pos[:, None] - cmp_end[None, :]
    o_cmp, p_cmp = attend(q, kc, vc, alibi_bias(slopes, dist_c), (dist_c >= 0)[None, None])
    n_sel = ks_blocks.shape[1]
    imp = jnp.einsum('bgrqc,cs->bgqs', p_cmp, sel_overlap(kc.shape[1], n_sel))
    blk = np.arange(n_sel)
    valid = blk[None, :] * SEL_BLOCK <= q_pos[:, None]
    forced = (blk[None, :] == 0) | (blk[None, :] == q_pos[:, None] // SEL_BLOCK)
    score = jnp.where(valid, imp + jnp.where(forced, FORCE_BONUS, 0.0), NEG_INF)
    _, idx = lax.top_k(score, min(SEL_TOPN, n_sel))
    n = idx.shape[-1]
    bi = jnp.arange(B)[:, None, None, None]
    gi = jnp.arange(G)[None, :, None, None]
    kg = ks_blocks.transpose(0, 3, 1, 2, 4)[bi, gi, idx].reshape(B, G, Tq, n * SEL_BLOCK, hd)
    vg = vs_blocks.transpose(0, 3, 1, 2, 4)[bi, gi, idx].reshape(B, G, Tq, n * SEL_BLOCK, hd)
    kpos = (idx[..., None] * SEL_BLOCK + jnp.arange(SEL_BLOCK)).reshape(B, G, Tq, n * SEL_BLOCK)
    dist_s = q_pos[None, None, :, None] - kpos
    s = (jnp.einsum('bqgrd,bgqkd->bgrqk', q.astype(F32), kg.astype(F32)) * hd ** -0.5
         - slopes[None, :, :, None, None] * dist_s[:, :, None].astype(F32))
    p = masked_softmax(s, (dist_s >= 0)[:, :, None])
    o_slc = jnp.einsum('bgrqk,bgqkd->bqgrd', p, vg.astype(F32))
    return o_cmp, o_slc


def window_attend(q, k, v, q_pos, k_pos, slopes):
    dist = q_pos[..., :, None] - k_pos[..., None, :]
    mask = (dist >= 0) & (dist < WINDOW) & (k_pos[..., None, :] >= 0)
    o, _ = attend(q, k, v, alibi_bias(slopes, dist), mask[..., None, None, :, :])
    return o


def window_prompt(q, k, v, slopes):
    B, T = k.shape[:2]
    nqb = T // Q_BLOCK
    band = np.arange(nqb)[:, None] * Q_BLOCK + np.arange(WINDOW + Q_BLOCK)[None, :]
    pad = ((0, 0), (WINDOW, 0), (0, 0), (0, 0))
    kb = jnp.pad(k, pad)[:, band]
    vb = jnp.pad(v, pad)[:, band]
    qb = q.reshape(B, nqb, Q_BLOCK, NSA_KV_HEADS, NSA_GROUP, NSA_HEAD_DIM)
    o = window_attend(qb, kb, vb, np.arange(T).reshape(nqb, Q_BLOCK), band - WINDOW, slopes)
    return o.reshape(B, T, NSA_KV_HEADS, NSA_GROUP, NSA_HEAD_DIM)


def gla_chunked(q, k, v, log_a, s0, chunk):
    B, T, H, dk = q.shape
    dv = v.shape[-1]
    n = T // chunk

    def to_chunks(a):
        return a.astype(F32).reshape(B, n, chunk, H, a.shape[-1]).transpose(1, 0, 2, 3, 4)

    qc, kc, vc, gc = to_chunks(q), to_chunks(k), to_chunks(v), to_chunks(log_a)
    b = jnp.cumsum(gc, axis=2)
    b_last = b[:, :, -1:]
    q_dec = qc * jnp.exp(b)
    k_inv = kc * jnp.exp(-b)
    k_tail = kc * jnp.exp(b_last - b)
    causal = np.tril(np.ones((chunk, chunk), dtype=bool))
    a_intra = jnp.where(causal, jnp.einsum('nbihd,nbjhd->nbhij', q_dec, k_inv), 0.0)
    o_intra = jnp.einsum('nbhij,nbjhv->nbihv', a_intra, vc)

    def step(S, xs):
        qd, kt, vv, bl = xs
        o_inter = jnp.einsum('bihd,bhdv->bihv', qd, S)
        S = jnp.exp(bl[:, 0])[..., None] * S + jnp.einsum('bjhd,bjhv->bhdv', kt, vv)
        return S, o_inter

    s_final, o_inter = lax.scan(step, s0.astype(F32), (q_dec, k_tail, vc, b_last))
    o = (o_intra + o_inter).transpose(1, 0, 2, 3, 4).reshape(B, T, H, dv)
    return o, s_final


def mixer_inputs(h, w_in, a_up, a_b):
    B, T, _ = h.shape
    (q, kv, gate, z_a, q_g, k_g, v_g, a_low, z_b, merge) = jnp.split(h @ w_in, _split_offsets(), axis=-1)
    q = q.reshape(B, T, NSA_KV_HEADS, NSA_GROUP, NSA_HEAD_DIM)
    kv = kv.reshape(B, T, 6, NSA_KV_HEADS, NSA_HEAD_DIM)
    gate = jax.nn.sigmoid(gate.astype(F32)).reshape(B, T, 3, NSA_KV_HEADS, NSA_GROUP)
    q_g = q_g.reshape(B, T, GLA_HEADS, GLA_DK) * GLA_DK ** -0.5
    k_g = k_g.reshape(B, T, GLA_HEADS, GLA_DK)
    v_g = v_g.reshape(B, T, GLA_HEADS, GLA_DV)
    log_a = (jax.nn.log_sigmoid((a_low @ a_up + a_b).astype(F32)) / GLA_GATE_TAU).reshape(B, T, GLA_HEADS, GLA_DK)
    return (q, kv[:, :, 0], kv[:, :, 1], kv[:, :, 2], kv[:, :, 3], kv[:, :, 4], kv[:, :, 5],
            gate, z_a, q_g, k_g, v_g, log_a, z_b, merge)


def mixer_outputs(x, o_cmp, o_slc, o_win, gate, z_a, o_gla, z_b, merge, gla_g, w_ba, w_bb, w_out):
    B, T, _ = x.shape
    o_nsa = gate[:, :, 0, :, :, None] * o_cmp + gate[:, :, 1, :, :, None] * o_slc + gate[:, :, 2, :, :, None] * o_win
    o_nsa = o_nsa.reshape(B, T, NSA_WIDTH).astype(x.dtype) * jax.nn.silu(z_a)
    o_g = o_gla * lax.rsqrt(jnp.mean(o_gla * o_gla, axis=-1, keepdims=True) + NORM_EPS)
    o_g = (o_g.astype(x.dtype) * gla_g).reshape(B, T, GLA_VAL_WIDTH) * jax.nn.silu(z_b)
    g_a, g_b = jnp.split(merge, 2, axis=-1)
    mix = jax.nn.sigmoid(g_a) * (o_nsa @ w_ba) + jax.nn.sigmoid(g_b) * (o_g @ w_bb)
    return x + mix @ w_out


def prompt_layer(x, lp, slopes):
    (g_norm, w_in, pk_pe, pk_w1, pk_w2, pv_pe, pv_w1, pv_w2, a_up, a_b, gla_g, w_ba, w_bb, w_out) = lp
    B, T, _ = x.shape
    h = rms_norm(x, g_norm)
    (q, k_cmp, v_cmp, k_slc, v_slc, k_win, v_win, gate, z_a, q_g, k_g, v_g, log_a, z_b, merge) = mixer_inputs(h, w_in, a_up, a_b)
    kc = compress_blocks(k_cmp, pk_pe, pk_w1, pk_w2)
    vc = compress_blocks(v_cmp, pv_pe, pv_w1, pv_w2)
    cmp_end = cmp_block_end(T)
    ks_blocks = to_sel_blocks(k_slc)
    vs_blocks = to_sel_blocks(v_slc)
    nqb = T // Q_BLOCK
    qb = q.reshape(B, nqb, Q_BLOCK, NSA_KV_HEADS, NSA_GROUP, NSA_HEAD_DIM).swapaxes(0, 1)
    posb = jnp.arange(T, dtype=jnp.int32).reshape(nqb, Q_BLOCK)

    def block_fn(args):
        qq, pp = args
        return nsa_cmp_slc(qq, pp, kc, vc, cmp_end, ks_blocks, vs_blocks, slopes)

    o_cmp, o_slc = lax.map(block_fn, (qb, posb))
    o_cmp = o_cmp.swapaxes(0, 1).reshape(B, T, NSA_KV_HEADS, NSA_GROUP, NSA_HEAD_DIM)
    o_slc = o_slc.swapaxes(0, 1).reshape(B, T, NSA_KV_HEADS, NSA_GROUP, NSA_HEAD_DIM)
    o_win = window_prompt(q, k_win, v_win, slopes)
    s0 = jnp.zeros((B, GLA_HEADS, GLA_DK, GLA_DV), F32)
    o_gla, s_gla = gla_chunked(q_g, k_g, v_g, log_a, s0, GLA_CHUNK)
    y = mixer_outputs(x, o_cmp, o_slc, o_win, gate, z_a, o_gla, z_b, merge, gla_g, w_ba, w_bb, w_out)
    keep = min(WINDOW, T)
    return y, (k_cmp, v_cmp, k_slc, v_slc, k_win[:, T - keep:], v_win[:, T - keep:], s_gla)


def gather_pages(pool, page_table):
    rows = pool[page_table]
    return rows.reshape(page_table.shape[0], page_table.shape[1] * pool.shape[1], pool.shape[2], pool.shape[3])


def sample_layer(x, lp, slopes, pk_cmp, pv_cmp, pk_slc, pv_slc, buf_k, buf_v, s_prev, page_table):
    (g_norm, w_in, pk_pe, pk_w1, pk_w2, pv_pe, pv_w1, pv_w2, a_up, a_b, gla_g, w_ba, w_bb, w_out) = lp
    DB, DS, _ = x.shape
    past = page_table.shape[1] * pk_cmp.shape[1]
    h = rms_norm(x, g_norm)
    (q, k_cmp, v_cmp, k_slc, v_slc, k_win, v_win, gate, z_a, q_g, k_g, v_g, log_a, z_b, merge) = mixer_inputs(h, w_in, a_up, a_b)
    q_pos = past + np.arange(DS)
    kc_all = jnp.concatenate([gather_pages(pk_cmp, page_table), k_cmp], axis=1)
    vc_all = jnp.concatenate([gather_pages(pv_cmp, page_table), v_cmp], axis=1)
    L = past + DS
    kc = compress_blocks(kc_all, pk_pe, pk_w1, pk_w2)
    vc = compress_blocks(vc_all, pv_pe, pv_w1, pv_w2)
    ks_blocks = to_sel_blocks(jnp.concatenate([gather_pages(pk_slc, page_table), k_slc], axis=1))
    vs_blocks = to_sel_blocks(jnp.concatenate([gather_pages(pv_slc, page_table), v_slc], axis=1))
    o_cmp, o_slc = nsa_cmp_slc(q, q_pos, kc, vc, cmp_block_end(L), ks_blocks, vs_blocks, slopes)
    w_buf = buf_k.shape[1]
    keys_w = jnp.concatenate([buf_k, k_win], axis=1)
    vals_w = jnp.concatenate([buf_v, v_win], axis=1)
    k_pos = past - w_buf + np.arange(w_buf + DS)
    o_win = window_attend(q, keys_w, vals_w, q_pos, k_pos, slopes)
    o_gla, s_gla = gla_chunked(q_g, k_g, v_g, log_a, s_prev, DS)
    y = mixer_outputs(x, o_cmp, o_slc, o_win, gate, z_a, o_gla, z_b, merge, gla_g, w_ba, w_bb, w_out)
    return y, (k_cmp, v_cmp, k_slc, v_slc, keys_w[:, DS:], vals_w[:, DS:], s_gla)


def setup_inputs(seed: int = 0) -> dict:
    key = jax.random.key(seed)
    ks = jax.random.split(key, 32)
    n_pages = PAST_LEN // PAGE_SIZE
    n_used = DEC_BATCH * n_pages
    n_pool = n_used + n_used // 4
    w_buf = min(WINDOW, PAST_LEN)
    G, hd = NSA_KV_HEADS, NSA_HEAD_DIM

    def nrm(k, shape, scale):
        return scale * jax.random.normal(k, shape, F32)

    pool_shape = (DEPTH, n_pool, PAGE_SIZE, G, hd)
    page_table = jax.random.permutation(ks[0], n_pool)[:n_used].reshape(DEC_BATCH, n_pages).astype(jnp.int32)
    return {
        'x_prompt': nrm(ks[1], (BATCH, SEQ, D_MODEL), 1.0),
        'x_sample': nrm(ks[2], (DEC_BATCH, DEC_SEQ, D_MODEL), 1.0),
        'cache_k_cmp': nrm(ks[3], pool_shape, 1.0),
        'cache_v_cmp': nrm(ks[4], pool_shape, 1.0),
        'cache_k_slc': nrm(ks[5], pool_shape, 1.0),
        'cache_v_slc': nrm(ks[6], pool_shape, 1.0),
        'cache_k_win': nrm(ks[7], (DEPTH, DEC_BATCH, w_buf, G, hd), 1.0),
        'cache_v_win': nrm(ks[8], (DEPTH, DEC_BATCH, w_buf, G, hd), 1.0),
        'state_gla': nrm(ks[9], (DEPTH, DEC_BATCH, GLA_HEADS, GLA_DK, GLA_DV), 1.0),
        'page_table': page_table,
        'norm_g': 1.0 + nrm(ks[10], (DEPTH, D_MODEL), 0.1),
        'w_in': nrm(ks[11], (DEPTH, D_MODEL, D_IN), D_MODEL ** -0.5),
        'phi_k_pe': nrm(ks[12], (DEPTH, CMP_BLOCK, hd), 0.1),
        'phi_k_w1': nrm(ks[13], (DEPTH, CMP_BLOCK * hd, CMP_HIDDEN), (CMP_BLOCK * hd) ** -0.5),
        'phi_k_w2': nrm(ks[14], (DEPTH, CMP_HIDDEN, hd), CMP_HIDDEN ** -0.5),
        'phi_v_pe': nrm(ks[15], (DEPTH, CMP_BLOCK, hd), 0.1),
        'phi_v_w1': nrm(ks[16], (DEPTH, CMP_BLOCK * hd, CMP_HIDDEN), (CMP_BLOCK * hd) ** -0.5),
        'phi_v_w2': nrm(ks[17], (DEPTH, CMP_HIDDEN, hd), CMP_HIDDEN ** -0.5),
        'gla_alpha_up': nrm(ks[18], (DEPTH, GLA_GATE_RANK, GLA_KEY_WIDTH), GLA_GATE_RANK ** -0.5),
        'gla_alpha_b': nrm(ks[19], (DEPTH, GLA_KEY_WIDTH), 0.1),
        'gla_norm_g': 1.0 + nrm(ks[20], (DEPTH, GLA_DV), 0.1),
        'w_branch_a': nrm(ks[21], (DEPTH, NSA_WIDTH, D_MODEL), NSA_WIDTH ** -0.5),
        'w_branch_b': nrm(ks[22], (DEPTH, GLA_VAL_WIDTH, D_MODEL), GLA_VAL_WIDTH ** -0.5),
        'w_out': nrm(ks[23], (DEPTH, D_MODEL, D_MODEL), D_MODEL ** -0.5),
        'final_norm_g': 1.0 + nrm(ks[24], (D_MODEL,), 0.1),
    }


def reference(x_prompt, x_sample, cache_k_cmp, cache_v_cmp, cache_k_slc, cache_v_slc, cache_k_win, cache_v_win,
              state_gla, page_table, norm_g, w_in, phi_k_pe, phi_k_w1, phi_k_w2, phi_v_pe, phi_v_w1, phi_v_w2,
              gla_alpha_up, gla_alpha_b, gla_norm_g, w_branch_a, w_branch_b, w_out, final_norm_g):
    slopes = alibi_slopes()
    y_p, y_s = x_prompt, x_sample
    prompt_new, sample_new = [], []
    for l in range(DEPTH):
        lp = (norm_g[l], w_in[l], phi_k_pe[l], phi_k_w1[l], phi_k_w2[l], phi_v_pe[l], phi_v_w1[l], phi_v_w2[l],
              gla_alpha_up[l], gla_alpha_b[l], gla_norm_g[l], w_branch_a[l], w_branch_b[l], w_out[l])
        y_p, st_p = prompt_layer(y_p, lp, slopes)
        y_s, st_s = sample_layer(y_s, lp, slopes, cache_k_cmp[l], cache_v_cmp[l], cache_k_slc[l], cache_v_slc[l],
                                 cache_k_win[l], cache_v_win[l], state_gla[l], page_table)
        prompt_new.append(st_p)
        sample_new.append(st_s)
    y_prompt = rms_norm(y_p, final_norm_g)
    y_sample = rms_norm(y_s, final_norm_g)
    (p_k_cmp, p_v_cmp, p_k_slc, p_v_slc, p_k_win, p_v_win, p_gla) = [jnp.stack(t) for t in zip(*prompt_new)]
    (s_k_cmp, s_v_cmp, s_k_slc, s_v_slc, s_k_win, s_v_win, s_gla) = [jnp.stack(t) for t in zip(*sample_new)]
    return (y_prompt, y_sample, p_k_cmp, p_v_cmp, p_k_slc, p_v_slc, p_k_win, p_v_win, p_gla,
            s_k_cmp, s_v_cmp, s_k_slc, s_v_slc, s_k_win, s_v_win, s_gla)
```

```python
import functools

import numpy as np
import jax
import jax.numpy as jnp
from jax import lax
from jax.experimental import pallas as pl
from jax.experimental.pallas import tpu as pltpu

F32 = jnp.float32
BF16 = jnp.bfloat16

D_MODEL = 1024
NSA_HEADS = 8
NSA_KV_HEADS = 2
NSA_GROUP = NSA_HEADS // NSA_KV_HEADS
HEAD_DIM = 64
NSA_WIDTH = NSA_HEADS * HEAD_DIM
KV_WIDTH = NSA_KV_HEADS * HEAD_DIM
CMP_BLOCK = 32
CMP_STRIDE = 16
CMP_HIDDEN = 2 * HEAD_DIM
SEL_BLOCK = 64
SEL_TOPN = 16
WINDOW = 512
GLA_HEADS = 4
GLA_KEY_WIDTH = D_MODEL // 2
GLA_VAL_WIDTH = D_MODEL
GLA_DK = GLA_KEY_WIDTH // GLA_HEADS
GLA_DV = GLA_VAL_WIDTH // GLA_HEADS
GLA_GATE_RANK = 16
GLA_GATE_TAU = 16.0
GLA_CHUNK = 32
Q_BLOCK = 128
NORM_EPS = 1e-6
NEG_INF = -1e30
TINY = 1e-30
FORCE_BONUS = 1e4
PADDED_Q = NSA_HEADS * KV_WIDTH

_IN_SIZES = (NSA_WIDTH, 6 * KV_WIDTH, 3 * NSA_HEADS, NSA_WIDTH, GLA_KEY_WIDTH, GLA_KEY_WIDTH,
             GLA_VAL_WIDTH, GLA_GATE_RANK, GLA_VAL_WIDTH, 2 * D_MODEL)
_OFF = tuple(int(o) for o in np.cumsum((0,) + _IN_SIZES))

R_ZA, R_QG, R_KG, R_VG, R_ZB, R_MG = 0, 1024, 1536, 2048, 3072, 4096
R_WIDTH = 6144
GATE_LANE0 = 0
ALOW_LANE0 = 3 * NSA_HEADS

VMEM_LIMIT = 48 * 1024 * 1024


def _cparams(*sem):
    return pltpu.CompilerParams(dimension_semantics=sem, vmem_limit_bytes=VMEM_LIMIT)


def _tile(n, target, mult=8):
    if n <= target:
        return n
    t = (target // mult) * mult
    while t >= mult:
        if n % t == 0:
            return t
        t -= mult
    return n


def _sigmoid(x):
    return 1.0 / (1.0 + jnp.exp(-x))


def _silu(x):
    return x * _sigmoid(x)


def _log_sigmoid(x):
    return -(jnp.maximum(-x, 0.0) + jnp.log1p(jnp.exp(-jnp.abs(x))))


def _dot_nt(a, b):
    return lax.dot_general(a, b, (((1,), (1,)), ((), ())), preferred_element_type=F32)


def _dot(a, b):
    return jnp.dot(a, b, preferred_element_type=F32)


def _norm_kernel(x_ref, g_ref, o_ref):
    xf = x_ref[...]
    xn = xf * lax.rsqrt(jnp.mean(xf * xf, axis=-1, keepdims=True) + NORM_EPS)
    o_ref[...] = (xn * g_ref[...]).astype(o_ref.dtype)


def rms_norm_rows(x, g, out_dtype):
    n, d = x.shape
    tm = _tile(n, 512)
    return pl.pallas_call(
        _norm_kernel,
        grid=(n // tm,),
        in_specs=[pl.BlockSpec((tm, d), lambda i: (i, 0)), pl.BlockSpec((1, d), lambda i: (0, 0))],
        out_specs=pl.BlockSpec((tm, d), lambda i: (i, 0)),
        out_shape=jax.ShapeDtypeStruct((n, d), out_dtype),
        compiler_params=_cparams("parallel"),
        name="rms_norm",
    )(x, g.reshape(1, d))


def _mm_kernel(x_ref, w_ref, *o_refs, scale):
    acc = _dot(x_ref[...], w_ref[...])
    if scale != 1.0:
        acc = acc * scale
    for o_ref in o_refs:
        o_ref[...] = acc.astype(o_ref.dtype).reshape(o_ref.shape)


def matmul_rows(x, w, out_dtypes, *, tn, scale=1.0, slabs=False):
    n, k = x.shape
    m = w.shape[1]
    tm = _tile(n, 512)
    if slabs:
        out_shape = [jax.ShapeDtypeStruct((m // tn, n, tn), dt) for dt in out_dtypes]
        out_specs = [pl.BlockSpec((1, tm, tn), lambda i, j: (j, i, 0)) for _ in out_dtypes]
    else:
        out_shape = [jax.ShapeDtypeStruct((n, m), dt) for dt in out_dtypes]
        out_specs = [pl.BlockSpec((tm, tn), lambda i, j: (i, j)) for _ in out_dtypes]
    return pl.pallas_call(
        functools.partial(_mm_kernel, scale=scale),
        grid=(n // tm, m // tn),
        in_specs=[pl.BlockSpec((tm, k), lambda i, j: (i, 0)), pl.BlockSpec((k, tn), lambda i, j: (0, j))],
        out_specs=out_specs,
        out_shape=out_shape,
        compiler_params=_cparams("parallel", "parallel"),
        name="proj",
    )(x, w)


def _cmp_ab_kernel(x_ref, pea_ref, peb_ref, wa_ref, wb_ref, a_ref, b_ref):
    x = x_ref[...].reshape(x_ref.shape[-2:])
    a_ref[...] = _dot((x + pea_ref[...]).astype(BF16), wa_ref[...])
    b_ref[...] = _dot((x + peb_ref[...]).astype(BF16), wb_ref[...])


def compress_halves(chunks, lead_idx, n_rows, cw):
    pea, peb, wa, wb = cw
    tm = _tile(n_rows, 512)
    width = chunks.shape[-1]
    if chunks.ndim == 3:
        x_spec = pl.BlockSpec((1, tm, width), lambda i: (lead_idx, i, 0))
    else:
        x_spec = pl.BlockSpec((tm, width), lambda i: (i, 0))
    full = lambda shape: pl.BlockSpec(shape, lambda i: (0,) * len(shape))
    hid = wa.shape[1]
    return pl.pallas_call(
        _cmp_ab_kernel,
        grid=(n_rows // tm,),
        in_specs=[x_spec, full(pea.shape), full(peb.shape), full(wa.shape), full(wb.shape)],
        out_specs=[pl.BlockSpec((tm, hid), lambda i: (i, 0))] * 2,
        out_shape=[jax.ShapeDtypeStruct((n_rows, hid), F32)] * 2,
        compiler_params=_cparams("parallel"),
        name="cmp_halves",
    )(chunks, pea, peb, wa, wb)


def _summaries(a, b_next, w2bd):
    return _dot(_silu(a + b_next).astype(BF16), w2bd)


def _shift_up(x):
    n = x.shape[0]
    return pltpu.roll(x, n - 1, 0)


def _pcmp_kernel(ak_ref, bk_ref, av_ref, bv_ref, w2k_ref, w2v_ref, kc_ref, vc_ref):
    kc_ref[0] = _summaries(ak_ref[...], _shift_up(bk_ref[...]), w2k_ref[...]).astype(kc_ref.dtype)
    vc_ref[0] = _summaries(av_ref[...], _shift_up(bv_ref[...]), w2v_ref[...]).astype(vc_ref.dtype)


def prompt_summaries(ak, bk, av, bv, w2k, w2v, batch):
    rows = ak.shape[0] // batch
    hid = ak.shape[1]
    blk = pl.BlockSpec((rows, hid), lambda b: (b, 0))
    wspec = pl.BlockSpec(w2k.shape, lambda b: (0, 0))
    out = pl.BlockSpec((1, rows, KV_WIDTH), lambda b: (b, 0, 0))
    return pl.pallas_call(
        _pcmp_kernel,
        grid=(batch,),
        in_specs=[blk, blk, blk, blk, wspec, wspec],
        out_specs=[out, out],
        out_shape=[jax.ShapeDtypeStruct((batch, rows, KV_WIDTH), BF16)] * 2,
        compiler_params=_cparams("parallel"),
        name="prompt_summaries",
    )(ak, bk, av, bv, w2k, w2v)


def _slope(h):
    return float(2.0 ** (-(h + 1)))


def _select_blocks(score, lane_idx, n_cand, topn):
    rank = jnp.zeros(score.shape, jnp.int32)
    for j in range(n_cand):
        col = score[:, j:j + 1]
        rank = rank + jnp.where(col > score, 1, 0) + jnp.where(col == score, jnp.where(lane_idx > j, 1, 0), 0)
    return jnp.where(rank < topn, 1.0, 0.0)


def _pattn_kernel(q_ref, kc_ref, vc_ref, ks_ref, vs_ref, kw_ref, vw_ref, sm_ref, ov_ref, ex_ref,
                  o_ref, mask_scr, m_scr, l_scr, acc_scr, *, n_sel, topn):
    qi = pl.program_id(1)
    t0 = qi * Q_BLOCK
    q = q_ref[...]
    qs = jnp.concatenate([q[:, h * KV_WIDTH:(h + 1) * KV_WIDTH] for h in range(NSA_HEADS)], axis=0)
    gates = _sigmoid(sm_ref[...])

    def gate_col(branch, h):
        j = GATE_LANE0 + branch * NSA_HEADS + h
        return gates[:, j:j + 1]

    kc = kc_ref[...]
    vc = vc_ref[...]
    n_c = kc.shape[0]
    s_all = _dot_nt(qs, kc)
    tq = lax.broadcasted_iota(jnp.int32, (Q_BLOCK, n_c), 0) + t0
    cend = lax.broadcasted_iota(jnp.int32, (Q_BLOCK, n_c), 1) * CMP_STRIDE + (CMP_BLOCK - 1)
    dist_c = tq - cend
    vis_c = dist_c >= 0
    dist_cf = dist_c.astype(F32)
    psum = [None] * NSA_KV_HEADS
    for h in range(NSA_HEADS):
        g = h // NSA_GROUP
        s = s_all[h * Q_BLOCK:(h + 1) * Q_BLOCK] - _slope(h) * dist_cf
        s = jnp.where(vis_c, s, NEG_INF)
        m = jnp.max(s, axis=-1, keepdims=True)
        p = jnp.where(vis_c, jnp.exp(s - m), 0.0)
        p = p / jnp.maximum(jnp.sum(p, axis=-1, keepdims=True), TINY)
        psum[g] = p if psum[g] is None else psum[g] + p
        o_cmp = _dot(p.astype(BF16), vc)
        o_ref[:, h * KV_WIDTH:(h + 1) * KV_WIDTH] = gate_col(0, h) * o_cmp

    s_idx = lax.broadcasted_iota(jnp.int32, (Q_BLOCK, n_sel), 1)
    tq_s = lax.broadcasted_iota(jnp.int32, (Q_BLOCK, n_sel), 0) + t0
    valid = s_idx * SEL_BLOCK <= tq_s
    forced = (s_idx == 0) | (s_idx == tq_s // SEL_BLOCK)
    for g in range(NSA_KV_HEADS):
        hi = psum[g].astype(BF16)
        lo = (psum[g] - hi.astype(F32)).astype(BF16)
        imp = _dot(hi, ov_ref[...]) + _dot(lo, ov_ref[...])
        score = jnp.where(valid, imp + jnp.where(forced, FORCE_BONUS, 0.0), NEG_INF)
        sel = _select_blocks(score, s_idx, n_sel, topn)
        mask_scr[g] = _dot(sel.astype(BF16), ex_ref[...])

    row_i = lax.broadcasted_iota(jnp.int32, (Q_BLOCK, Q_BLOCK), 0)
    lane_i = lax.broadcasted_iota(jnp.int32, (Q_BLOCK, Q_BLOCK), 1)

    def reset():
        m_scr[...] = jnp.full(m_scr.shape, NEG_INF, F32)
        l_scr[...] = jnp.zeros(l_scr.shape, F32)
        acc_scr[...] = jnp.zeros(acc_scr.shape, F32)

    def flash_step(kb, k_ref, v_ref, selected):
        k0 = pl.multiple_of(kb * Q_BLOCK, Q_BLOCK)
        kblk = k_ref[pl.ds(k0, Q_BLOCK), :]
        vblk = v_ref[pl.ds(k0, Q_BLOCK), :]
        s_all = _dot_nt(qs, kblk)
        dist = (t0 - k0) + (row_i - lane_i)
        dist_f = dist.astype(F32)
        if selected:
            keep = [jnp.where(dist >= 0, mask_scr[g, :, pl.ds(k0, Q_BLOCK)], 0.0) > 0.5
                    for g in range(NSA_KV_HEADS)]
        else:
            keep = [jnp.where(dist >= 0, dist, WINDOW) < WINDOW] * NSA_KV_HEADS
        ps = []
        alphas = []
        for h in range(NSA_HEADS):
            rows = slice(h * Q_BLOCK, (h + 1) * Q_BLOCK)
            kp = keep[h // NSA_GROUP]
            s = jnp.where(kp, s_all[rows] - _slope(h) * dist_f, NEG_INF)
            m_prev = m_scr[rows]
            m_new = jnp.maximum(m_prev, jnp.max(s, axis=-1, keepdims=True))
            p = jnp.where(kp, jnp.exp(s - m_new), 0.0)
            alpha = jnp.exp(m_prev - m_new)
            l_scr[rows] = alpha * l_scr[rows] + jnp.sum(p, axis=-1, keepdims=True)
            m_scr[rows] = m_new
            ps.append(p.astype(BF16))
            alphas.append(alpha)
        pv = _dot(jnp.concatenate(ps, axis=0), vblk)
        acc_scr[...] = jnp.concatenate(alphas, axis=0) * acc_scr[...] + pv

    def finish(branch):
        for h in range(NSA_HEADS):
            rows = slice(h * Q_BLOCK, (h + 1) * Q_BLOCK)
            cols = slice(h * KV_WIDTH, (h + 1) * KV_WIDTH)
            o_b = acc_scr[rows] / jnp.maximum(l_scr[rows], TINY)
            o_ref[:, cols] = o_ref[:, cols] + gate_col(branch, h) * o_b

    reset()
    lax.fori_loop(0, qi + 1, lambda kb, c: (flash_step(kb, ks_ref, vs_ref, True), c)[1], 0)
    finish(1)
    reset()
    w_blocks = WINDOW // Q_BLOCK
    lax.fori_loop(jnp.maximum(qi - w_blocks, 0), qi + 1,
                  lambda kb, c: (flash_step(kb, kw_ref, vw_ref, False), c)[1], 0)
    finish(2)


def prompt_attention(qp, kc, vc, kvb, small, batch, seq):
    nqb = seq // Q_BLOCK
    n_c = kc.shape[1]
    n_sel = seq // SEL_BLOCK
    topn = min(SEL_TOPN, n_sel)
    cs = np.arange(n_c)[:, None] * CMP_STRIDE
    ss = np.arange(n_sel)[None, :] * SEL_BLOCK
    ov = np.clip(np.minimum(cs + CMP_BLOCK, ss + SEL_BLOCK) - np.maximum(cs, ss), 0, None) / CMP_BLOCK
    ov[(seq - CMP_BLOCK) // CMP_STRIDE + 1:] = 0.0
    ex = (np.arange(seq)[None, :] // SEL_BLOCK == np.arange(n_sel)[:, None]).astype(np.float32)
    kv_spec = lambda idx: pl.BlockSpec((None, None, seq, KV_WIDTH), lambda b, i: (idx, b, 0, 0))
    kvb4 = kvb.reshape(7, batch, seq, KV_WIDTH)
    return pl.pallas_call(
        functools.partial(_pattn_kernel, n_sel=n_sel, topn=topn),
        grid=(batch, nqb),
        in_specs=[
            pl.BlockSpec((Q_BLOCK, PADDED_Q), lambda b, i: (b * nqb + i, 0)),
            pl.BlockSpec((None, n_c, KV_WIDTH), lambda b, i: (b, 0, 0)),
            pl.BlockSpec((None, n_c, KV_WIDTH), lambda b, i: (b, 0, 0)),
            kv_spec(2), kv_spec(3), kv_spec(4), kv_spec(5),
            pl.BlockSpec((None, Q_BLOCK, KV_WIDTH), lambda b, i: (6, b * nqb + i, 0)),
            pl.BlockSpec((n_c, n_sel), lambda b, i: (0, 0)),
            pl.BlockSpec((n_sel, seq), lambda b, i: (0, 0)),
        ],
        out_specs=pl.BlockSpec((Q_BLOCK, PADDED_Q), lambda b, i: (b * nqb + i, 0)),
        out_shape=jax.ShapeDtypeStruct((batch * seq, PADDED_Q), F32),
        scratch_shapes=[
            pltpu.VMEM((NSA_KV_HEADS, Q_BLOCK, seq), F32),
            pltpu.VMEM((NSA_HEADS * Q_BLOCK, KV_WIDTH), F32),
            pltpu.VMEM((NSA_HEADS * Q_BLOCK, KV_WIDTH), F32),
            pltpu.VMEM((NSA_HEADS * Q_BLOCK, KV_WIDTH), F32),
        ],
        compiler_params=_cparams("parallel", "arbitrary"),
        name="prompt_nsa",
    )(qp, kc, vc, kvb4, kvb4, kvb4, kvb4, small, jnp.asarray(ov, BF16), jnp.asarray(ex, BF16))


def _gla_log_decay(sm, aup_ref, ab_ref):
    z = _dot(sm.astype(BF16), aup_ref[...]) + ab_ref[...]
    return _log_sigmoid(z) / GLA_GATE_TAU


def _gla_head_norm(o, gg_ref):
    on = o * lax.rsqrt(jnp.mean(o * o, axis=-1, keepdims=True) + NORM_EPS)
    return on * gg_ref[...]


def _pgla_kernel(qg_ref, kg_ref, vg_ref, sm_ref, aup_ref, ab_ref, gg_ref, og_ref, sfin_ref, st_scr, *, n_tiles):
    ti = pl.program_id(1)
    tt = qg_ref.shape[0]
    n_chunks = tt // GLA_CHUNK

    @pl.when(ti == 0)
    def _():
        st_scr[...] = jnp.zeros(st_scr.shape, F32)

    la = _gla_log_decay(sm_ref[...], aup_ref, ab_ref)
    rin = lax.broadcasted_iota(jnp.int32, la.shape, 0) % GLA_CHUNK
    b = la
    sh = 1
    while sh < GLA_CHUNK:
        b = b + jnp.where(rin >= sh, pltpu.roll(b, sh, 0), 0.0)
        sh *= 2
    b3 = b.reshape(n_chunks, GLA_CHUNK, b.shape[-1])
    b_last3 = b3[:, GLA_CHUNK - 1:GLA_CHUNK, :]
    b_last = jnp.broadcast_to(b_last3, b3.shape).reshape(b.shape)
    e_b = jnp.exp(b)
    e_nb = jnp.exp(-b)
    e_tail = jnp.exp(b_last - b)
    e_last = jnp.exp(b_last3)

    r_i = lax.broadcasted_iota(jnp.int32, (tt, tt), 0)
    c_i = lax.broadcasted_iota(jnp.int32, (tt, tt), 1)
    causal = (r_i // GLA_CHUNK == c_i // GLA_CHUNK) & (c_i <= r_i)

    for h in range(GLA_HEADS):
        ks = slice(h * GLA_DK, (h + 1) * GLA_DK)
        vs = slice(h * GLA_DV, (h + 1) * GLA_DV)
        q = qg_ref[:, ks] * (GLA_DK ** -0.5)
        k = kg_ref[:, ks]
        v = vg_ref[:, vs].astype(BF16)
        q_dec = (q * e_b[:, ks]).astype(BF16)
        k_inv = (k * e_nb[:, ks]).astype(BF16)
        k_tail = (k * e_tail[:, ks]).astype(BF16)
        a = jnp.where(causal, _dot_nt(q_dec, k_inv), 0.0)
        o = _dot(a.astype(BF16), v)
        st = st_scr[h]
        o_inter = []
        for c in range(n_chunks):
            rows = slice(c * GLA_CHUNK, (c + 1) * GLA_CHUNK)
            o_inter.append(_dot_nt(q_dec[rows], st.astype(BF16)))
            kv_t = lax.dot_general(v[rows], k_tail[rows], (((0,), (0,)), ((), ())),
                                   preferred_element_type=F32)
            st = st * e_last[c, :, ks] + kv_t
        st_scr[h] = st
        o = o + jnp.concatenate(o_inter, axis=0)
        og_ref[:, vs] = _gla_head_norm(o, gg_ref)

    @pl.when(ti == n_tiles - 1)
    def _():
        for h in range(GLA_HEADS):
            sfin_ref[h] = st_scr[h].T


def prompt_gla(r, small, aup, ab, gg, batch, seq):
    tt = _tile(seq, 256, GLA_CHUNK)
    nt = seq // tt
    row = lambda b, i: b * nt + i
    return pl.pallas_call(
        functools.partial(_pgla_kernel, n_tiles=nt),
        grid=(batch, nt),
        in_specs=[
            pl.BlockSpec((tt, GLA_KEY_WIDTH), lambda b, i: (row(b, i), R_QG // GLA_KEY_WIDTH)),
            pl.BlockSpec((tt, GLA_KEY_WIDTH), lambda b, i: (row(b, i), R_KG // GLA_KEY_WIDTH)),
            pl.BlockSpec((tt, GLA_VAL_WIDTH), lambda b, i: (row(b, i), R_VG // GLA_VAL_WIDTH)),
            pl.BlockSpec((None, tt, KV_WIDTH), lambda b, i: (6, row(b, i), 0)),
            pl.BlockSpec(aup.shape, lambda b, i: (0, 0)),
            pl.BlockSpec(ab.shape, lambda b, i: (0, 0)),
            pl.BlockSpec(gg.shape, lambda b, i: (0, 0)),
        ],
        out_specs=[
            pl.BlockSpec((tt, GLA_VAL_WIDTH), lambda b, i: (row(b, i), 0)),
            pl.BlockSpec((None, GLA_HEADS, GLA_DK, GLA_DV), lambda b, i: (b, 0, 0, 0)),
        ],
        out_shape=[
            jax.ShapeDtypeStruct((batch * seq, GLA_VAL_WIDTH), F32),
            jax.ShapeDtypeStruct((batch, GLA_HEADS, GLA_DK, GLA_DV), F32),
        ],
        scratch_shapes=[pltpu.VMEM((GLA_HEADS, GLA_DV, GLA_DK), F32)],
        compiler_params=_cparams("parallel", "arbitrary"),
        name="prompt_gla",
    )(r, r, r, small, aup, ab, gg)


def _sgla_kernel(qg_ref, kg_ref, vg_ref, sm_ref, aup_ref, ab_ref, gg_ref, s_ref, og_ref, snew_ref, *, ds):
    rows = qg_ref.shape[0]
    nb = rows // ds
    la = _gla_log_decay(sm_ref[...], aup_ref, ab_ref)
    ri = lax.broadcasted_iota(jnp.int32, la.shape, 0) % ds
    b = la
    sh = 1
    while sh < ds:
        b = b + jnp.where(ri >= sh, pltpu.roll(b, sh, 0), 0.0)
        sh *= 2
    b_last = b
    for d in range(1, ds):
        b_last = jnp.where(ri == ds - 1 - d, pltpu.roll(b, rows - d, 0), b_last)
    e_b = jnp.exp(b)
    e_nb = jnp.exp(-b)
    e_tail = jnp.exp(b_last - b)
    e_last = jnp.exp(b_last)

    r_i = lax.broadcasted_iota(jnp.int32, (rows, rows), 0)
    c_i = lax.broadcasted_iota(jnp.int32, (rows, rows), 1)
    causal = (r_i // ds == c_i // ds) & (c_i <= r_i)
    row_b = lax.broadcasted_iota(jnp.int32, (rows, GLA_DV), 0) // ds

    for h in range(GLA_HEADS):
        ks = slice(h * GLA_DK, (h + 1) * GLA_DK)
        vs = slice(h * GLA_DV, (h + 1) * GLA_DV)
        q = qg_ref[:, ks] * (GLA_DK ** -0.5)
        k = kg_ref[:, ks]
        v = vg_ref[:, vs]
        q_dec = (q * e_b[:, ks]).astype(BF16)
        k_inv = (k * e_nb[:, ks]).astype(BF16)
        k_tail_t = (k * e_tail[:, ks]).T
        e_last_t = e_last[:, ks].T
        a = jnp.where(causal, _dot_nt(q_dec, k_inv), 0.0)
        o = _dot(a.astype(BF16), v.astype(BF16))
        for bb in range(nb):
            s_prev = s_ref[bb, h]
            o_inter = _dot(q_dec, s_prev.astype(BF16))
            o = o + jnp.where(row_b == bb, o_inter, 0.0)
            s_new = e_last_t[:, bb * ds:bb * ds + 1] * s_prev
            for j in range(ds):
                rr = bb * ds + j
                s_new = s_new + k_tail_t[:, rr:rr + 1] * v[rr:rr + 1, :]
            snew_ref[bb, h] = s_new
        og_ref[:, vs] = _gla_head_norm(o, gg_ref)


def sample_gla(r, small, aup, ab, gg, state, layer, dec_batch, ds):
    nb = _tile(dec_batch, 8, 1)
    rows = nb * ds
    return pl.pallas_call(
        functools.partial(_sgla_kernel, ds=ds),
        grid=(dec_batch // nb,),
        in_specs=[
            pl.BlockSpec((rows, GLA_KEY_WIDTH), lambda i: (i, R_QG // GLA_KEY_WIDTH)),
            pl.BlockSpec((rows, GLA_KEY_WIDTH), lambda i: (i, R_KG // GLA_KEY_WIDTH)),
            pl.BlockSpec((rows, GLA_VAL_WIDTH), lambda i: (i, R_VG // GLA_VAL_WIDTH)),
            pl.BlockSpec((None, rows, KV_WIDTH), lambda i: (6, i, 0)),
            pl.BlockSpec(aup.shape, lambda i: (0, 0)),
            pl.BlockSpec(ab.shape, lambda i: (0, 0)),
            pl.BlockSpec(gg.shape, lambda i: (0, 0)),
            pl.BlockSpec((None, nb, GLA_HEADS, GLA_DK, GLA_DV), lambda i: (layer, i, 0, 0, 0)),
        ],
        out_specs=[
            pl.BlockSpec((rows, GLA_VAL_WIDTH), lambda i: (i, 0)),
            pl.BlockSpec((nb, GLA_HEADS, GLA_DK, GLA_DV), lambda i: (i, 0, 0, 0)),
        ],
        out_shape=[
            jax.ShapeDtypeStruct((dec_batch * ds, GLA_VAL_WIDTH), F32),
            jax.ShapeDtypeStruct((dec_batch, GLA_HEADS, GLA_DK, GLA_DV), F32),
        ],
        compiler_params=_cparams("parallel"),
        name="sample_gla",
    )(r, r, r, small, aup, ab, gg, state)


def _mix_kernel(x_ref, on_ref, za_ref, og_ref, zb_ref, mg_ref, wba_ref, wbb_ref, wo_ref, fg_ref, *o_refs):
    pa = _dot((on_ref[...] * _silu(za_ref[...])).astype(BF16), wba_ref[...])
    pb = _dot((og_ref[...] * _silu(zb_ref[...])).astype(BF16), wbb_ref[...])
    mix = _sigmoid(mg_ref[:, :D_MODEL]) * pa + _sigmoid(mg_ref[:, D_MODEL:]) * pb
    y = x_ref[...] + _dot(mix.astype(BF16), wo_ref[...])
    o_refs[0][...] = y
    if len(o_refs) > 1:
        yn = y * lax.rsqrt(jnp.mean(y * y, axis=-1, keepdims=True) + NORM_EPS)
        o_refs[1][...] = yn * fg_ref[...]


def mixer_output(x, o_nsa, r, o_gla, wba, wbb, wo, final_g, with_final_norm):
    n = x.shape[0]
    tm = _tile(n, 256)
    row = lambda w: pl.BlockSpec((tm, w), lambda i: (i, 0))
    wspec = lambda w: pl.BlockSpec(w.shape, lambda i: (0, 0))
    n_out = 2 if with_final_norm else 1
    return pl.pallas_call(
        _mix_kernel,
        grid=(n // tm,),
        in_specs=[
            row(D_MODEL), row(PADDED_Q),
            pl.BlockSpec((tm, PADDED_Q), lambda i: (i, R_ZA // PADDED_Q)),
            row(GLA_VAL_WIDTH),
            pl.BlockSpec((tm, GLA_VAL_WIDTH), lambda i: (i, R_ZB // GLA_VAL_WIDTH)),
            pl.BlockSpec((tm, 2 * D_MODEL), lambda i: (i, R_MG // (2 * D_MODEL))),
            wspec(wba), wspec(wbb), wspec(wo), wspec(final_g),
        ],
        out_specs=[row(D_MODEL)] * n_out,
        out_shape=[jax.ShapeDtypeStruct((n, D_MODEL), F32)] * n_out,
        compiler_params=_cparams("parallel"),
        name="mixer_output",
    )(x, o_nsa, r, o_gla, r, r, wba, wbb, wo, final_g)


def _softmax_two(parts, keeps):
    m = functools.reduce(jnp.maximum, [jnp.max(s, axis=-1, keepdims=True) for s in parts])
    ps = [jnp.where(kp, jnp.exp(s - m), 0.0) for s, kp in zip(parts, keeps)]
    total = functools.reduce(lambda a, c: a + c, [jnp.sum(p, axis=-1, keepdims=True) for p in ps])
    return ps, jnp.maximum(total, TINY)


def _smain_kernel(pt_ref, q_ref, gate_ref, slope_ref, *rest, n_pages, past, ds, n_sel, topn):
    del pt_ref
    rest = list(rest)
    take = lambda n: [rest.pop(0) for _ in range(n)]
    ak_refs, bk_refs, av_refs, bv_refs = take(n_pages), take(n_pages), take(n_pages), take(n_pages)
    ks_refs, vs_refs = take(n_pages), take(n_pages)
    kw_ref, vw_ref, new_ref, w2k_ref, w2v_ref, ov_ref, exp_ref, exn_ref = take(8)
    o_ref, kw_out_ref, vw_out_ref = take(3)
    b_par = pl.program_id(0) % 2
    rows = q_ref.shape[0]
    grp_rows = NSA_KV_HEADS * ds
    cat0 = lambda refs: jnp.concatenate([r[...] for r in refs], axis=0)

    qs = q_ref[...]
    slope = slope_ref[:, 0:1]
    kc = _summaries(cat0(ak_refs), _shift_up(cat0(bk_refs)), w2k_ref[...]).astype(BF16)
    vc = _summaries(cat0(av_refs), _shift_up(cat0(bv_refs)), w2v_ref[...]).astype(BF16)
    n_c = kc.shape[0]

    def irow(shape):
        return lax.broadcasted_iota(jnp.int32, shape, 0) % ds

    dist_c = (past + irow((rows, n_c))) - (lax.broadcasted_iota(jnp.int32, (rows, n_c), 1) * CMP_STRIDE
                                          + (CMP_BLOCK - 1))
    vis_c = dist_c >= 0
    s_c = jnp.where(vis_c, _dot_nt(qs, kc) - slope * dist_c.astype(F32), NEG_INF)
    m_c = jnp.max(s_c, axis=-1, keepdims=True)
    p_c = jnp.where(vis_c, jnp.exp(s_c - m_c), 0.0)
    p_c = p_c / jnp.maximum(jnp.sum(p_c, axis=-1, keepdims=True), TINY)
    o_cmp = _dot(p_c.astype(BF16), vc)

    hi = p_c.astype(BF16)
    lo = (p_c - hi.astype(F32)).astype(BF16)
    imp_h = _dot(hi, ov_ref[...]) + _dot(lo, ov_ref[...])
    imp = functools.reduce(lambda a, c: a + c,
                           [imp_h[r * grp_rows:(r + 1) * grp_rows] for r in range(NSA_GROUP)])
    s_idx = lax.broadcasted_iota(jnp.int32, imp.shape, 1)
    qpos = past + irow(imp.shape)
    valid = (s_idx < n_sel) & (s_idx * SEL_BLOCK <= qpos)
    forced = (s_idx == 0) | (s_idx == qpos // SEL_BLOCK)
    score = jnp.where(valid, imp + jnp.where(forced, FORCE_BONUS, 0.0), NEG_INF)
    sel = _select_blocks(score, s_idx, n_sel, topn)
    sel = jnp.concatenate([sel] * NSA_GROUP, axis=0).astype(BF16)
    keep_past = _dot(sel, exp_ref[...]) > 0.5
    sel_new = _dot(sel, exn_ref[...]) > 0.5

    n_new = 2 * ds
    slot = lax.broadcasted_iota(jnp.int32, (rows, n_new), 1)
    dist_n = irow((rows, n_new)) - slot % ds
    mine = (slot // ds == b_par) & (dist_n >= 0)
    bias_n = slope * dist_n.astype(F32)

    def new_rows_pv(p, v_new):
        return functools.reduce(lambda a, c: a + c,
                                [p[:, j:j + 1] * v_new[j:j + 1, :] for j in range(n_new)])

    s_p = jnp.concatenate([_dot_nt(qs, r[...].astype(BF16)) for r in ks_refs], axis=1)
    dist_p = (past + irow(s_p.shape)) - lax.broadcasted_iota(jnp.int32, s_p.shape, 1)
    s_p = jnp.where(keep_past, s_p - slope * dist_p.astype(F32), NEG_INF)
    keep_n = mine & sel_new[:, 0:n_new]
    s_n = jnp.where(keep_n, _dot_nt(qs, new_ref[2].astype(BF16)) - bias_n, NEG_INF)
    (p_p, p_n), total = _softmax_two([s_p, s_n], [keep_past, keep_n])
    acc = new_rows_pv(p_n, new_ref[3])
    for j, r in enumerate(vs_refs):
        acc = acc + _dot(p_p[:, j * Q_BLOCK:(j + 1) * Q_BLOCK].astype(BF16), r[...].astype(BF16))
    o_slc = acc / total

    kbuf = kw_ref[...]
    vbuf = vw_ref[...]
    w_buf = kbuf.shape[0]
    dist_w = (w_buf + irow((rows, w_buf))) - lax.broadcasted_iota(jnp.int32, (rows, w_buf), 1)
    keep_w = dist_w < WINDOW
    s_w = jnp.where(keep_w, _dot_nt(qs, kbuf.astype(BF16)) - slope * dist_w.astype(F32), NEG_INF)
    s_wn = jnp.where(mine, _dot_nt(qs, new_ref[4].astype(BF16)) - bias_n, NEG_INF)
    (p_w, p_wn), total_w = _softmax_two([s_w, s_wn], [keep_w, mine])
    o_win = (_dot(p_w.astype(BF16), vbuf.astype(BF16)) + new_rows_pv(p_wn, new_ref[5])) / total_w

    gates = _sigmoid(gate_ref[...])
    o_ref[...] = gates[:, 0:1] * o_cmp + gates[:, 1:2] * o_slc + gates[:, 2:3] * o_win

    row8 = lax.broadcasted_iota(jnp.int32, (n_new, KV_WIDTH), 0)
    for buf, slab, out_ref in ((kbuf, 4, kw_out_ref), (vbuf, 5, vw_out_ref)):
        rolled = pltpu.roll(buf, w_buf - ds, 0)
        tile = new_ref[slab]
        mine_last = jnp.where(b_par == 0, pltpu.roll(tile, ds, 0), tile)
        out_ref[0:w_buf - n_new, :] = rolled[0:w_buf - n_new]
        out_ref[w_buf - n_new:w_buf, :] = jnp.where(row8 >= n_new - ds, mine_last, rolled[w_buf - n_new:w_buf])


def sample_attention(page_table, q_rows, gate_rows, ab_pages, slc_pools, win_bufs, new_rows, w2k, w2v, layer, past, ds):
    assert ds == 4 and past % SEL_BLOCK == 0 and past % Q_BLOCK == 0
    dec_batch, n_pages = page_table.shape
    rows = q_rows.shape[1]
    w_buf = win_bufs[0].shape[2]
    n_c = n_pages * (Q_BLOCK // CMP_STRIDE)
    n_cmp = (past + ds - CMP_BLOCK) // CMP_STRIDE + 1
    n_sel = -(-(past + ds) // SEL_BLOCK)
    topn = min(SEL_TOPN, n_sel)
    assert n_sel <= KV_WIDTH
    cs = np.arange(n_c)[:, None] * CMP_STRIDE
    ss = np.arange(KV_WIDTH)[None, :] * SEL_BLOCK
    ov = np.clip(np.minimum(cs + CMP_BLOCK, ss + SEL_BLOCK) - np.maximum(cs, ss), 0, None) / CMP_BLOCK
    ov[n_cmp:] = 0.0
    ov[:, n_sel:] = 0.0
    ex_past = (np.arange(past)[None, :] // SEL_BLOCK == np.arange(KV_WIDTH)[:, None]).astype(np.float32)
    ex_new = np.zeros((KV_WIDTH, KV_WIDTH), np.float32)
    ex_new[:, :2 * ds] = (past + np.arange(2 * ds) % ds)[None, :] // SEL_BLOCK == np.arange(KV_WIDTH)[:, None]
    head = np.arange(rows) // (NSA_KV_HEADS * ds) + NSA_GROUP * ((np.arange(rows) // ds) % NSA_KV_HEADS)
    slope = np.broadcast_to((2.0 ** -(head + 1.0))[:, None], (rows, KV_WIDTH)).astype(np.float32)

    const = lambda shape: pl.BlockSpec(shape, lambda b, pt: (0,) * len(shape))
    page = lambda j, shape: pl.BlockSpec((None,) + shape, lambda b, pt: (pt[b, j], 0, 0))
    pool = lambda j: pl.BlockSpec((None, None, Q_BLOCK, KV_WIDTH), lambda b, pt: (layer, pt[b, j], 0, 0))
    hid = ab_pages[0].shape[-1]
    in_specs = [pl.BlockSpec((None, rows, KV_WIDTH), lambda b, pt: (b, 0, 0)),
                pl.BlockSpec((None, rows, KV_WIDTH), lambda b, pt: (b, 0, 0)),
                const((rows, KV_WIDTH))]
    operands = [q_rows, gate_rows, jnp.asarray(slope)]
    for arr in ab_pages:
        in_specs += [page(j, (Q_BLOCK // CMP_STRIDE, hid)) for j in range(n_pages)]
        operands += [arr] * n_pages
    for arr in slc_pools:
        in_specs += [pool(j) for j in range(n_pages)]
        operands += [arr] * n_pages
    for arr in win_bufs:
        in_specs.append(pl.BlockSpec((None, None, w_buf, KV_WIDTH), lambda b, pt: (layer, b, 0, 0)))
        operands.append(arr)
    in_specs.append(pl.BlockSpec((7, 2 * ds, KV_WIDTH), lambda b, pt: (0, b // 2, 0)))
    operands.append(new_rows)
    consts = [w2k, w2v, jnp.asarray(ov, BF16), jnp.asarray(ex_past, BF16), jnp.asarray(ex_new, BF16)]
    in_specs += [const(c.shape) for c in consts]
    operands += consts
    buf_out = pl.BlockSpec((None, w_buf, KV_WIDTH), lambda b, pt: (b, 0, 0))
    return pl.pallas_call(
        functools.partial(_smain_kernel, n_pages=n_pages, past=past, ds=ds, n_sel=n_sel, topn=topn),
        grid_spec=pltpu.PrefetchScalarGridSpec(
            num_scalar_prefetch=1,
            grid=(dec_batch,),
            in_specs=in_specs,
            out_specs=[pl.BlockSpec((None, rows, KV_WIDTH), lambda b, pt: (b, 0, 0)), buf_out, buf_out],
        ),
        out_shape=[jax.ShapeDtypeStruct((dec_batch, rows, KV_WIDTH), F32),
                   jax.ShapeDtypeStruct((dec_batch, w_buf, KV_WIDTH), F32),
                   jax.ShapeDtypeStruct((dec_batch, w_buf, KV_WIDTH), F32)],
        compiler_params=_cparams("arbitrary"),
        name="sample_nsa",
    )(page_table, *operands)


def _pad_heads_cols(w):
    k = w.shape[0]
    w4 = w.reshape(k, NSA_KV_HEADS, NSA_GROUP, HEAD_DIM)
    eye = jnp.eye(NSA_KV_HEADS, dtype=w.dtype)
    return jnp.einsum("kgrd,gp->kgrpd", w4, eye).reshape(k, PADDED_Q)


def _layer_weights(w_in, pk_pe, pk_w1, pk_w2, pv_pe, pv_w1, pv_w2, a_up, a_b, gla_g, w_ba, w_bb, w_out):
    o = _OFF
    col = lambda i: w_in[:, o[i]:o[i + 1]]
    small = jnp.concatenate([col(2), col(7), jnp.zeros((D_MODEL, KV_WIDTH - 3 * NSA_HEADS - GLA_GATE_RANK), F32)], axis=1)
    w_q = _pad_heads_cols(col(0)).astype(BF16)
    w_kv = jnp.concatenate([col(1), small], axis=1).astype(BF16)
    w_r = jnp.concatenate([_pad_heads_cols(col(3)), col(4), col(5), col(6), col(8), col(9)], axis=1).astype(BF16)
    half = CMP_BLOCK // 2

    def cmp_weights(pe, w1, w2):
        eye = jnp.eye(NSA_KV_HEADS, dtype=F32)
        w1h = w1.reshape(2, half, HEAD_DIM, CMP_HIDDEN)
        big = lambda w: jnp.einsum("ldh,gk->lgdkh", w, eye).reshape(half * KV_WIDTH, NSA_KV_HEADS * CMP_HIDDEN)
        peh = pe.reshape(2, half, 1, HEAD_DIM)
        pe_row = lambda p: jnp.broadcast_to(p, (half, NSA_KV_HEADS, HEAD_DIM)).reshape(1, half * KV_WIDTH)
        w2bd = jnp.einsum("hd,gk->ghkd", w2, eye).reshape(NSA_KV_HEADS * CMP_HIDDEN, KV_WIDTH)
        return (pe_row(peh[0]), pe_row(peh[1]), big(w1h[0]).astype(BF16), big(w1h[1]).astype(BF16)), w2bd.astype(BF16)

    cw_k, w2k = cmp_weights(pk_pe, pk_w1, pk_w2)
    cw_v, w2v = cmp_weights(pv_pe, pv_w1, pv_w2)
    aup = jnp.zeros((KV_WIDTH, GLA_KEY_WIDTH), F32).at[ALOW_LANE0:ALOW_LANE0 + GLA_GATE_RANK].set(a_up).astype(BF16)
    w_ba_p = _pad_heads_cols(w_ba.T).T.astype(BF16)
    return dict(w_q=w_q, w_kv=w_kv, w_r=w_r, cw_k=cw_k, cw_v=cw_v, w2k=w2k, w2v=w2v, aup=aup,
                ab=a_b.reshape(1, GLA_KEY_WIDTH), gg=gla_g.reshape(1, GLA_DV),
                w_ba=w_ba_p, w_bb=w_bb.astype(BF16), w_out=w_out.astype(BF16))


def _project(x, g_norm, lw):
    h = rms_norm_rows(x, g_norm, BF16)
    (qp,) = matmul_rows(h, lw["w_q"], [BF16], tn=512, scale=HEAD_DIM ** -0.5)
    kv_f, kv_b = matmul_rows(h, lw["w_kv"], [F32, BF16], tn=KV_WIDTH, slabs=True)
    (r,) = matmul_rows(h, lw["w_r"], [F32], tn=512)
    return qp, kv_f, kv_b, r


def kernel(x_prompt, x_sample, cache_k_cmp, cache_v_cmp, cache_k_slc, cache_v_slc, cache_k_win, cache_v_win, state_gla, page_table, norm_g, w_in, phi_k_pe, phi_k_w1, phi_k_w2, phi_v_pe, phi_v_w1, phi_v_w2, gla_alpha_up, gla_alpha_b, gla_norm_g, w_branch_a, w_branch_b, w_out, final_norm_g):
    batch, seq, _ = x_prompt.shape
    dec_batch, ds, _ = x_sample.shape
    depth = norm_g.shape[0]
    n_pool, page_size = cache_k_cmp.shape[1:3]
    assert page_size == Q_BLOCK
    n_pages = page_table.shape[1]
    past = n_pages * page_size
    w_buf = cache_k_win.shape[2]
    chunks_per_page = page_size // CMP_STRIDE
    final_g = final_norm_g.reshape(1, D_MODEL)

    pool_chunks = lambda c: c.reshape(depth, n_pool * chunks_per_page, CMP_STRIDE * KV_WIDTH)
    pool_rows = lambda c: c.reshape(depth, n_pool, page_size, KV_WIDTH)
    kc_chunks, vc_chunks = pool_chunks(cache_k_cmp), pool_chunks(cache_v_cmp)
    slc_pools = (pool_rows(cache_k_slc), pool_rows(cache_v_slc))
    win_bufs = (cache_k_win.reshape(depth, dec_batch, w_buf, KV_WIDTH), cache_v_win.reshape(depth, dec_batch, w_buf, KV_WIDTH))

    y_p = x_prompt.reshape(batch * seq, D_MODEL)
    y_s = x_sample.reshape(dec_batch * ds, D_MODEL)
    outs_p, outs_s = [], []
    for l in range(depth):
        lw = _layer_weights(w_in[l], phi_k_pe[l], phi_k_w1[l], phi_k_w2[l], phi_v_pe[l], phi_v_w1[l], phi_v_w2[l],
                            gla_alpha_up[l], gla_alpha_b[l], gla_norm_g[l], w_branch_a[l], w_branch_b[l], w_out[l])
        last = l == depth - 1

        qp, kv_f, kv_b, r = _project(y_p, norm_g[l], lw)
        n_chunk_rows = batch * seq // CMP_STRIDE
        chunk_view = kv_f.reshape(7, n_chunk_rows, CMP_STRIDE * KV_WIDTH)
        ak, bk = compress_halves(chunk_view, 0, n_chunk_rows, lw["cw_k"])
        av, bv = compress_halves(chunk_view, 1, n_chunk_rows, lw["cw_v"])
        kc, vc = prompt_summaries(ak, bk, av, bv, lw["w2k"], lw["w2v"], batch)
        o_nsa = prompt_attention(qp, kc, vc, kv_b, kv_f, batch, seq)
        o_gla, s_gla = prompt_gla(r, kv_f, lw["aup"], lw["ab"], lw["gg"], batch, seq)
        res = mixer_output(y_p, o_nsa, r, o_gla, lw["w_ba"], lw["w_bb"], lw["w_out"], final_g, last)
        y_p = res[0]
        if last:
            y_p_out = res[1]
        keep = min(WINDOW, seq)
        kv5 = kv_f[:6].reshape(6, batch, seq, NSA_KV_HEADS, HEAD_DIM)
        outs_p.append((kv5[0], kv5[1], kv5[2], kv5[3], kv5[4][:, seq - keep:], kv5[5][:, seq - keep:], s_gla))

        qs, kvs_f, _, rs = _project(y_s, norm_g[l], lw)
        n_pool_rows = n_pool * chunks_per_page
        ab_pages = []
        for chunks, cw in ((kc_chunks, lw["cw_k"]), (vc_chunks, lw["cw_v"])):
            a_half, b_half = compress_halves(chunks, l, n_pool_rows, cw)
            ab_pages += [a_half.reshape(n_pool, chunks_per_page, -1), b_half.reshape(n_pool, chunks_per_page, -1)]
        q_rows = qs.reshape(dec_batch, ds, NSA_KV_HEADS, NSA_GROUP, KV_WIDTH).transpose(0, 3, 2, 1, 4)
        q_rows = q_rows.reshape(dec_batch, NSA_HEADS * ds, KV_WIDTH)
        gl = kvs_f[6][:, GATE_LANE0:GATE_LANE0 + 3 * NSA_HEADS].reshape(dec_batch, ds, 3, NSA_KV_HEADS, NSA_GROUP)
        gl = gl.transpose(0, 4, 3, 1, 2).reshape(dec_batch, NSA_HEADS * ds, 3)
        gate_rows = jnp.pad(gl, ((0, 0), (0, 0), (0, KV_WIDTH - 3)))
        o_rows, kw_new, vw_new = sample_attention(page_table, q_rows, gate_rows, ab_pages, slc_pools, win_bufs,
                                                  kvs_f, lw["w2k"], lw["w2v"], l, past, ds)
        o_nsa_s = o_rows.reshape(dec_batch, NSA_GROUP, NSA_KV_HEADS, ds, KV_WIDTH).transpose(0, 3, 2, 1, 4)
        o_nsa_s = o_nsa_s.reshape(dec_batch * ds, PADDED_Q)
        o_gla_s, s_gla_s = sample_gla(rs, kvs_f, lw["aup"], lw["ab"], lw["gg"], state_gla, l, dec_batch, ds)
        res = mixer_output(y_s, o_nsa_s, rs, o_gla_s, lw["w_ba"], lw["w_bb"], lw["w_out"], final_g, last)
        y_s = res[0]
        if last:
            y_s_out = res[1]
        kvs5 = kvs_f[:6].reshape(6, dec_batch, ds, NSA_KV_HEADS, HEAD_DIM)
        buf5 = lambda a: a.reshape(dec_batch, w_buf, NSA_KV_HEADS, HEAD_DIM)
        outs_s.append((kvs5[0], kvs5[1], kvs5[2], kvs5[3], buf5(kw_new), buf5(vw_new), s_gla_s))

    stack = lambda outs: [jnp.stack(t) for t in zip(*outs)]
    return (y_p_out.reshape(batch, seq, D_MODEL), y_s_out.reshape(dec_batch, ds, D_MODEL),
            *stack(outs_p), *stack(outs_s))
```

```python
import functools

import numpy as np
import jax
import jax.numpy as jnp
from jax import lax
from jax.experimental import pallas as pl
from jax.experimental.pallas import tpu as pltpu

F32 = jnp.float32
BF16 = jnp.bfloat16

D_MODEL = 1024
NSA_HEADS = 8
NSA_KV_HEADS = 2
NSA_GROUP = NSA_HEADS // NSA_KV_HEADS
HEAD_DIM = 64
NSA_WIDTH = NSA_HEADS * HEAD_DIM
KV_WIDTH = NSA_KV_HEADS * HEAD_DIM
CMP_BLOCK = 32
CMP_STRIDE = 16
CMP_HIDDEN = 2 * HEAD_DIM
SEL_BLOCK = 64
SEL_TOPN = 16
WINDOW = 512
GLA_HEADS = 4
GLA_KEY_WIDTH = D_MODEL // 2
GLA_VAL_WIDTH = D_MODEL
GLA_DK = GLA_KEY_WIDTH // GLA_HEADS
GLA_DV = GLA_VAL_WIDTH // GLA_HEADS
GLA_GATE_RANK = 16
GLA_GATE_TAU = 16.0
GLA_CHUNK = 32
Q_BLOCK = 128
KEY_BLOCK = 256
NORM_EPS = 1e-6
NEG_INF = -1e30
TINY = 1e-30
FORCE_BONUS = 1e4
MASK_BIG = float(2.0 ** 100)
PADDED_Q = NSA_HEADS * KV_WIDTH
LANES = 128

_IN_SIZES = (NSA_WIDTH, 6 * KV_WIDTH, 3 * NSA_HEADS, NSA_WIDTH, GLA_KEY_WIDTH, GLA_KEY_WIDTH,
             GLA_VAL_WIDTH, GLA_GATE_RANK, GLA_VAL_WIDTH, 2 * D_MODEL)
_OFF = tuple(int(o) for o in np.cumsum((0,) + _IN_SIZES))

R_ZA, R_QG, R_KG, R_VG, R_ZB, R_MG, R_SMALL = 0, 1024, 1536, 2048, 3072, 4096, 6144
R_WIDTH = R_SMALL + LANES
GATE_LANE0 = 0
ALOW_LANE0 = 3 * NSA_HEADS
FEAT_POS_HI, FEAT_POS_LO, FEAT_ONE, FEAT_SEL0 = 0, 1, 2, 64

VMEM_LIMIT = 48 * 1024 * 1024


def _cparams(*sem):
    return pltpu.CompilerParams(dimension_semantics=sem, vmem_limit_bytes=VMEM_LIMIT)


def _tile(n, target, mult=8):
    if n <= target:
        return n
    t = (target // mult) * mult
    while t >= mult:
        if n % t == 0:
            return t
        t -= mult
    return n


def _sigmoid(x):
    return 1.0 / (1.0 + jnp.exp(-x))


def _silu(x):
    return x * _sigmoid(x)


def _log_sigmoid(x):
    return -(jnp.maximum(-x, 0.0) + jnp.log1p(jnp.exp(-jnp.abs(x))))


def _dot_nt(a, b):
    return lax.dot_general(a, b, (((1,), (1,)), ((), ())), preferred_element_type=F32)


def _dot(a, b):
    return jnp.dot(a, b, preferred_element_type=F32)


def _slope(h):
    return float(2.0 ** (-(h + 1)))


def _norm_kernel(x_ref, g_ref, o_ref):
    xf = x_ref[...]
    xn = xf * lax.rsqrt(jnp.mean(xf * xf, axis=-1, keepdims=True) + NORM_EPS)
    o_ref[...] = (xn * g_ref[...]).astype(o_ref.dtype)


def rms_norm_rows(x, g, out_dtype):
    n, d = x.shape
    tm = _tile(n, 512)
    return pl.pallas_call(
        _norm_kernel,
        grid=(n // tm,),
        in_specs=[pl.BlockSpec((tm, d), lambda i: (i, 0)), pl.BlockSpec((1, d), lambda i: (0, 0))],
        out_specs=pl.BlockSpec((tm, d), lambda i: (i, 0)),
        out_shape=jax.ShapeDtypeStruct((n, d), out_dtype),
        compiler_params=_cparams("parallel"),
        name="rms_norm",
    )(x, g.reshape(1, d))


def _mm_kernel(x_ref, w_ref, o_ref, *, scale):
    acc = _dot(x_ref[...], w_ref[...])
    if scale != 1.0:
        acc = acc * scale
    o_ref[...] = acc.astype(o_ref.dtype)


def matmul_rows(x, w, out_dtype, *, tn, scale=1.0):
    n, k = x.shape
    m = w.shape[1]
    tm = _tile(n, 512)
    return pl.pallas_call(
        functools.partial(_mm_kernel, scale=scale),
        grid=(n // tm, m // tn),
        in_specs=[pl.BlockSpec((tm, k), lambda i, j: (i, 0)), pl.BlockSpec((k, tn), lambda i, j: (0, j))],
        out_specs=pl.BlockSpec((tm, tn), lambda i, j: (i, j)),
        out_shape=jax.ShapeDtypeStruct((n, m), out_dtype),
        compiler_params=_cparams("parallel", "parallel"),
        name="proj",
    )(x, w)


def _mm_t_kernel(wt_ref, x_ref, of_ref, ob_ref):
    acc = _dot_nt(wt_ref[...], x_ref[...])
    acc = acc.reshape(of_ref.shape)
    of_ref[...] = acc
    ob_ref[...] = acc.astype(ob_ref.dtype)


def kv_project_t(x, wt, batch, seq):
    k = x.shape[1]
    tm = _tile(seq, 512, LANES)
    nt = seq // tm
    n_slab = wt.shape[0] // KV_WIDTH
    out = pl.BlockSpec((n_slab, None, KV_WIDTH, tm), lambda b, i: (0, b, 0, i))
    return pl.pallas_call(
        _mm_t_kernel,
        grid=(batch, nt),
        in_specs=[pl.BlockSpec(wt.shape, lambda b, i: (0, 0)), pl.BlockSpec((tm, k), lambda b, i: (b * nt + i, 0))],
        out_specs=[out, out],
        out_shape=[jax.ShapeDtypeStruct((n_slab, batch, KV_WIDTH, seq), F32),
                   jax.ShapeDtypeStruct((n_slab, batch, KV_WIDTH, seq), BF16)],
        compiler_params=_cparams("parallel", "parallel"),
        name="proj_kv_t",
    )(wt, x)


def _chunk_rows(xs_ref, n_chunks):
    return jnp.concatenate([xs_ref[pl.ds(l, n_chunks, stride=CMP_STRIDE), :] for l in range(CMP_STRIDE)], axis=1)


def _halves(chunks, pea_ref, peb_ref, wa_ref, wb_ref):
    a = _dot((chunks + pea_ref[...]).astype(BF16), wa_ref[...])
    b = _dot((chunks + peb_ref[...]).astype(BF16), wb_ref[...])
    return a, b


def _summaries(a, b_next, w2bd):
    return _dot(_silu(a + b_next).astype(BF16), w2bd)


def _shift_up(x):
    n = x.shape[0]
    return pltpu.roll(x, n - 1, 0)


def _transpose_pages(src, xs_ref, n_pages):
    for p in range(n_pages):
        xs_ref[p * LANES:(p + 1) * LANES, :] = src(p).T


def _pool_ab_kernel(x_ref, pea_ref, peb_ref, wa_ref, wb_ref, a_ref, b_ref, xs_scr):
    n_pages = x_ref.shape[0]
    _transpose_pages(lambda p: x_ref[p], xs_scr, n_pages)
    chunks = _chunk_rows(xs_scr, n_pages * (LANES // CMP_STRIDE))
    a_ref[...], b_ref[...] = _halves(chunks, pea_ref, peb_ref, wa_ref, wb_ref)


def pool_halves(pool_t, layer, cw):
    pea, peb, wa, wb = cw
    n_pool = pool_t.shape[1]
    pg = _tile(n_pool, 16, 1)
    rows = pg * (LANES // CMP_STRIDE)
    hid = wa.shape[1]
    full = lambda a: pl.BlockSpec(a.shape, lambda i: (0,) * a.ndim)
    return pl.pallas_call(
        _pool_ab_kernel,
        grid=(n_pool // pg,),
        in_specs=[pl.BlockSpec((None, pg, KV_WIDTH, LANES), lambda i: (layer, i, 0, 0)),
                  full(pea), full(peb), full(wa), full(wb)],
        out_specs=[pl.BlockSpec((rows, hid), lambda i: (i, 0))] * 2,
        out_shape=[jax.ShapeDtypeStruct((n_pool * (LANES // CMP_STRIDE), hid), F32)] * 2,
        scratch_shapes=[pltpu.VMEM((pg * LANES, KV_WIDTH), F32)],
        compiler_params=_cparams("parallel"),
        name="pool_halves",
    )(pool_t, pea, peb, wa, wb)


def _pcmp_kernel(kt_ref, vt_ref, kpea, kpeb, kwa, kwb, kw2, vpea, vpeb, vwa, vwb, vw2, kct_ref, vc_ref, xs_scr):
    seq = kt_ref.shape[1]
    n_blocks = seq // LANES
    n_chunks = seq // CMP_STRIDE
    for src_ref, (pea, peb, wa, wb, w2), is_key in ((kt_ref, (kpea, kpeb, kwa, kwb, kw2), True),
                                                   (vt_ref, (vpea, vpeb, vwa, vwb, vw2), False)):
        _transpose_pages(lambda p: src_ref[:, p * LANES:(p + 1) * LANES], xs_scr, n_blocks)
        a, b = _halves(_chunk_rows(xs_scr, n_chunks), pea, peb, wa, wb)
        rows = _summaries(a, _shift_up(b), w2[...])
        if is_key:
            kct_ref[...] = rows.T.astype(kct_ref.dtype)
        else:
            vc_ref[...] = rows.astype(vc_ref.dtype)


def prompt_summaries(kvt_f, cw_k, w2k, cw_v, w2v):
    _, batch, _, seq = kvt_f.shape
    n_c = seq // CMP_STRIDE
    consts = [*cw_k, w2k, *cw_v, w2v]
    full = lambda a: pl.BlockSpec(a.shape, lambda b: (0,) * a.ndim)
    return pl.pallas_call(
        _pcmp_kernel,
        grid=(batch,),
        in_specs=[pl.BlockSpec((None, None, KV_WIDTH, seq), lambda b: (0, b, 0, 0)),
                  pl.BlockSpec((None, None, KV_WIDTH, seq), lambda b: (1, b, 0, 0))] + [full(c) for c in consts],
        out_specs=[pl.BlockSpec((None, KV_WIDTH, n_c), lambda b: (b, 0, 0)),
                   pl.BlockSpec((None, n_c, KV_WIDTH), lambda b: (b, 0, 0))],
        out_shape=[jax.ShapeDtypeStruct((batch, KV_WIDTH, n_c), BF16),
                   jax.ShapeDtypeStruct((batch, n_c, KV_WIDTH), BF16)],
        scratch_shapes=[pltpu.VMEM((seq, KV_WIDTH), F32)],
        compiler_params=_cparams("parallel"),
        name="prompt_summaries",
    )(kvt_f, kvt_f, *consts)


def _feature_rows(seq):
    kpos = np.arange(seq)
    f = np.zeros((LANES, seq), np.float32)
    f[FEAT_POS_HI] = kpos // SEL_BLOCK
    f[FEAT_POS_LO] = kpos % SEL_BLOCK
    f[FEAT_ONE] = 1.0
    n_sel = seq // SEL_BLOCK
    f[FEAT_SEL0:FEAT_SEL0 + n_sel] = kpos[None, :] // SEL_BLOCK == np.arange(n_sel)[:, None]
    return f


def _rank_select_t(score_t, n_cand, topn):
    idx = lax.broadcasted_iota(jnp.int32, score_t.shape, 0)
    rank = jnp.zeros(score_t.shape, jnp.int32)
    for j in range(n_cand):
        row = score_t[j:j + 1, :]
        rank = rank + jnp.where(idx > j, jnp.where(row >= score_t, 1, 0), jnp.where(row > score_t, 1, 0))
    return jnp.where(rank < topn, 1.0, 0.0)


def _pattn_kernel(q_ref, kct_ref, vc_ref, kst_ref, vst_ref, kwt_ref, vwt_ref, feat_ref, sm_ref, ovt_ref,
                  o_ref, kfs_scr, kfw_scr, m_scr, l_scr, acc_scr, *, n_sel, topn):
    qi = pl.program_id(1)
    t0 = qi * Q_BLOCK

    @pl.when(qi == 0)
    def _():
        kfs_scr[0:KV_WIDTH, :] = kst_ref[...]
        kfs_scr[KV_WIDTH:, :] = feat_ref[...]
        kfw_scr[0:KV_WIDTH, :] = kwt_ref[...]
        kfw_scr[KV_WIDTH:, :] = feat_ref[...]

    q = q_ref[...]
    qs = jnp.concatenate([q[:, h * KV_WIDTH:(h + 1) * KV_WIDTH] for h in range(NSA_HEADS)], axis=0)
    gates = _sigmoid(sm_ref[...])

    def gate_col(branch, h):
        j = GATE_LANE0 + branch * NSA_HEADS + h
        return gates[:, j:j + 1]

    n_c = kct_ref.shape[1]
    s_all = _dot(qs, kct_ref[...])
    tq = lax.broadcasted_iota(jnp.int32, (Q_BLOCK, n_c), 0) + t0
    cend = lax.broadcasted_iota(jnp.int32, (Q_BLOCK, n_c), 1) * CMP_STRIDE + (CMP_BLOCK - 1)
    dist_c = tq - cend
    vis_c = dist_c >= 0
    dist_cf = dist_c.astype(F32)
    vc = vc_ref[...]
    psum = [None] * NSA_KV_HEADS
    for h in range(NSA_HEADS):
        g = h // NSA_GROUP
        s = s_all[h * Q_BLOCK:(h + 1) * Q_BLOCK] - _slope(h) * dist_cf
        s = jnp.where(vis_c, s, NEG_INF)
        m = jnp.max(s, axis=-1, keepdims=True)
        p = jnp.where(vis_c, jnp.exp(s - m), 0.0)
        p = p / jnp.maximum(jnp.sum(p, axis=-1, keepdims=True), TINY)
        psum[g] = p if psum[g] is None else psum[g] + p
        o_ref[:, h * KV_WIDTH:(h + 1) * KV_WIDTH] = gate_col(0, h) * _dot(p.astype(BF16), vc)

    s_idx = lax.broadcasted_iota(jnp.int32, (n_sel, Q_BLOCK), 0)
    tq_s = lax.broadcasted_iota(jnp.int32, (n_sel, Q_BLOCK), 1) + t0
    valid = s_idx * SEL_BLOCK <= tq_s
    forced = (s_idx == 0) | (s_idx == tq_s // SEL_BLOCK)
    pad_lo = jnp.zeros((FEAT_SEL0, Q_BLOCK), F32)
    pad_hi = jnp.zeros((LANES - FEAT_SEL0 - n_sel, Q_BLOCK), F32)
    sel_lanes = []
    for g in range(NSA_KV_HEADS):
        hi = psum[g].astype(BF16)
        lo = (psum[g] - hi.astype(F32)).astype(BF16)
        imp_t = _dot_nt(ovt_ref[...], hi) + _dot_nt(ovt_ref[...], lo)
        score_t = jnp.where(valid, imp_t + jnp.where(forced, FORCE_BONUS, 0.0), NEG_INF)
        sel_t = _rank_select_t(score_t, n_sel, topn)
        parts = [pad_lo, sel_t] + ([pad_hi] if pad_hi.shape[0] else [])
        sel_lanes.append(jnp.concatenate(parts, axis=0).T)

    lane = lax.broadcasted_iota(jnp.int32, (Q_BLOCK, LANES), 1)
    t0f = t0.astype(F32)

    def feat_coeffs(h, selected):
        sl = _slope(h)
        c = jnp.where(lane == FEAT_POS_HI, sl * SEL_BLOCK,
                      jnp.where(lane == FEAT_POS_LO, sl, jnp.where(lane == FEAT_ONE, -sl * t0f, 0.0)))
        if selected:
            c = jnp.where(lane >= FEAT_SEL0, (sel_lanes[h // NSA_GROUP] - 1.0) * MASK_BIG, c)
        return c.astype(BF16)

    def q_operand(selected):
        feats = jnp.concatenate([feat_coeffs(h, selected) for h in range(NSA_HEADS)], axis=0)
        return jnp.concatenate([qs, feats], axis=1)

    row_i = lax.broadcasted_iota(jnp.int32, (Q_BLOCK, KEY_BLOCK), 0)
    lane_i = lax.broadcasted_iota(jnp.int32, (Q_BLOCK, KEY_BLOCK), 1)

    def reset():
        m_scr[...] = jnp.full(m_scr.shape, NEG_INF, F32)
        l_scr[...] = jnp.zeros(l_scr.shape, F32)
        acc_scr[...] = jnp.zeros(acc_scr.shape, F32)

    def flash_step(kb, qa, kf_ref, vt_ref, masked):
        k0 = pl.multiple_of(kb * KEY_BLOCK, KEY_BLOCK)
        s_all = _dot(qa, kf_ref[:, pl.ds(k0, KEY_BLOCK)])
        if masked:
            dist = (t0 - k0) + (row_i - lane_i)
            keep = jnp.where(dist >= 0, dist, WINDOW) < WINDOW
        ps = []
        alphas = []
        for h in range(NSA_HEADS):
            rows = slice(h * Q_BLOCK, (h + 1) * Q_BLOCK)
            s = s_all[rows]
            if masked:
                s = jnp.where(keep, s, NEG_INF)
            m_prev = m_scr[rows]
            m_new = jnp.maximum(m_prev, jnp.max(s, axis=-1, keepdims=True))
            p = jnp.exp(s - jnp.concatenate([m_new] * (KEY_BLOCK // LANES), axis=1))
            if masked:
                p = jnp.where(keep, p, 0.0)
            alpha = jnp.exp(m_prev - m_new)
            l_scr[rows] = alpha * l_scr[rows] + jnp.sum(p, axis=-1, keepdims=True)
            m_scr[rows] = m_new
            ps.append(p.astype(BF16))
            alphas.append(alpha)
        pv = _dot_nt(jnp.concatenate(ps, axis=0), vt_ref[:, pl.ds(k0, KEY_BLOCK)])
        acc_scr[...] = jnp.concatenate(alphas, axis=0) * acc_scr[...] + pv

    def finish(branch):
        for h in range(NSA_HEADS):
            rows = slice(h * Q_BLOCK, (h + 1) * Q_BLOCK)
            cols = slice(h * KV_WIDTH, (h + 1) * KV_WIDTH)
            o_b = acc_scr[rows] / jnp.maximum(l_scr[rows], TINY)
            o_ref[:, cols] = o_ref[:, cols] + gate_col(branch, h) * o_b

    def loop(lo, hi, qa, kf_ref, vt_ref, masked):
        lax.fori_loop(lo, hi, lambda kb, c: (flash_step(kb, qa, kf_ref, vt_ref, masked), c)[1], 0)

    per_key_block = KEY_BLOCK // Q_BLOCK
    kb_diag = qi // per_key_block
    reset()
    qa = q_operand(True)
    loop(0, kb_diag, qa, kfs_scr, vst_ref, False)
    flash_step(kb_diag, qa, kfs_scr, vst_ref, True)
    finish(1)
    reset()
    qa = q_operand(False)
    kb_lo = jnp.maximum(qi - WINDOW // Q_BLOCK, 0) // per_key_block
    loop(kb_lo, kb_diag + 1, qa, kfw_scr, vwt_ref, True)
    finish(2)


def prompt_attention(qp, kct, vc, kvt_b, r, batch, seq):
    assert seq % KEY_BLOCK == 0 and seq // SEL_BLOCK <= LANES - FEAT_SEL0
    nqb = seq // Q_BLOCK
    n_c = kct.shape[2]
    n_sel = seq // SEL_BLOCK
    topn = min(SEL_TOPN, n_sel)
    cs = np.arange(n_c)[:, None] * CMP_STRIDE
    ss = np.arange(n_sel)[None, :] * SEL_BLOCK
    ov = np.clip(np.minimum(cs + CMP_BLOCK, ss + SEL_BLOCK) - np.maximum(cs, ss), 0, None) / CMP_BLOCK
    ov[(seq - CMP_BLOCK) // CMP_STRIDE + 1:] = 0.0
    kv_spec = lambda idx: pl.BlockSpec((None, None, KV_WIDTH, seq), lambda b, i: (idx, b, 0, 0))
    return pl.pallas_call(
        functools.partial(_pattn_kernel, n_sel=n_sel, topn=topn),
        grid=(batch, nqb),
        in_specs=[
            pl.BlockSpec((Q_BLOCK, PADDED_Q), lambda b, i: (b * nqb + i, 0)),
            pl.BlockSpec((None, KV_WIDTH, n_c), lambda b, i: (b, 0, 0)),
            pl.BlockSpec((None, n_c, KV_WIDTH), lambda b, i: (b, 0, 0)),
            kv_spec(2), kv_spec(3), kv_spec(4), kv_spec(5),
            pl.BlockSpec((LANES, seq), lambda b, i: (0, 0)),
            pl.BlockSpec((Q_BLOCK, LANES), lambda b, i: (b * nqb + i, R_SMALL // LANES)),
            pl.BlockSpec((n_sel, n_c), lambda b, i: (0, 0)),
        ],
        out_specs=pl.BlockSpec((Q_BLOCK, PADDED_Q), lambda b, i: (b * nqb + i, 0)),
        out_shape=jax.ShapeDtypeStruct((batch * seq, PADDED_Q), F32),
        scratch_shapes=[
            pltpu.VMEM((2 * KV_WIDTH, seq), BF16),
            pltpu.VMEM((2 * KV_WIDTH, seq), BF16),
            pltpu.VMEM((NSA_HEADS * Q_BLOCK, LANES), F32),
            pltpu.VMEM((NSA_HEADS * Q_BLOCK, LANES), F32),
            pltpu.VMEM((NSA_HEADS * Q_BLOCK, KV_WIDTH), F32),
        ],
        compiler_params=_cparams("parallel", "arbitrary"),
        name="prompt_nsa",
    )(qp, kct, vc, kvt_b, kvt_b, kvt_b, kvt_b, jnp.asarray(_feature_rows(seq), BF16), r,
      jnp.asarray(ov.T, BF16))


def _gla_log_decay(sm, aup_ref, ab_ref):
    z = _dot(sm.astype(BF16), aup_ref[...]) + ab_ref[...]
    return _log_sigmoid(z) / GLA_GATE_TAU


def _gla_head_norm(o, gg_ref):
    on = o * lax.rsqrt(jnp.mean(o * o, axis=-1, keepdims=True) + NORM_EPS)
    return on * gg_ref[...]


def _pgla_kernel(qg_ref, kg_ref, vg_ref, sm_ref, aup_ref, ab_ref, gg_ref, og_ref, sfin_ref, st_scr, *, n_tiles):
    ti = pl.program_id(1)
    tt = qg_ref.shape[0]
    n_chunks = tt // GLA_CHUNK

    @pl.when(ti == 0)
    def _():
        st_scr[...] = jnp.zeros(st_scr.shape, F32)

    la = _gla_log_decay(sm_ref[...], aup_ref, ab_ref)
    rin = lax.broadcasted_iota(jnp.int32, la.shape, 0) % GLA_CHUNK
    b = la
    sh = 1
    while sh < GLA_CHUNK:
        b = b + jnp.where(rin >= sh, pltpu.roll(b, sh, 0), 0.0)
        sh *= 2
    b3 = b.reshape(n_chunks, GLA_CHUNK, b.shape[-1])
    b_last3 = b3[:, GLA_CHUNK - 1:GLA_CHUNK, :]
    b_last = jnp.broadcast_to(b_last3, b3.shape).reshape(b.shape)
    e_b = jnp.exp(b)
    e_nb = jnp.exp(-b)
    e_tail = jnp.exp(b_last - b)
    e_last = jnp.exp(b_last3)

    r_i = lax.broadcasted_iota(jnp.int32, (tt, tt), 0)
    c_i = lax.broadcasted_iota(jnp.int32, (tt, tt), 1)
    causal = (r_i // GLA_CHUNK == c_i // GLA_CHUNK) & (c_i <= r_i)

    for h in range(GLA_HEADS):
        ks = slice(h * GLA_DK, (h + 1) * GLA_DK)
        vs = slice(h * GLA_DV, (h + 1) * GLA_DV)
        q = qg_ref[:, ks] * (GLA_DK ** -0.5)
        k = kg_ref[:, ks]
        v = vg_ref[:, vs].astype(BF16)
        q_dec = (q * e_b[:, ks]).astype(BF16)
        k_inv = (k * e_nb[:, ks]).astype(BF16)
        k_tail = (k * e_tail[:, ks]).astype(BF16)
        a = jnp.where(causal, _dot_nt(q_dec, k_inv), 0.0)
        o = _dot(a.astype(BF16), v)
        st = st_scr[h]
        o_inter = []
        for c in range(n_chunks):
            rows = slice(c * GLA_CHUNK, (c + 1) * GLA_CHUNK)
            o_inter.append(_dot_nt(q_dec[rows], st.astype(BF16)))
            kv_t = lax.dot_general(v[rows], k_tail[rows], (((0,), (0,)), ((), ())),
                                   preferred_element_type=F32)
            st = st * e_last[c, :, ks] + kv_t
        st_scr[h] = st
        o = o + jnp.concatenate(o_inter, axis=0)
        og_ref[:, vs] = _gla_head_norm(o, gg_ref)

    @pl.when(ti == n_tiles - 1)
    def _():
        for h in range(GLA_HEADS):
            sfin_ref[h] = st_scr[h].T


def prompt_gla(r, aup, ab, gg, batch, seq):
    tt = _tile(seq, 256, GLA_CHUNK)
    nt = seq // tt
    row = lambda b, i: b * nt + i
    return pl.pallas_call(
        functools.partial(_pgla_kernel, n_tiles=nt),
        grid=(batch, nt),
        in_specs=[
            pl.BlockSpec((tt, GLA_KEY_WIDTH), lambda b, i: (row(b, i), R_QG // GLA_KEY_WIDTH)),
            pl.BlockSpec((tt, GLA_KEY_WIDTH), lambda b, i: (row(b, i), R_KG // GLA_KEY_WIDTH)),
            pl.BlockSpec((tt, GLA_VAL_WIDTH), lambda b, i: (row(b, i), R_VG // GLA_VAL_WIDTH)),
            pl.BlockSpec((tt, LANES), lambda b, i: (row(b, i), R_SMALL // LANES)),
            pl.BlockSpec(aup.shape, lambda b, i: (0, 0)),
            pl.BlockSpec(ab.shape, lambda b, i: (0, 0)),
            pl.BlockSpec(gg.shape, lambda b, i: (0, 0)),
        ],
        out_specs=[
            pl.BlockSpec((tt, GLA_VAL_WIDTH), lambda b, i: (row(b, i), 0)),
            pl.BlockSpec((None, GLA_HEADS, GLA_DK, GLA_DV), lambda b, i: (b, 0, 0, 0)),
        ],
        out_shape=[
            jax.ShapeDtypeStruct((batch * seq, GLA_VAL_WIDTH), F32),
            jax.ShapeDtypeStruct((batch, GLA_HEADS, GLA_DK, GLA_DV), F32),
        ],
        scratch_shapes=[pltpu.VMEM((GLA_HEADS, GLA_DV, GLA_DK), F32)],
        compiler_params=_cparams("parallel", "arbitrary"),
        name="prompt_gla",
    )(r, r, r, r, aup, ab, gg)


def _sgla_kernel(qg_ref, kg_ref, vg_ref, sm_ref, aup_ref, ab_ref, gg_ref, s_ref, og_ref, snew_ref, *, ds):
    rows = qg_ref.shape[0]
    nb = rows // ds
    la = _gla_log_decay(sm_ref[...], aup_ref, ab_ref)
    ri = lax.broadcasted_iota(jnp.int32, la.shape, 0) % ds
    b = la
    sh = 1
    while sh < ds:
        b = b + jnp.where(ri >= sh, pltpu.roll(b, sh, 0), 0.0)
        sh *= 2
    b_last = b
    for d in range(1, ds):
        b_last = jnp.where(ri == ds - 1 - d, pltpu.roll(b, rows - d, 0), b_last)
    e_b = jnp.exp(b)
    e_nb = jnp.exp(-b)
    e_tail = jnp.exp(b_last - b)
    e_last = jnp.exp(b_last)

    r_i = lax.broadcasted_iota(jnp.int32, (rows, rows), 0)
    c_i = lax.broadcasted_iota(jnp.int32, (rows, rows), 1)
    causal = (r_i // ds == c_i // ds) & (c_i <= r_i)
    row_b = lax.broadcasted_iota(jnp.int32, (rows, GLA_DV), 0) // ds

    for h in range(GLA_HEADS):
        ks = slice(h * GLA_DK, (h + 1) * GLA_DK)
        vs = slice(h * GLA_DV, (h + 1) * GLA_DV)
        q = qg_ref[:, ks] * (GLA_DK ** -0.5)
        k = kg_ref[:, ks]
        v = vg_ref[:, vs]
        q_dec = (q * e_b[:, ks]).astype(BF16)
        k_inv = (k * e_nb[:, ks]).astype(BF16)
        k_tail_t = (k * e_tail[:, ks]).T
        e_last_t = e_last[:, ks].T
        a = jnp.where(causal, _dot_nt(q_dec, k_inv), 0.0)
        o = _dot(a.astype(BF16), v.astype(BF16))
        for bb in range(nb):
            s_prev = s_ref[bb, h]
            o_inter = _dot(q_dec, s_prev.astype(BF16))
            o = o + jnp.where(row_b == bb, o_inter, 0.0)
            s_new = e_last_t[:, bb * ds:bb * ds + 1] * s_prev
            for j in range(ds):
                rr = bb * ds + j
                s_new = s_new + k_tail_t[:, rr:rr + 1] * v[rr:rr + 1, :]
            snew_ref[bb, h] = s_new
        og_ref[:, vs] = _gla_head_norm(o, gg_ref)


def sample_gla(r, aup, ab, gg, state, layer, dec_batch, ds):
    nb = _tile(dec_batch, 8, 1)
    rows = nb * ds
    return pl.pallas_call(
        functools.partial(_sgla_kernel, ds=ds),
        grid=(dec_batch // nb,),
        in_specs=[
            pl.BlockSpec((rows, GLA_KEY_WIDTH), lambda i: (i, R_QG // GLA_KEY_WIDTH)),
            pl.BlockSpec((rows, GLA_KEY_WIDTH), lambda i: (i, R_KG // GLA_KEY_WIDTH)),
            pl.BlockSpec((rows, GLA_VAL_WIDTH), lambda i: (i, R_VG // GLA_VAL_WIDTH)),
            pl.BlockSpec((rows, LANES), lambda i: (i, R_SMALL // LANES)),
            pl.BlockSpec(aup.shape, lambda i: (0, 0)),
            pl.BlockSpec(ab.shape, lambda i: (0, 0)),
            pl.BlockSpec(gg.shape, lambda i: (0, 0)),
            pl.BlockSpec((None, nb, GLA_HEADS, GLA_DK, GLA_DV), lambda i: (layer, i, 0, 0, 0)),
        ],
        out_specs=[
            pl.BlockSpec((rows, GLA_VAL_WIDTH), lambda i: (i, 0)),
            pl.BlockSpec((nb, GLA_HEADS, GLA_DK, GLA_DV), lambda i: (i, 0, 0, 0)),
        ],
        out_shape=[
            jax.ShapeDtypeStruct((dec_batch * ds, GLA_VAL_WIDTH), F32),
            jax.ShapeDtypeStruct((dec_batch, GLA_HEADS, GLA_DK, GLA_DV), F32),
        ],
        compiler_params=_cparams("parallel"),
        name="sample_gla",
    )(r, r, r, r, aup, ab, gg, state)


def _mix_kernel(x_ref, on_ref, za_ref, og_ref, zb_ref, mg_ref, wba_ref, wbb_ref, wo_ref, fg_ref, *o_refs):
    pa = _dot((on_ref[...] * _silu(za_ref[...])).astype(BF16), wba_ref[...])
    pb = _dot((og_ref[...] * _silu(zb_ref[...])).astype(BF16), wbb_ref[...])
    mix = _sigmoid(mg_ref[:, :D_MODEL]) * pa + _sigmoid(mg_ref[:, D_MODEL:]) * pb
    y = x_ref[...] + _dot(mix.astype(BF16), wo_ref[...])
    o_refs[0][...] = y
    if len(o_refs) > 1:
        yn = y * lax.rsqrt(jnp.mean(y * y, axis=-1, keepdims=True) + NORM_EPS)
        o_refs[1][...] = yn * fg_ref[...]


def mixer_output(x, o_nsa, r, o_gla, wba, wbb, wo, final_g, with_final_norm):
    n = x.shape[0]
    tm = _tile(n, 256)
    row = lambda w: pl.BlockSpec((tm, w), lambda i: (i, 0))
    wspec = lambda w: pl.BlockSpec(w.shape, lambda i: (0, 0))
    n_out = 2 if with_final_norm else 1
    return pl.pallas_call(
        _mix_kernel,
        grid=(n // tm,),
        in_specs=[
            row(D_MODEL), row(PADDED_Q),
            pl.BlockSpec((tm, PADDED_Q), lambda i: (i, R_ZA // PADDED_Q)),
            row(GLA_VAL_WIDTH),
            pl.BlockSpec((tm, GLA_VAL_WIDTH), lambda i: (i, R_ZB // GLA_VAL_WIDTH)),
            pl.BlockSpec((tm, 2 * D_MODEL), lambda i: (i, R_MG // (2 * D_MODEL))),
            wspec(wba), wspec(wbb), wspec(wo), wspec(final_g),
        ],
        out_specs=[row(D_MODEL)] * n_out,
        out_shape=[jax.ShapeDtypeStruct((n, D_MODEL), F32)] * n_out,
        compiler_params=_cparams("parallel"),
        name="mixer_output",
    )(x, o_nsa, r, o_gla, r, r, wba, wbb, wo, final_g)


def _select_blocks(score, lane_idx, n_cand, topn):
    rank = jnp.zeros(score.shape, jnp.int32)
    for j in range(n_cand):
        col = score[:, j:j + 1]
        rank = rank + jnp.where(col > score, 1, 0) + jnp.where(col == score, jnp.where(lane_idx > j, 1, 0), 0)
    return jnp.where(rank < topn, 1.0, 0.0)


def _softmax_two(parts, keeps):
    m = functools.reduce(jnp.maximum, [jnp.max(s, axis=-1, keepdims=True) for s in parts])
    ps = [jnp.where(kp, jnp.exp(s - m), 0.0) for s, kp in zip(parts, keeps)]
    total = functools.reduce(lambda a, c: a + c, [jnp.sum(p, axis=-1, keepdims=True) for p in ps])
    return ps, jnp.maximum(total, TINY)


def _smain_kernel(pt_ref, q_ref, gate_ref, slope_ref, *rest, n_pages, past, ds, n_sel, topn):
    del pt_ref
    rest = list(rest)
    take = lambda n: [rest.pop(0) for _ in range(n)]
    ak_refs, bk_refs, av_refs, bv_refs = take(n_pages), take(n_pages), take(n_pages), take(n_pages)
    kt_refs, vt_refs = take(n_pages), take(n_pages)
    kw_ref, vw_ref, new_ref, w2k_ref, w2v_ref, ov_ref, exp_ref, exn_ref = take(8)
    o_ref, kw_out_ref, vw_out_ref = take(3)
    rows = q_ref.shape[0]
    grp_rows = NSA_KV_HEADS * ds
    n_new = new_ref.shape[2]
    b_loc = pl.program_id(0) % (n_new // ds)
    cat0 = lambda refs: jnp.concatenate([r[...] for r in refs], axis=0)

    qs = q_ref[...]
    slope = slope_ref[:, 0:1]
    kc = _summaries(cat0(ak_refs), _shift_up(cat0(bk_refs)), w2k_ref[...]).astype(BF16)
    vc = _summaries(cat0(av_refs), _shift_up(cat0(bv_refs)), w2v_ref[...]).astype(BF16)
    n_c = kc.shape[0]

    def irow(shape):
        return lax.broadcasted_iota(jnp.int32, shape, 0) % ds

    dist_c = (past + irow((rows, n_c))) - (lax.broadcasted_iota(jnp.int32, (rows, n_c), 1) * CMP_STRIDE
                                          + (CMP_BLOCK - 1))
    vis_c = dist_c >= 0
    s_c = jnp.where(vis_c, _dot_nt(qs, kc) - slope * dist_c.astype(F32), NEG_INF)
    m_c = jnp.max(s_c, axis=-1, keepdims=True)
    p_c = jnp.where(vis_c, jnp.exp(s_c - m_c), 0.0)
    p_c = p_c / jnp.maximum(jnp.sum(p_c, axis=-1, keepdims=True), TINY)
    o_cmp = _dot(p_c.astype(BF16), vc)

    hi = p_c.astype(BF16)
    lo = (p_c - hi.astype(F32)).astype(BF16)
    imp_h = _dot(hi, ov_ref[...]) + _dot(lo, ov_ref[...])
    imp = functools.reduce(lambda a, c: a + c,
                           [imp_h[r * grp_rows:(r + 1) * grp_rows] for r in range(NSA_GROUP)])
    s_idx = lax.broadcasted_iota(jnp.int32, imp.shape, 1)
    qpos = past + irow(imp.shape)
    valid = (s_idx < n_sel) & (s_idx * SEL_BLOCK <= qpos)
    forced = (s_idx == 0) | (s_idx == qpos // SEL_BLOCK)
    score = jnp.where(valid, imp + jnp.where(forced, FORCE_BONUS, 0.0), NEG_INF)
    sel = _select_blocks(score, s_idx, n_sel, topn)
    sel = jnp.concatenate([sel] * NSA_GROUP, axis=0).astype(BF16)
    keep_past = _dot(sel, exp_ref[...]) > 0.5
    sel_new = _dot(sel, exn_ref[...]) > 0.5

    slot = lax.broadcasted_iota(jnp.int32, (rows, n_new), 1)
    dist_n = irow((rows, n_new)) - slot % ds
    mine = (slot // ds == b_loc) & (dist_n >= 0)
    bias_n = slope * dist_n.astype(F32)

    s_p = jnp.concatenate([_dot(qs, r[...].astype(BF16)) for r in kt_refs], axis=1)
    dist_p = (past + irow(s_p.shape)) - lax.broadcasted_iota(jnp.int32, s_p.shape, 1)
    s_p = jnp.where(keep_past, s_p - slope * dist_p.astype(F32), NEG_INF)
    keep_n = mine & sel_new
    s_n = jnp.where(keep_n, _dot(qs, new_ref[2].astype(BF16)) - bias_n, NEG_INF)
    (p_p, p_n), total = _softmax_two([s_p, s_n], [keep_past, keep_n])
    acc = _dot_nt(p_n.astype(BF16), new_ref[3].astype(BF16))
    for j, r in enumerate(vt_refs):
        acc = acc + _dot_nt(p_p[:, j * LANES:(j + 1) * LANES].astype(BF16), r[...].astype(BF16))
    o_slc = acc / total

    kbuf = kw_ref[...]
    vbuf = vw_ref[...]
    w_buf = kbuf.shape[1]
    dist_w = (w_buf + irow((rows, w_buf))) - lax.broadcasted_iota(jnp.int32, (rows, w_buf), 1)
    keep_w = dist_w < WINDOW
    s_w = jnp.where(keep_w, _dot(qs, kbuf.astype(BF16)) - slope * dist_w.astype(F32), NEG_INF)
    s_wn = jnp.where(mine, _dot(qs, new_ref[4].astype(BF16)) - bias_n, NEG_INF)
    (p_w, p_wn), total_w = _softmax_two([s_w, s_wn], [keep_w, mine])
    o_win = (_dot_nt(p_w.astype(BF16), vbuf.astype(BF16)) + _dot_nt(p_wn.astype(BF16), new_ref[5].astype(BF16))) / total_w

    gates = _sigmoid(gate_ref[...])
    o_ref[...] = gates[:, 0:1] * o_cmp + gates[:, 1:2] * o_slc + gates[:, 2:3] * o_win

    lane_new = lax.broadcasted_iota(jnp.int32, (KV_WIDTH, n_new), 1)
    shift = (n_new - ds) - b_loc * ds
    for buf, slab, out_ref in ((kbuf, 4, kw_out_ref), (vbuf, 5, vw_out_ref)):
        rolled = pltpu.roll(buf, w_buf - ds, 1)
        own_last = pltpu.roll(new_ref[slab], shift, 1)
        out_ref[:, 0:w_buf - n_new] = rolled[:, 0:w_buf - n_new]
        out_ref[:, w_buf - n_new:w_buf] = jnp.where(lane_new >= n_new - ds, own_last, rolled[:, w_buf - n_new:w_buf])


def sample_attention(page_table, q_rows, gate_rows, ab_pages, slc_pools_t, win_bufs_t, new_t, w2k, w2v, layer, past, ds):
    assert past % SEL_BLOCK == 0 and past % LANES == 0
    dec_batch, n_pages = page_table.shape
    rows = q_rows.shape[1]
    w_buf = win_bufs_t[0].shape[3]
    n_new = min(LANES, dec_batch * ds)
    assert (dec_batch * ds) % n_new == 0 and n_new % ds == 0 and w_buf >= n_new
    n_c = n_pages * (LANES // CMP_STRIDE)
    n_cmp = (past + ds - CMP_BLOCK) // CMP_STRIDE + 1
    n_sel = -(-(past + ds) // SEL_BLOCK)
    topn = min(SEL_TOPN, n_sel)
    assert n_sel <= LANES and ds <= CMP_STRIDE
    cs = np.arange(n_c)[:, None] * CMP_STRIDE
    ss = np.arange(LANES)[None, :] * SEL_BLOCK
    ov = np.clip(np.minimum(cs + CMP_BLOCK, ss + SEL_BLOCK) - np.maximum(cs, ss), 0, None) / CMP_BLOCK
    ov[n_cmp:] = 0.0
    ov[:, n_sel:] = 0.0
    ex_past = (np.arange(past)[None, :] // SEL_BLOCK == np.arange(LANES)[:, None]).astype(np.float32)
    ex_new = ((past + np.arange(n_new) % ds)[None, :] // SEL_BLOCK == np.arange(LANES)[:, None]).astype(np.float32)
    head = np.arange(rows) // (NSA_KV_HEADS * ds) + NSA_GROUP * ((np.arange(rows) // ds) % NSA_KV_HEADS)
    slope = np.broadcast_to((2.0 ** -(head + 1.0))[:, None], (rows, LANES)).astype(np.float32)

    const = lambda shape: pl.BlockSpec(shape, lambda b, pt: (0,) * len(shape))
    page = lambda j, shape: pl.BlockSpec((None,) + shape, lambda b, pt: (pt[b, j], 0, 0))
    pool = lambda j: pl.BlockSpec((None, None, KV_WIDTH, LANES), lambda b, pt: (layer, pt[b, j], 0, 0))
    hid = ab_pages[0].shape[-1]
    in_specs = [pl.BlockSpec((None, rows, LANES), lambda b, pt: (b, 0, 0)),
                pl.BlockSpec((None, rows, LANES), lambda b, pt: (b, 0, 0)),
                const((rows, LANES))]
    operands = [q_rows, gate_rows, jnp.asarray(slope)]
    for arr in ab_pages:
        in_specs += [page(j, (LANES // CMP_STRIDE, hid)) for j in range(n_pages)]
        operands += [arr] * n_pages
    for arr in slc_pools_t:
        in_specs += [pool(j) for j in range(n_pages)]
        operands += [arr] * n_pages
    for arr in win_bufs_t:
        in_specs.append(pl.BlockSpec((None, None, KV_WIDTH, w_buf), lambda b, pt: (layer, b, 0, 0)))
        operands.append(arr)
    per_tile = n_new // ds
    in_specs.append(pl.BlockSpec((6, None, KV_WIDTH, n_new), lambda b, pt: (0, 0, 0, b // per_tile)))
    operands.append(new_t)
    consts = [w2k, w2v, jnp.asarray(ov, BF16), jnp.asarray(ex_past, BF16), jnp.asarray(ex_new, BF16)]
    in_specs += [const(c.shape) for c in consts]
    operands += consts
    buf_out = pl.BlockSpec((None, KV_WIDTH, w_buf), lambda b, pt: (b, 0, 0))
    return pl.pallas_call(
        functools.partial(_smain_kernel, n_pages=n_pages, past=past, ds=ds, n_sel=n_sel, topn=topn),
        grid_spec=pltpu.PrefetchScalarGridSpec(
            num_scalar_prefetch=1,
            grid=(dec_batch,),
            in_specs=in_specs,
            out_specs=[pl.BlockSpec((None, rows, LANES), lambda b, pt: (b, 0, 0)), buf_out, buf_out],
        ),
        out_shape=[jax.ShapeDtypeStruct((dec_batch, rows, LANES), F32),
                   jax.ShapeDtypeStruct((dec_batch, KV_WIDTH, w_buf), F32),
                   jax.ShapeDtypeStruct((dec_batch, KV_WIDTH, w_buf), F32)],
        compiler_params=_cparams("arbitrary"),
        name="sample_nsa",
    )(page_table, *operands)


def _pad_heads_cols(w):
    k = w.shape[0]
    w4 = w.reshape(k, NSA_KV_HEADS, NSA_GROUP, HEAD_DIM)
    eye = jnp.eye(NSA_KV_HEADS, dtype=w.dtype)
    return jnp.einsum("kgrd,gp->kgrpd", w4, eye).reshape(k, PADDED_Q)


def _layer_weights(w_in, pk_pe, pk_w1, pk_w2, pv_pe, pv_w1, pv_w2, a_up, a_b, gla_g, w_ba, w_bb, w_out):
    o = _OFF
    col = lambda i: w_in[:, o[i]:o[i + 1]]
    small = jnp.concatenate([col(2), col(7), jnp.zeros((D_MODEL, LANES - 3 * NSA_HEADS - GLA_GATE_RANK), F32)], axis=1)
    w_q = _pad_heads_cols(col(0)).astype(BF16)
    w_kv_t = col(1).T.astype(BF16)
    w_r = jnp.concatenate([_pad_heads_cols(col(3)), col(4), col(5), col(6), col(8), col(9), small], axis=1).astype(BF16)
    half = CMP_BLOCK // 2

    def cmp_weights(pe, w1, w2):
        eye = jnp.eye(NSA_KV_HEADS, dtype=F32)
        w1h = w1.reshape(2, half, HEAD_DIM, CMP_HIDDEN)
        big = lambda w: jnp.einsum("ldh,gk->lgdkh", w, eye).reshape(half * KV_WIDTH, NSA_KV_HEADS * CMP_HIDDEN)
        peh = pe.reshape(2, half, 1, HEAD_DIM)
        pe_row = lambda p: jnp.broadcast_to(p, (half, NSA_KV_HEADS, HEAD_DIM)).reshape(1, half * KV_WIDTH)
        w2bd = jnp.einsum("hd,gk->ghkd", w2, eye).reshape(NSA_KV_HEADS * CMP_HIDDEN, KV_WIDTH)
        return (pe_row(peh[0]), pe_row(peh[1]), big(w1h[0]).astype(BF16), big(w1h[1]).astype(BF16)), w2bd.astype(BF16)

    cw_k, w2k = cmp_weights(pk_pe, pk_w1, pk_w2)
    cw_v, w2v = cmp_weights(pv_pe, pv_w1, pv_w2)
    aup = jnp.zeros((LANES, GLA_KEY_WIDTH), F32).at[ALOW_LANE0:ALOW_LANE0 + GLA_GATE_RANK].set(a_up).astype(BF16)
    w_ba_p = _pad_heads_cols(w_ba.T).T.astype(BF16)
    return dict(w_q=w_q, w_kv_t=w_kv_t, w_r=w_r, cw_k=cw_k, cw_v=cw_v, w2k=w2k, w2v=w2v, aup=aup,
                ab=a_b.reshape(1, GLA_KEY_WIDTH), gg=gla_g.reshape(1, GLA_DV),
                w_ba=w_ba_p, w_bb=w_bb.astype(BF16), w_out=w_out.astype(BF16))


def _project(x, g_norm, lw, batch, seq):
    h = rms_norm_rows(x, g_norm, BF16)
    qp = matmul_rows(h, lw["w_q"], BF16, tn=512, scale=HEAD_DIM ** -0.5)
    kvt_f, kvt_b = kv_project_t(h, lw["w_kv_t"], batch, seq)
    r = matmul_rows(h, lw["w_r"], F32, tn=R_WIDTH // 7)
    return qp, kvt_f, kvt_b, r


def _tokens_minor(cache):
    lead = cache.shape[:-3]
    n = cache.ndim
    perm = tuple(range(n - 3)) + (n - 2, n - 1, n - 3)
    return cache.transpose(perm).reshape(lead + (KV_WIDTH, cache.shape[-3]))


def _tokens_major(x_t):
    lead = x_t.shape[:-2]
    n = len(lead)
    x5 = x_t.reshape(lead + (NSA_KV_HEADS, HEAD_DIM, x_t.shape[-1]))
    return x5.transpose(tuple(range(n)) + (n + 2, n, n + 1))


def kernel(x_prompt, x_sample, cache_k_cmp, cache_v_cmp, cache_k_slc, cache_v_slc, cache_k_win, cache_v_win, state_gla, page_table, norm_g, w_in, phi_k_pe, phi_k_w1, phi_k_w2, phi_v_pe, phi_v_w1, phi_v_w2, gla_alpha_up, gla_alpha_b, gla_norm_g, w_branch_a, w_branch_b, w_out, final_norm_g):
    batch, seq, _ = x_prompt.shape
    dec_batch, ds, _ = x_sample.shape
    depth = norm_g.shape[0]
    n_pool, page_size = cache_k_cmp.shape[1:3]
    assert page_size == LANES
    n_pages = page_table.shape[1]
    past = n_pages * page_size
    chunks_per_page = page_size // CMP_STRIDE
    final_g = final_norm_g.reshape(1, D_MODEL)

    cmp_pools_t = (_tokens_minor(cache_k_cmp), _tokens_minor(cache_v_cmp))
    slc_pools_t = (_tokens_minor(cache_k_slc), _tokens_minor(cache_v_slc))
    win_bufs_t = (_tokens_minor(cache_k_win), _tokens_minor(cache_v_win))

    y_p = x_prompt.reshape(batch * seq, D_MODEL)
    y_s = x_sample.reshape(dec_batch * ds, D_MODEL)
    outs_p, outs_s = [], []
    for l in range(depth):
        lw = _layer_weights(w_in[l], phi_k_pe[l], phi_k_w1[l], phi_k_w2[l], phi_v_pe[l], phi_v_w1[l], phi_v_w2[l],
                            gla_alpha_up[l], gla_alpha_b[l], gla_norm_g[l], w_branch_a[l], w_branch_b[l], w_out[l])
        last = l == depth - 1

        qp, kvt_f, kvt_b, r = _project(y_p, norm_g[l], lw, batch, seq)
        kct, vc = prompt_summaries(kvt_f, lw["cw_k"], lw["w2k"], lw["cw_v"], lw["w2v"])
        o_nsa = prompt_attention(qp, kct, vc, kvt_b, r, batch, seq)
        o_gla, s_gla = prompt_gla(r, lw["aup"], lw["ab"], lw["gg"], batch, seq)
        res = mixer_output(y_p, o_nsa, r, o_gla, lw["w_ba"], lw["w_bb"], lw["w_out"], final_g, last)
        y_p = res[0]
        if last:
            y_p_out = res[1]
        keep = min(WINDOW, seq)
        kv5 = _tokens_major(kvt_f)
        outs_p.append((kv5[0], kv5[1], kv5[2], kv5[3], kv5[4][:, seq - keep:], kv5[5][:, seq - keep:], s_gla))

        qs, kvs_t, _, rs = _project(y_s, norm_g[l], lw, 1, dec_batch * ds)
        ab_pages = []
        for pool_t, cw in zip(cmp_pools_t, (lw["cw_k"], lw["cw_v"])):
            a_half, b_half = pool_halves(pool_t, l, cw)
            ab_pages += [a_half.reshape(n_pool, chunks_per_page, -1), b_half.reshape(n_pool, chunks_per_page, -1)]
        q_rows = qs.reshape(dec_batch, ds, NSA_KV_HEADS, NSA_GROUP, KV_WIDTH).transpose(0, 3, 2, 1, 4)
        q_rows = q_rows.reshape(dec_batch, NSA_HEADS * ds, KV_WIDTH)
        gl = rs[:, R_SMALL + GATE_LANE0:R_SMALL + GATE_LANE0 + 3 * NSA_HEADS]
        gl = gl.reshape(dec_batch, ds, 3, NSA_KV_HEADS, NSA_GROUP).transpose(0, 4, 3, 1, 2)
        gate_rows = jnp.pad(gl.reshape(dec_batch, NSA_HEADS * ds, 3), ((0, 0), (0, 0), (0, LANES - 3)))
        o_rows, kw_new, vw_new = sample_attention(page_table, q_rows, gate_rows, ab_pages, slc_pools_t, win_bufs_t,
                                                  kvs_t, lw["w2k"], lw["w2v"], l, past, ds)
        o_nsa_s = o_rows.reshape(dec_batch, NSA_GROUP, NSA_KV_HEADS, ds, KV_WIDTH).transpose(0, 3, 2, 1, 4)
        o_nsa_s = o_nsa_s.reshape(dec_batch * ds, PADDED_Q)
        o_gla_s, s_gla_s = sample_gla(rs, lw["aup"], lw["ab"], lw["gg"], state_gla, l, dec_batch, ds)
        res = mixer_output(y_s, o_nsa_s, rs, o_gla_s, lw["w_ba"], lw["w_bb"], lw["w_out"], final_g, last)
        y_s = res[0]
        if last:
            y_s_out = res[1]
        kvs5 = _tokens_major(kvs_t[:, 0]).reshape(6, dec_batch, ds, NSA_KV_HEADS, HEAD_DIM)
        outs_s.append((kvs5[0], kvs5[1], kvs5[2], kvs5[3], _tokens_major(kw_new), _tokens_major(vw_new), s_gla_s))

    stack = lambda outs: [jnp.stack(t) for t in zip(*outs)]
    return (y_p_out.reshape(batch, seq, D_MODEL), y_s_out.reshape(dec_batch, ds, D_MODEL),
            *stack(outs_p), *stack(outs_s))
```

```python
import functools

import numpy as np
import jax
import jax.numpy as jnp
from jax import lax
from jax.experimental import pallas as pl
from jax.experimental.pallas import tpu as pltpu

F32 = jnp.float32
BF16 = jnp.bfloat16

D_MODEL = 1024
NSA_HEADS = 8
NSA_KV_HEADS = 2
NSA_GROUP = NSA_HEADS // NSA_KV_HEADS
HEAD_DIM = 64
NSA_WIDTH = NSA_HEADS * HEAD_DIM
KV_WIDTH = NSA_KV_HEADS * HEAD_DIM
CMP_BLOCK = 32
CMP_STRIDE = 16
CMP_HIDDEN = 2 * HEAD_DIM
SEL_BLOCK = 64
SEL_TOPN = 16
WINDOW = 512
GLA_HEADS = 4
GLA_KEY_WIDTH = D_MODEL // 2
GLA_VAL_WIDTH = D_MODEL
GLA_DK = GLA_KEY_WIDTH // GLA_HEADS
GLA_DV = GLA_VAL_WIDTH // GLA_HEADS
GLA_GATE_RANK = 16
GLA_GATE_TAU = 16.0
GLA_CHUNK = 32
Q_BLOCK = 128
KEY_BLOCK = 256
NORM_EPS = 1e-6
NEG_INF = -1e30
TINY = 1e-30
FORCE_BONUS = 1e4
MASK_BIG = float(2.0 ** 100)
PADDED_Q = NSA_HEADS * KV_WIDTH
LANES = 128

_IN_SIZES = (NSA_WIDTH, 6 * KV_WIDTH, 3 * NSA_HEADS, NSA_WIDTH, GLA_KEY_WIDTH, GLA_KEY_WIDTH,
             GLA_VAL_WIDTH, GLA_GATE_RANK, GLA_VAL_WIDTH, 2 * D_MODEL)
_OFF = tuple(int(o) for o in np.cumsum((0,) + _IN_SIZES))

R_ZA, R_QG, R_KG, R_VG, R_ZB, R_MG, R_SMALL = 0, 1024, 1536, 2048, 3072, 4096, 6144
R_WIDTH = R_SMALL + LANES
GATE_LANE0 = 0
ALOW_LANE0 = 3 * NSA_HEADS
FEAT_POS_HI, FEAT_POS_LO, FEAT_ONE, FEAT_SEL0 = 0, 1, 2, 64

VMEM_LIMIT = 48 * 1024 * 1024


def _cparams(*sem):
    return pltpu.CompilerParams(dimension_semantics=sem, vmem_limit_bytes=VMEM_LIMIT)


def _tile(n, target, mult=8):
    if n <= target:
        return n
    t = (target // mult) * mult
    while t >= mult:
        if n % t == 0:
            return t
        t -= mult
    return n


def _sigmoid(x):
    return 1.0 / (1.0 + jnp.exp(-x))


def _silu(x):
    return x * _sigmoid(x)


def _log_sigmoid(x):
    return -(jnp.maximum(-x, 0.0) + jnp.log1p(jnp.exp(-jnp.abs(x))))


def _dot_nt(a, b):
    return lax.dot_general(a, b, (((1,), (1,)), ((), ())), preferred_element_type=F32)


def _dot(a, b):
    return jnp.dot(a, b, preferred_element_type=F32)


def _slope(h):
    return float(2.0 ** (-(h + 1)))


def _norm_kernel(x_ref, g_ref, o_ref):
    xf = x_ref[...]
    xn = xf * lax.rsqrt(jnp.mean(xf * xf, axis=-1, keepdims=True) + NORM_EPS)
    o_ref[...] = (xn * g_ref[...]).astype(o_ref.dtype)


def rms_norm_rows(x, g, out_dtype):
    n, d = x.shape
    tm = _tile(n, 512)
    return pl.pallas_call(
        _norm_kernel,
        grid=(n // tm,),
        in_specs=[pl.BlockSpec((tm, d), lambda i: (i, 0)), pl.BlockSpec((1, d), lambda i: (0, 0))],
        out_specs=pl.BlockSpec((tm, d), lambda i: (i, 0)),
        out_shape=jax.ShapeDtypeStruct((n, d), out_dtype),
        compiler_params=_cparams("parallel"),
        name="rms_norm",
    )(x, g.reshape(1, d))


def _mm_kernel(x_ref, w_ref, o_ref, *, scale):
    acc = _dot(x_ref[...], w_ref[...])
    if scale != 1.0:
        acc = acc * scale
    o_ref[...] = acc.astype(o_ref.dtype)


def matmul_rows(x, w, out_dtype, *, tn, scale=1.0):
    n, k = x.shape
    m = w.shape[1]
    tm = _tile(n, 1024)
    return pl.pallas_call(
        functools.partial(_mm_kernel, scale=scale),
        grid=(n // tm, m // tn),
        in_specs=[pl.BlockSpec((tm, k), lambda i, j: (i, 0)), pl.BlockSpec((k, tn), lambda i, j: (0, j))],
        out_specs=pl.BlockSpec((tm, tn), lambda i, j: (i, j)),
        out_shape=jax.ShapeDtypeStruct((n, m), out_dtype),
        compiler_params=_cparams("parallel", "parallel"),
        name="proj",
    )(x, w)


def _mm_t_kernel(wt_ref, x_ref, of_ref, ob_ref):
    acc = _dot_nt(wt_ref[...], x_ref[...])
    acc = acc.reshape(of_ref.shape)
    of_ref[...] = acc
    ob_ref[...] = acc.astype(ob_ref.dtype)


def kv_project_t(x, wt, batch, seq):
    k = x.shape[1]
    tm = _tile(seq, 512, LANES)
    nt = seq // tm
    n_slab = wt.shape[0] // KV_WIDTH
    out = pl.BlockSpec((n_slab, None, KV_WIDTH, tm), lambda b, i: (0, b, 0, i))
    return pl.pallas_call(
        _mm_t_kernel,
        grid=(batch, nt),
        in_specs=[pl.BlockSpec(wt.shape, lambda b, i: (0, 0)), pl.BlockSpec((tm, k), lambda b, i: (b * nt + i, 0))],
        out_specs=[out, out],
        out_shape=[jax.ShapeDtypeStruct((n_slab, batch, KV_WIDTH, seq), F32),
                   jax.ShapeDtypeStruct((n_slab, batch, KV_WIDTH, seq), BF16)],
        compiler_params=_cparams("parallel", "parallel"),
        name="proj_kv_t",
    )(wt, x)


def _chunk_rows(xs_ref, n_chunks):
    return jnp.concatenate([xs_ref[pl.ds(l, n_chunks, stride=CMP_STRIDE), :] for l in range(CMP_STRIDE)], axis=1)


def _halves(chunks, pea_ref, peb_ref, wa_ref, wb_ref):
    a = _dot((chunks + pea_ref[...]).astype(BF16), wa_ref[...])
    b = _dot((chunks + peb_ref[...]).astype(BF16), wb_ref[...])
    return a, b


def _summaries(a, b_next, w2bd):
    return _dot(_silu(a + b_next).astype(BF16), w2bd)


def _shift_up(x):
    n = x.shape[0]
    return pltpu.roll(x, n - 1, 0)


def _transpose_pages(src, xs_ref, n_pages):
    for p in range(n_pages):
        xs_ref[p * LANES:(p + 1) * LANES, :] = src(p).T


def _pool_ab_kernel(xk_ref, xv_ref, kpea, kpeb, kwa, kwb, vpea, vpeb, vwa, vwb, o_ref, xs_scr):
    n_pages = xk_ref.shape[0]
    hid = kwa.shape[1]
    for i, (x_ref, cw) in enumerate(((xk_ref, (kpea, kpeb, kwa, kwb)), (xv_ref, (vpea, vpeb, vwa, vwb)))):
        _transpose_pages(lambda p: x_ref[p], xs_scr, n_pages)
        a, b = _halves(_chunk_rows(xs_scr, n_pages * (LANES // CMP_STRIDE)), *cw)
        o_ref[:, (2 * i) * hid:(2 * i + 1) * hid] = a
        o_ref[:, (2 * i + 1) * hid:(2 * i + 2) * hid] = b


def pool_halves(pool_k_t, pool_v_t, layer, cw_k, cw_v):
    n_pool = pool_k_t.shape[1]
    pg = _tile(n_pool, 16, 1)
    rows = pg * (LANES // CMP_STRIDE)
    hid = cw_k[2].shape[1]
    full = lambda a: pl.BlockSpec(a.shape, lambda i: (0,) * a.ndim)
    page_spec = pl.BlockSpec((None, pg, KV_WIDTH, LANES), lambda i: (layer, i, 0, 0))
    return pl.pallas_call(
        _pool_ab_kernel,
        grid=(n_pool // pg,),
        in_specs=[page_spec, page_spec] + [full(c) for c in (*cw_k, *cw_v)],
        out_specs=pl.BlockSpec((rows, 4 * hid), lambda i: (i, 0)),
        out_shape=jax.ShapeDtypeStruct((n_pool * (LANES // CMP_STRIDE), 4 * hid), F32),
        scratch_shapes=[pltpu.VMEM((pg * LANES, KV_WIDTH), F32)],
        compiler_params=_cparams("parallel"),
        name="pool_halves",
    )(pool_k_t, pool_v_t, *cw_k, *cw_v)


def _pcmp_kernel(kt_ref, vt_ref, kpea, kpeb, kwa, kwb, kw2, vpea, vpeb, vwa, vwb, vw2, kct_ref, vc_ref, xs_scr):
    seq = kt_ref.shape[1]
    n_blocks = seq // LANES
    n_chunks = seq // CMP_STRIDE
    for src_ref, (pea, peb, wa, wb, w2), is_key in ((kt_ref, (kpea, kpeb, kwa, kwb, kw2), True),
                                                   (vt_ref, (vpea, vpeb, vwa, vwb, vw2), False)):
        _transpose_pages(lambda p: src_ref[:, p * LANES:(p + 1) * LANES], xs_scr, n_blocks)
        a, b = _halves(_chunk_rows(xs_scr, n_chunks), pea, peb, wa, wb)
        rows = _summaries(a, _shift_up(b), w2[...])
        if is_key:
            kct_ref[...] = rows.T.astype(kct_ref.dtype)
        else:
            vc_ref[...] = rows.astype(vc_ref.dtype)


def prompt_summaries(kvt_f, cw_k, w2k, cw_v, w2v):
    _, batch, _, seq = kvt_f.shape
    n_c = seq // CMP_STRIDE
    consts = [*cw_k, w2k, *cw_v, w2v]
    full = lambda a: pl.BlockSpec(a.shape, lambda b: (0,) * a.ndim)
    return pl.pallas_call(
        _pcmp_kernel,
        grid=(batch,),
        in_specs=[pl.BlockSpec((None, None, KV_WIDTH, seq), lambda b: (0, b, 0, 0)),
                  pl.BlockSpec((None, None, KV_WIDTH, seq), lambda b: (1, b, 0, 0))] + [full(c) for c in consts],
        out_specs=[pl.BlockSpec((None, KV_WIDTH, n_c), lambda b: (b, 0, 0)),
                   pl.BlockSpec((None, n_c, KV_WIDTH), lambda b: (b, 0, 0))],
        out_shape=[jax.ShapeDtypeStruct((batch, KV_WIDTH, n_c), BF16),
                   jax.ShapeDtypeStruct((batch, n_c, KV_WIDTH), BF16)],
        scratch_shapes=[pltpu.VMEM((seq, KV_WIDTH), F32)],
        compiler_params=_cparams("parallel"),
        name="prompt_summaries",
    )(kvt_f, kvt_f, *consts)


def _feature_rows(seq):
    kpos = np.arange(seq)
    f = np.zeros((LANES, seq), np.float32)
    f[FEAT_POS_HI] = kpos // SEL_BLOCK
    f[FEAT_POS_LO] = kpos % SEL_BLOCK
    f[FEAT_ONE] = 1.0
    n_sel = seq // SEL_BLOCK
    f[FEAT_SEL0:FEAT_SEL0 + n_sel] = kpos[None, :] // SEL_BLOCK == np.arange(n_sel)[:, None]
    return f


def _rank_select_t(score_t, n_cand, topn):
    idx = lax.broadcasted_iota(jnp.int32, score_t.shape, 0)
    rank = jnp.zeros(score_t.shape, jnp.int32)
    for j in range(n_cand):
        row = score_t[j:j + 1, :]
        rank = rank + jnp.where(idx > j, jnp.where(row >= score_t, 1, 0), jnp.where(row > score_t, 1, 0))
    return jnp.where(rank < topn, 1.0, 0.0)


def _pattn_kernel(q_ref, kct_ref, vc_ref, kst_ref, vst_ref, kwt_ref, vwt_ref, feat_ref, sm_ref, ovt_ref,
                  o_ref, kfs_scr, kfw_scr, vs_scr, vw_scr, m_scr, acc_scr, *, n_sel, topn):
    qi = pl.program_id(1)
    t0 = qi * Q_BLOCK
    half = KV_WIDTH // NSA_KV_HEADS

    @pl.when(qi == 0)
    def _():
        kfs_scr[0:KV_WIDTH, :] = kst_ref[...]
        kfs_scr[KV_WIDTH:, :] = feat_ref[...]
        kfw_scr[0:KV_WIDTH, :] = kwt_ref[...]
        kfw_scr[KV_WIDTH:, :] = feat_ref[...]
        ones = jnp.ones((half, kst_ref.shape[1]), BF16)
        for src, dst in ((vst_ref, vs_scr), (vwt_ref, vw_scr)):
            dst[0, 0:half, :] = src[0:half, :]
            dst[0, half:, :] = ones
            dst[1, 0:half, :] = ones
            dst[1, half:, :] = src[half:, :]

    q = q_ref[...]
    qs = jnp.concatenate([q[:, h * KV_WIDTH:(h + 1) * KV_WIDTH] for h in range(NSA_HEADS)], axis=0)
    gates = _sigmoid(sm_ref[...])

    def gate_col(branch, h):
        j = GATE_LANE0 + branch * NSA_HEADS + h
        return gates[:, j:j + 1]

    n_c = kct_ref.shape[1]
    s_all = _dot(qs, kct_ref[...])
    tq = lax.broadcasted_iota(jnp.int32, (Q_BLOCK, n_c), 0) + t0
    cend = lax.broadcasted_iota(jnp.int32, (Q_BLOCK, n_c), 1) * CMP_STRIDE + (CMP_BLOCK - 1)
    dist_c = tq - cend
    vis_c = dist_c >= 0
    dist_cf = dist_c.astype(F32)
    vc = vc_ref[...]
    psum = [None] * NSA_KV_HEADS
    for h in range(NSA_HEADS):
        g = h // NSA_GROUP
        s = s_all[h * Q_BLOCK:(h + 1) * Q_BLOCK] - _slope(h) * dist_cf
        s = jnp.where(vis_c, s, NEG_INF)
        m = jnp.max(s, axis=-1, keepdims=True)
        p = jnp.where(vis_c, jnp.exp(s - m), 0.0)
        p = p / jnp.maximum(jnp.sum(p, axis=-1, keepdims=True), TINY)
        psum[g] = p if psum[g] is None else psum[g] + p
        o_ref[:, h * KV_WIDTH:(h + 1) * KV_WIDTH] = gate_col(0, h) * _dot(p.astype(BF16), vc)

    s_idx = lax.broadcasted_iota(jnp.int32, (n_sel, Q_BLOCK), 0)
    tq_s = lax.broadcasted_iota(jnp.int32, (n_sel, Q_BLOCK), 1) + t0
    valid = s_idx * SEL_BLOCK <= tq_s
    forced = (s_idx == 0) | (s_idx == tq_s // SEL_BLOCK)
    pad_lo = jnp.zeros((FEAT_SEL0, Q_BLOCK), F32)
    pad_hi = jnp.zeros((LANES - FEAT_SEL0 - n_sel, Q_BLOCK), F32)
    sel_lanes = []
    for g in range(NSA_KV_HEADS):
        hi = psum[g].astype(BF16)
        lo = (psum[g] - hi.astype(F32)).astype(BF16)
        imp_t = _dot_nt(ovt_ref[...], hi) + _dot_nt(ovt_ref[...], lo)
        score_t = jnp.where(valid, imp_t + jnp.where(forced, FORCE_BONUS, 0.0), NEG_INF)
        sel_t = _rank_select_t(score_t, n_sel, topn)
        parts = [pad_lo, sel_t] + ([pad_hi] if pad_hi.shape[0] else [])
        sel_lanes.append(jnp.concatenate(parts, axis=0).T)

    lane = lax.broadcasted_iota(jnp.int32, (Q_BLOCK, LANES), 1)
    t0f = t0.astype(F32)

    def feat_coeffs(h, selected):
        sl = _slope(h)
        c = jnp.where(lane == FEAT_POS_HI, sl * SEL_BLOCK,
                      jnp.where(lane == FEAT_POS_LO, sl, jnp.where(lane == FEAT_ONE, -sl * t0f, 0.0)))
        if selected:
            c = jnp.where(lane >= FEAT_SEL0, (sel_lanes[h // NSA_GROUP] - 1.0) * MASK_BIG, c)
        return c.astype(BF16)

    def q_operand(selected):
        feats = jnp.concatenate([feat_coeffs(h, selected) for h in range(NSA_HEADS)], axis=0)
        return jnp.concatenate([qs, feats], axis=1)

    row_i = lax.broadcasted_iota(jnp.int32, (Q_BLOCK, KEY_BLOCK), 0)
    lane_i = lax.broadcasted_iota(jnp.int32, (Q_BLOCK, KEY_BLOCK), 1)

    def reset():
        m_scr[...] = jnp.full(m_scr.shape, NEG_INF, F32)
        acc_scr[...] = jnp.zeros(acc_scr.shape, F32)

    grp_rows = NSA_GROUP * Q_BLOCK

    def flash_step(kb, qa, kf_ref, v_scr, masked):
        k0 = pl.multiple_of(kb * KEY_BLOCK, KEY_BLOCK)
        s_all = _dot(qa, kf_ref[:, pl.ds(k0, KEY_BLOCK)])
        if masked:
            dist = (t0 - k0) + (row_i - lane_i)
            keep = jnp.where(dist >= 0, dist, WINDOW) < WINDOW
        ps = []
        alphas = []
        for h in range(NSA_HEADS):
            rows = slice(h * Q_BLOCK, (h + 1) * Q_BLOCK)
            s = s_all[rows]
            if masked:
                s = jnp.where(keep, s, NEG_INF)
            m_prev = m_scr[rows]
            m_new = jnp.maximum(m_prev, jnp.max(s, axis=-1, keepdims=True))
            p = jnp.exp(s - jnp.concatenate([m_new] * (KEY_BLOCK // LANES), axis=1))
            if masked:
                p = jnp.where(keep, p, 0.0)
            alpha = jnp.exp(m_prev - m_new)
            m_scr[rows] = m_new
            ps.append(p.astype(BF16))
            alphas.append(alpha)
        p_all = jnp.concatenate(ps, axis=0)
        pv = [_dot_nt(p_all[g * grp_rows:(g + 1) * grp_rows], v_scr[g, :, pl.ds(k0, KEY_BLOCK)])
              for g in range(NSA_KV_HEADS)]
        acc_scr[...] = jnp.concatenate(alphas, axis=0) * acc_scr[...] + jnp.concatenate(pv, axis=0)

    lane_o = lax.broadcasted_iota(jnp.int32, (Q_BLOCK, KV_WIDTH), 1)

    def finish(branch):
        for h in range(NSA_HEADS):
            rows = slice(h * Q_BLOCK, (h + 1) * Q_BLOCK)
            cols = slice(h * KV_WIDTH, (h + 1) * KV_WIDTH)
            a = acc_scr[rows]
            total = pltpu.roll(a, half, 1)
            own = (lane_o < half) if h < NSA_GROUP else (lane_o >= half)
            o_b = jnp.where(own, a / jnp.maximum(total, TINY), 0.0)
            o_ref[:, cols] = o_ref[:, cols] + gate_col(branch, h) * o_b

    def loop(lo, hi, qa, kf_ref, v_scr):
        n = jnp.maximum(hi - lo, 0)

        def pair(i, c):
            flash_step(lo + 2 * i, qa, kf_ref, v_scr, False)
            flash_step(lo + 2 * i + 1, qa, kf_ref, v_scr, False)
            return c

        lax.fori_loop(0, n // 2, pair, 0)

        @pl.when(n % 2 == 1)
        def _():
            flash_step(hi - 1, qa, kf_ref, v_scr, False)

    per_key_block = KEY_BLOCK // Q_BLOCK
    kb_diag = qi // per_key_block
    reset()
    qa = q_operand(True)
    loop(0, kb_diag, qa, kfs_scr, vs_scr)
    flash_step(kb_diag, qa, kfs_scr, vs_scr, True)
    finish(1)
    reset()
    qa = q_operand(False)
    kb_lo = jnp.maximum(qi - WINDOW // Q_BLOCK, 0) // per_key_block

    @pl.when(kb_lo < kb_diag)
    def _():
        flash_step(kb_lo, qa, kfw_scr, vw_scr, True)

    loop(kb_lo + 1, kb_diag, qa, kfw_scr, vw_scr)
    flash_step(kb_diag, qa, kfw_scr, vw_scr, True)
    finish(2)


def prompt_attention(qp, kct, vc, kvt_b, r, batch, seq):
    assert seq % KEY_BLOCK == 0 and seq // SEL_BLOCK <= LANES - FEAT_SEL0
    nqb = seq // Q_BLOCK
    n_c = kct.shape[2]
    n_sel = seq // SEL_BLOCK
    topn = min(SEL_TOPN, n_sel)
    cs = np.arange(n_c)[:, None] * CMP_STRIDE
    ss = np.arange(n_sel)[None, :] * SEL_BLOCK
    ov = np.clip(np.minimum(cs + CMP_BLOCK, ss + SEL_BLOCK) - np.maximum(cs, ss), 0, None) / CMP_BLOCK
    ov[(seq - CMP_BLOCK) // CMP_STRIDE + 1:] = 0.0
    kv_spec = lambda idx: pl.BlockSpec((None, None, KV_WIDTH, seq), lambda b, i: (idx, b, 0, 0))
    return pl.pallas_call(
        functools.partial(_pattn_kernel, n_sel=n_sel, topn=topn),
        grid=(batch, nqb),
        in_specs=[
            pl.BlockSpec((Q_BLOCK, PADDED_Q), lambda b, i: (b * nqb + i, 0)),
            pl.BlockSpec((None, KV_WIDTH, n_c), lambda b, i: (b, 0, 0)),
            pl.BlockSpec((None, n_c, KV_WIDTH), lambda b, i: (b, 0, 0)),
            kv_spec(2), kv_spec(3), kv_spec(4), kv_spec(5),
            pl.BlockSpec((LANES, seq), lambda b, i: (0, 0)),
            pl.BlockSpec((Q_BLOCK, LANES), lambda b, i: (b * nqb + i, R_SMALL // LANES)),
            pl.BlockSpec((n_sel, n_c), lambda b, i: (0, 0)),
        ],
        out_specs=pl.BlockSpec((Q_BLOCK, PADDED_Q), lambda b, i: (b * nqb + i, 0)),
        out_shape=jax.ShapeDtypeStruct((batch * seq, PADDED_Q), F32),
        scratch_shapes=[
            pltpu.VMEM((2 * KV_WIDTH, seq), BF16),
            pltpu.VMEM((2 * KV_WIDTH, seq), BF16),
            pltpu.VMEM((NSA_KV_HEADS, KV_WIDTH, seq), BF16),
            pltpu.VMEM((NSA_KV_HEADS, KV_WIDTH, seq), BF16),
            pltpu.VMEM((NSA_HEADS * Q_BLOCK, LANES), F32),
            pltpu.VMEM((NSA_HEADS * Q_BLOCK, KV_WIDTH), F32),
        ],
        compiler_params=_cparams("parallel", "arbitrary"),
        name="prompt_nsa",
    )(qp, kct, vc, kvt_b, kvt_b, kvt_b, kvt_b, jnp.asarray(_feature_rows(seq), BF16), r,
      jnp.asarray(ov.T, BF16))


def _gla_log_decay(sm, aup_ref, ab_ref):
    z = _dot(sm.astype(BF16), aup_ref[...]) + ab_ref[...]
    return _log_sigmoid(z) / GLA_GATE_TAU


def _gla_head_norm(o, gg_ref):
    on = o * lax.rsqrt(jnp.mean(o * o, axis=-1, keepdims=True) + NORM_EPS)
    return on * gg_ref[...]


def _pgla_kernel(qg_ref, kg_ref, vg_ref, sm_ref, aup_ref, ab_ref, gg_ref, og_ref, sfin_ref, st_scr, *, n_tiles):
    ti = pl.program_id(1)
    tt = qg_ref.shape[0]
    width = qg_ref.shape[1]

    @pl.when(ti == 0)
    def _():
        st_scr[...] = jnp.zeros(st_scr.shape, F32)

    la = _gla_log_decay(sm_ref[...], aup_ref, ab_ref)
    rin = lax.broadcasted_iota(jnp.int32, la.shape, 0)
    bg = la
    sh = 1
    while sh < tt:
        bg = bg + jnp.where(rin >= sh, pltpu.roll(bg, sh, 0), 0.0)
        sh *= 2

    def row_of_block(x, block, row):
        x3 = x.reshape(tt // block, block, width)
        return jnp.broadcast_to(x3[:, row:row + 1, :], x3.shape).reshape(tt, width)

    n_chunks = tt // GLA_CHUNK
    ends = bg.reshape(n_chunks, GLA_CHUNK, width)[:, GLA_CHUNK - 1:GLA_CHUNK, :]
    prev_end = jnp.concatenate([jnp.zeros((1, 1, width), F32), ends[:n_chunks - 1]], axis=0)
    b = (bg.reshape(n_chunks, GLA_CHUNK, width) - prev_end).reshape(tt, width)
    bg_end = bg[tt - 1:tt, :]
    scales = [(jnp.exp(b), jnp.exp(-b))]
    block = 2 * GLA_CHUNK
    while block <= tt:
        mid = row_of_block(bg, block, block // 2 - 1)
        scales.append((jnp.exp(jnp.minimum(bg - mid, 0.0)), jnp.exp(jnp.minimum(mid - bg, 0.0))))
        block *= 2
    e_in = jnp.exp(bg)
    e_out = jnp.exp(bg_end - bg)

    r_i = lax.broadcasted_iota(jnp.int32, (tt, tt), 0)
    c_i = lax.broadcasted_iota(jnp.int32, (tt, tt), 1)
    level = jnp.full((tt, tt), len(scales) - 1, jnp.int32)
    block = tt // 2
    lv = len(scales) - 2
    while block >= GLA_CHUNK:
        level = jnp.where(r_i // block == c_i // block, lv, level)
        block //= 2
        lv -= 1
    level = jnp.where(c_i <= r_i, level, -1)

    for h in range(GLA_HEADS):
        ks = slice(h * GLA_DK, (h + 1) * GLA_DK)
        vs = slice(h * GLA_DV, (h + 1) * GLA_DV)
        q = qg_ref[:, ks] * (GLA_DK ** -0.5)
        k = kg_ref[:, ks]
        v = vg_ref[:, vs].astype(BF16)
        a = jnp.zeros((tt, tt), F32)
        for lv, (sq, sk) in enumerate(scales):
            a_lv = _dot_nt((q * sq[:, ks]).astype(BF16), (k * sk[:, ks]).astype(BF16))
            a = jnp.where(level == lv, a_lv, a)
        st = st_scr[h]
        o = _dot(a.astype(BF16), v) + _dot_nt((q * e_in[:, ks]).astype(BF16), st.astype(BF16))
        kv_t = lax.dot_general(v, (k * e_out[:, ks]).astype(BF16), (((0,), (0,)), ((), ())),
                               preferred_element_type=F32)
        st_scr[h] = st * jnp.exp(bg_end[:, ks]) + kv_t
        og_ref[:, vs] = _gla_head_norm(o, gg_ref)

    @pl.when(ti == n_tiles - 1)
    def _():
        for h in range(GLA_HEADS):
            sfin_ref[h] = st_scr[h].T


def prompt_gla(r, aup, ab, gg, batch, seq):
    tt = _tile(seq, 256, GLA_CHUNK)
    nt = seq // tt
    row = lambda b, i: b * nt + i
    return pl.pallas_call(
        functools.partial(_pgla_kernel, n_tiles=nt),
        grid=(batch, nt),
        in_specs=[
            pl.BlockSpec((tt, GLA_KEY_WIDTH), lambda b, i: (row(b, i), R_QG // GLA_KEY_WIDTH)),
            pl.BlockSpec((tt, GLA_KEY_WIDTH), lambda b, i: (row(b, i), R_KG // GLA_KEY_WIDTH)),
            pl.BlockSpec((tt, GLA_VAL_WIDTH), lambda b, i: (row(b, i), R_VG // GLA_VAL_WIDTH)),
            pl.BlockSpec((tt, LANES), lambda b, i: (row(b, i), R_SMALL // LANES)),
            pl.BlockSpec(aup.shape, lambda b, i: (0, 0)),
            pl.BlockSpec(ab.shape, lambda b, i: (0, 0)),
            pl.BlockSpec(gg.shape, lambda b, i: (0, 0)),
        ],
        out_specs=[
            pl.BlockSpec((tt, GLA_VAL_WIDTH), lambda b, i: (row(b, i), 0)),
            pl.BlockSpec((None, GLA_HEADS, GLA_DK, GLA_DV), lambda b, i: (b, 0, 0, 0)),
        ],
        out_shape=[
            jax.ShapeDtypeStruct((batch * seq, GLA_VAL_WIDTH), F32),
            jax.ShapeDtypeStruct((batch, GLA_HEADS, GLA_DK, GLA_DV), F32),
        ],
        scratch_shapes=[pltpu.VMEM((GLA_HEADS, GLA_DV, GLA_DK), F32)],
        compiler_params=_cparams("parallel", "arbitrary"),
        name="prompt_gla",
    )(r, r, r, r, aup, ab, gg)


def _sgla_kernel(qg_ref, kg_ref, vg_ref, sm_ref, aup_ref, ab_ref, gg_ref, s_ref, og_ref, snew_ref, *, ds):
    rows = qg_ref.shape[0]
    nb = rows // ds
    la = _gla_log_decay(sm_ref[...], aup_ref, ab_ref)
    ri = lax.broadcasted_iota(jnp.int32, la.shape, 0) % ds
    b = la
    sh = 1
    while sh < ds:
        b = b + jnp.where(ri >= sh, pltpu.roll(b, sh, 0), 0.0)
        sh *= 2
    b_last = b
    for d in range(1, ds):
        b_last = jnp.where(ri == ds - 1 - d, pltpu.roll(b, rows - d, 0), b_last)
    e_b = jnp.exp(b)
    e_nb = jnp.exp(-b)
    e_tail = jnp.exp(b_last - b)
    e_last = jnp.exp(b_last)

    r_i = lax.broadcasted_iota(jnp.int32, (rows, rows), 0)
    c_i = lax.broadcasted_iota(jnp.int32, (rows, rows), 1)
    causal = (r_i // ds == c_i // ds) & (c_i <= r_i)
    row_b = lax.broadcasted_iota(jnp.int32, (rows, GLA_DV), 0) // ds

    for h in range(GLA_HEADS):
        ks = slice(h * GLA_DK, (h + 1) * GLA_DK)
        vs = slice(h * GLA_DV, (h + 1) * GLA_DV)
        q = qg_ref[:, ks] * (GLA_DK ** -0.5)
        k = kg_ref[:, ks]
        v = vg_ref[:, vs]
        q_dec = (q * e_b[:, ks]).astype(BF16)
        k_inv = (k * e_nb[:, ks]).astype(BF16)
        k_tail_t = (k * e_tail[:, ks]).T
        e_last_t = e_last[:, ks].T
        a = jnp.where(causal, _dot_nt(q_dec, k_inv), 0.0)
        o = _dot(a.astype(BF16), v.astype(BF16))
        for bb in range(nb):
            s_prev = s_ref[bb, h]
            o_inter = _dot(q_dec, s_prev.astype(BF16))
            o = o + jnp.where(row_b == bb, o_inter, 0.0)
            s_new = e_last_t[:, bb * ds:bb * ds + 1] * s_prev
            for j in range(ds):
                rr = bb * ds + j
                s_new = s_new + k_tail_t[:, rr:rr + 1] * v[rr:rr + 1, :]
            snew_ref[bb, h] = s_new
        og_ref[:, vs] = _gla_head_norm(o, gg_ref)


def sample_gla(r, aup, ab, gg, state, layer, dec_batch, ds):
    nb = _tile(dec_batch, 8, 1)
    rows = nb * ds
    return pl.pallas_call(
        functools.partial(_sgla_kernel, ds=ds),
        grid=(dec_batch // nb,),
        in_specs=[
            pl.BlockSpec((rows, GLA_KEY_WIDTH), lambda i: (i, R_QG // GLA_KEY_WIDTH)),
            pl.BlockSpec((rows, GLA_KEY_WIDTH), lambda i: (i, R_KG // GLA_KEY_WIDTH)),
            pl.BlockSpec((rows, GLA_VAL_WIDTH), lambda i: (i, R_VG // GLA_VAL_WIDTH)),
            pl.BlockSpec((rows, LANES), lambda i: (i, R_SMALL // LANES)),
            pl.BlockSpec(aup.shape, lambda i: (0, 0)),
            pl.BlockSpec(ab.shape, lambda i: (0, 0)),
            pl.BlockSpec(gg.shape, lambda i: (0, 0)),
            pl.BlockSpec((None, nb, GLA_HEADS, GLA_DK, GLA_DV), lambda i: (layer, i, 0, 0, 0)),
        ],
        out_specs=[
            pl.BlockSpec((rows, GLA_VAL_WIDTH), lambda i: (i, 0)),
            pl.BlockSpec((nb, GLA_HEADS, GLA_DK, GLA_DV), lambda i: (i, 0, 0, 0)),
        ],
        out_shape=[
            jax.ShapeDtypeStruct((dec_batch * ds, GLA_VAL_WIDTH), F32),
            jax.ShapeDtypeStruct((dec_batch, GLA_HEADS, GLA_DK, GLA_DV), F32),
        ],
        compiler_params=_cparams("parallel"),
        name="sample_gla",
    )(r, r, r, r, aup, ab, gg, state)


def _mix_kernel(x_ref, on_ref, za_ref, og_ref, zb_ref, mg_ref, wba_ref, wbb_ref, wo_ref, fg_ref, *o_refs):
    pa = _dot((on_ref[...] * _silu(za_ref[...])).astype(BF16), wba_ref[...])
    pb = _dot((og_ref[...] * _silu(zb_ref[...])).astype(BF16), wbb_ref[...])
    mix = _sigmoid(mg_ref[:, :D_MODEL]) * pa + _sigmoid(mg_ref[:, D_MODEL:]) * pb
    y = x_ref[...] + _dot(mix.astype(BF16), wo_ref[...])
    o_refs[0][...] = y
    if len(o_refs) > 1:
        yn = y * lax.rsqrt(jnp.mean(y * y, axis=-1, keepdims=True) + NORM_EPS)
        o_refs[1][...] = yn * fg_ref[...]


def mixer_output(x, o_nsa, r, o_gla, wba, wbb, wo, final_g, with_final_norm):
    n = x.shape[0]
    tm = _tile(n, 256)
    row = lambda w: pl.BlockSpec((tm, w), lambda i: (i, 0))
    wspec = lambda w: pl.BlockSpec(w.shape, lambda i: (0, 0))
    n_out = 2 if with_final_norm else 1
    return pl.pallas_call(
        _mix_kernel,
        grid=(n // tm,),
        in_specs=[
            row(D_MODEL), row(PADDED_Q),
            pl.BlockSpec((tm, PADDED_Q), lambda i: (i, R_ZA // PADDED_Q)),
            row(GLA_VAL_WIDTH),
            pl.BlockSpec((tm, GLA_VAL_WIDTH), lambda i: (i, R_ZB // GLA_VAL_WIDTH)),
            pl.BlockSpec((tm, 2 * D_MODEL), lambda i: (i, R_MG // (2 * D_MODEL))),
            wspec(wba), wspec(wbb), wspec(wo), wspec(final_g),
        ],
        out_specs=[row(D_MODEL)] * n_out,
        out_shape=[jax.ShapeDtypeStruct((n, D_MODEL), F32)] * n_out,
        compiler_params=_cparams("parallel"),
        name="mixer_output",
    )(x, o_nsa, r, o_gla, r, r, wba, wbb, wo, final_g)


def _select_blocks(score, lane_idx, n_cand, topn):
    rank = jnp.zeros(score.shape, jnp.int32)
    for j in range(n_cand):
        col = score[:, j:j + 1]
        rank = rank + jnp.where(col > score, 1, 0) + jnp.where(col == score, jnp.where(lane_idx > j, 1, 0), 0)
    return jnp.where(rank < topn, 1.0, 0.0)


def _softmax_two(parts, keeps):
    m = functools.reduce(jnp.maximum, [jnp.max(s, axis=-1, keepdims=True) for s in parts])
    ps = [jnp.where(kp, jnp.exp(s - m), 0.0) for s, kp in zip(parts, keeps)]
    total = functools.reduce(lambda a, c: a + c, [jnp.sum(p, axis=-1, keepdims=True) for p in ps])
    return ps, jnp.maximum(total, TINY)


def _smain_kernel(pt_ref, q_ref, gate_ref, slope_ref, *rest, n_pages, past, ds, n_sel, topn):
    del pt_ref
    rest = list(rest)
    take = lambda n: [rest.pop(0) for _ in range(n)]
    ab_refs, kt_refs, vt_refs = take(n_pages), take(n_pages), take(n_pages)
    kw_ref, vw_ref, new_ref, w2k_ref, w2v_ref, ov_ref, exp_ref, exn_ref = take(8)
    o_ref, kw_out_ref, vw_out_ref = take(3)
    rows = q_ref.shape[0]
    grp_rows = NSA_KV_HEADS * ds
    n_new = new_ref.shape[2]
    b_loc = pl.program_id(0) % (n_new // ds)

    qs = q_ref[...]
    slope = slope_ref[:, 0:1]
    ab = jnp.concatenate([r[...] for r in ab_refs], axis=0)
    hid = ab.shape[1] // 4
    kc = _summaries(ab[:, 0:hid], _shift_up(ab[:, hid:2 * hid]), w2k_ref[...]).astype(BF16)
    vc = _summaries(ab[:, 2 * hid:3 * hid], _shift_up(ab[:, 3 * hid:]), w2v_ref[...]).astype(BF16)
    n_c = kc.shape[0]

    def irow(shape):
        return lax.broadcasted_iota(jnp.int32, shape, 0) % ds

    dist_c = (past + irow((rows, n_c))) - (lax.broadcasted_iota(jnp.int32, (rows, n_c), 1) * CMP_STRIDE
                                          + (CMP_BLOCK - 1))
    vis_c = dist_c >= 0
    s_c = jnp.where(vis_c, _dot_nt(qs, kc) - slope * dist_c.astype(F32), NEG_INF)
    m_c = jnp.max(s_c, axis=-1, keepdims=True)
    p_c = jnp.where(vis_c, jnp.exp(s_c - m_c), 0.0)
    p_c = p_c / jnp.maximum(jnp.sum(p_c, axis=-1, keepdims=True), TINY)
    o_cmp = _dot(p_c.astype(BF16), vc)

    hi = p_c.astype(BF16)
    lo = (p_c - hi.astype(F32)).astype(BF16)
    imp_h = _dot(hi, ov_ref[...]) + _dot(lo, ov_ref[...])
    imp = functools.reduce(lambda a, c: a + c,
                           [imp_h[r * grp_rows:(r + 1) * grp_rows] for r in range(NSA_GROUP)])
    s_idx = lax.broadcasted_iota(jnp.int32, imp.shape, 1)
    qpos = past + irow(imp.shape)
    valid = (s_idx < n_sel) & (s_idx * SEL_BLOCK <= qpos)
    forced = (s_idx == 0) | (s_idx == qpos // SEL_BLOCK)
    score = jnp.where(valid, imp + jnp.where(forced, FORCE_BONUS, 0.0), NEG_INF)
    sel = _select_blocks(score, s_idx, n_sel, topn)
    sel = jnp.concatenate([sel] * NSA_GROUP, axis=0).astype(BF16)
    keep_past = _dot(sel, exp_ref[...]) > 0.5
    sel_new = _dot(sel, exn_ref[...]) > 0.5

    slot = lax.broadcasted_iota(jnp.int32, (rows, n_new), 1)
    dist_n = irow((rows, n_new)) - slot % ds
    mine = (slot // ds == b_loc) & (dist_n >= 0)
    bias_n = slope * dist_n.astype(F32)

    s_p = jnp.concatenate([_dot(qs, r[...].astype(BF16)) for r in kt_refs], axis=1)
    dist_p = (past + irow(s_p.shape)) - lax.broadcasted_iota(jnp.int32, s_p.shape, 1)
    s_p = jnp.where(keep_past, s_p - slope * dist_p.astype(F32), NEG_INF)
    keep_n = mine & sel_new
    s_n = jnp.where(keep_n, _dot(qs, new_ref[2].astype(BF16)) - bias_n, NEG_INF)
    (p_p, p_n), total = _softmax_two([s_p, s_n], [keep_past, keep_n])
    acc = _dot_nt(p_n.astype(BF16), new_ref[3].astype(BF16))
    for j, r in enumerate(vt_refs):
        acc = acc + _dot_nt(p_p[:, j * LANES:(j + 1) * LANES].astype(BF16), r[...].astype(BF16))
    o_slc = acc / total

    kbuf = kw_ref[...]
    vbuf = vw_ref[...]
    w_buf = kbuf.shape[1]
    dist_w = (w_buf + irow((rows, w_buf))) - lax.broadcasted_iota(jnp.int32, (rows, w_buf), 1)
    keep_w = dist_w < WINDOW
    s_w = jnp.where(keep_w, _dot(qs, kbuf.astype(BF16)) - slope * dist_w.astype(F32), NEG_INF)
    s_wn = jnp.where(mine, _dot(qs, new_ref[4].astype(BF16)) - bias_n, NEG_INF)
    (p_w, p_wn), total_w = _softmax_two([s_w, s_wn], [keep_w, mine])
    o_win = (_dot_nt(p_w.astype(BF16), vbuf.astype(BF16)) + _dot_nt(p_wn.astype(BF16), new_ref[5].astype(BF16))) / total_w

    gates = _sigmoid(gate_ref[...])
    o_ref[...] = gates[:, 0:1] * o_cmp + gates[:, 1:2] * o_slc + gates[:, 2:3] * o_win

    lane_new = lax.broadcasted_iota(jnp.int32, (KV_WIDTH, n_new), 1)
    shift = (n_new - ds) - b_loc * ds
    for buf, slab, out_ref in ((kbuf, 4, kw_out_ref), (vbuf, 5, vw_out_ref)):
        rolled = pltpu.roll(buf, w_buf - ds, 1)
        own_last = pltpu.roll(new_ref[slab], shift, 1)
        out_ref[:, 0:w_buf - n_new] = rolled[:, 0:w_buf - n_new]
        out_ref[:, w_buf - n_new:w_buf] = jnp.where(lane_new >= n_new - ds, own_last, rolled[:, w_buf - n_new:w_buf])


def sample_attention(page_table, q_rows, gate_rows, ab_pages, slc_pools_t, win_bufs_t, new_t, w2k, w2v, layer, past, ds):
    assert past % SEL_BLOCK == 0 and past % LANES == 0
    dec_batch, n_pages = page_table.shape
    rows = q_rows.shape[1]
    w_buf = win_bufs_t[0].shape[3]
    n_new = min(LANES, dec_batch * ds)
    assert (dec_batch * ds) % n_new == 0 and n_new % ds == 0 and w_buf >= n_new
    n_c = n_pages * (LANES // CMP_STRIDE)
    n_cmp = (past + ds - CMP_BLOCK) // CMP_STRIDE + 1
    n_sel = -(-(past + ds) // SEL_BLOCK)
    topn = min(SEL_TOPN, n_sel)
    assert n_sel <= LANES and ds <= CMP_STRIDE
    cs = np.arange(n_c)[:, None] * CMP_STRIDE
    ss = np.arange(LANES)[None, :] * SEL_BLOCK
    ov = np.clip(np.minimum(cs + CMP_BLOCK, ss + SEL_BLOCK) - np.maximum(cs, ss), 0, None) / CMP_BLOCK
    ov[n_cmp:] = 0.0
    ov[:, n_sel:] = 0.0
    ex_past = (np.arange(past)[None, :] // SEL_BLOCK == np.arange(LANES)[:, None]).astype(np.float32)
    ex_new = ((past + np.arange(n_new) % ds)[None, :] // SEL_BLOCK == np.arange(LANES)[:, None]).astype(np.float32)
    head = np.arange(rows) // (NSA_KV_HEADS * ds) + NSA_GROUP * ((np.arange(rows) // ds) % NSA_KV_HEADS)
    slope = np.broadcast_to((2.0 ** -(head + 1.0))[:, None], (rows, LANES)).astype(np.float32)

    const = lambda shape: pl.BlockSpec(shape, lambda b, pt: (0,) * len(shape))
    page = lambda j, shape: pl.BlockSpec((None,) + shape, lambda b, pt: (pt[b, j], 0, 0))
    pool = lambda j: pl.BlockSpec((None, None, KV_WIDTH, LANES), lambda b, pt: (layer, pt[b, j], 0, 0))
    in_specs = [pl.BlockSpec((None, rows, LANES), lambda b, pt: (b, 0, 0)),
                pl.BlockSpec((None, rows, LANES), lambda b, pt: (b, 0, 0)),
                const((rows, LANES))]
    operands = [q_rows, gate_rows, jnp.asarray(slope)]
    in_specs += [page(j, ab_pages.shape[1:]) for j in range(n_pages)]
    operands += [ab_pages] * n_pages
    for arr in slc_pools_t:
        in_specs += [pool(j) for j in range(n_pages)]
        operands += [arr] * n_pages
    for arr in win_bufs_t:
        in_specs.append(pl.BlockSpec((None, None, KV_WIDTH, w_buf), lambda b, pt: (layer, b, 0, 0)))
        operands.append(arr)
    per_tile = n_new // ds
    in_specs.append(pl.BlockSpec((6, None, KV_WIDTH, n_new), lambda b, pt: (0, 0, 0, b // per_tile)))
    operands.append(new_t)
    consts = [w2k, w2v, jnp.asarray(ov, BF16), jnp.asarray(ex_past, BF16), jnp.asarray(ex_new, BF16)]
    in_specs += [const(c.shape) for c in consts]
    operands += consts
    buf_out = pl.BlockSpec((None, KV_WIDTH, w_buf), lambda b, pt: (b, 0, 0))
    return pl.pallas_call(
        functools.partial(_smain_kernel, n_pages=n_pages, past=past, ds=ds, n_sel=n_sel, topn=topn),
        grid_spec=pltpu.PrefetchScalarGridSpec(
            num_scalar_prefetch=1,
            grid=(dec_batch,),
            in_specs=in_specs,
            out_specs=[pl.BlockSpec((None, rows, LANES), lambda b, pt: (b, 0, 0)), buf_out, buf_out],
        ),
        out_shape=[jax.ShapeDtypeStruct((dec_batch, rows, LANES), F32),
                   jax.ShapeDtypeStruct((dec_batch, KV_WIDTH, w_buf), F32),
                   jax.ShapeDtypeStruct((dec_batch, KV_WIDTH, w_buf), F32)],
        compiler_params=_cparams("arbitrary"),
        name="sample_nsa",
    )(page_table, *operands)


def _pad_heads_cols(w):
    k = w.shape[0]
    w4 = w.reshape(k, NSA_KV_HEADS, NSA_GROUP, HEAD_DIM)
    eye = jnp.eye(NSA_KV_HEADS, dtype=w.dtype)
    return jnp.einsum("kgrd,gp->kgrpd", w4, eye).reshape(k, PADDED_Q)


def _layer_weights(w_in, pk_pe, pk_w1, pk_w2, pv_pe, pv_w1, pv_w2, a_up, a_b, gla_g, w_ba, w_bb, w_out):
    o = _OFF
    col = lambda i: w_in[:, o[i]:o[i + 1]]
    small = jnp.concatenate([col(2), col(7), jnp.zeros((D_MODEL, LANES - 3 * NSA_HEADS - GLA_GATE_RANK), F32)], axis=1)
    w_q = _pad_heads_cols(col(0)).astype(BF16)
    w_kv_t = col(1).T.astype(BF16)
    w_r = jnp.concatenate([_pad_heads_cols(col(3)), col(4), col(5), col(6), col(8), col(9), small], axis=1).astype(BF16)
    half = CMP_BLOCK // 2

    def cmp_weights(pe, w1, w2):
        eye = jnp.eye(NSA_KV_HEADS, dtype=F32)
        w1h = w1.reshape(2, half, HEAD_DIM, CMP_HIDDEN)
        big = lambda w: jnp.einsum("ldh,gk->lgdkh", w, eye).reshape(half * KV_WIDTH, NSA_KV_HEADS * CMP_HIDDEN)
        peh = pe.reshape(2, half, 1, HEAD_DIM)
        pe_row = lambda p: jnp.broadcast_to(p, (half, NSA_KV_HEADS, HEAD_DIM)).reshape(1, half * KV_WIDTH)
        w2bd = jnp.einsum("hd,gk->ghkd", w2, eye).reshape(NSA_KV_HEADS * CMP_HIDDEN, KV_WIDTH)
        return (pe_row(peh[0]), pe_row(peh[1]), big(w1h[0]).astype(BF16), big(w1h[1]).astype(BF16)), w2bd.astype(BF16)

    cw_k, w2k = cmp_weights(pk_pe, pk_w1, pk_w2)
    cw_v, w2v = cmp_weights(pv_pe, pv_w1, pv_w2)
    aup = jnp.zeros((LANES, GLA_KEY_WIDTH), F32).at[ALOW_LANE0:ALOW_LANE0 + GLA_GATE_RANK].set(a_up).astype(BF16)
    w_ba_p = _pad_heads_cols(w_ba.T).T.astype(BF16)
    return dict(w_q=w_q, w_kv_t=w_kv_t, w_r=w_r, cw_k=cw_k, cw_v=cw_v, w2k=w2k, w2v=w2v, aup=aup,
                ab=a_b.reshape(1, GLA_KEY_WIDTH), gg=gla_g.reshape(1, GLA_DV),
                w_ba=w_ba_p, w_bb=w_bb.astype(BF16), w_out=w_out.astype(BF16))


def _project(x, g_norm, lw, batch, seq):
    h = rms_norm_rows(x, g_norm, BF16)
    qp = matmul_rows(h, lw["w_q"], BF16, tn=512, scale=HEAD_DIM ** -0.5)
    kvt_f, kvt_b = kv_project_t(h, lw["w_kv_t"], batch, seq)
    r = matmul_rows(h, lw["w_r"], F32, tn=R_WIDTH // 7)
    return qp, kvt_f, kvt_b, r


def _tokens_minor(cache):
    lead = cache.shape[:-3]
    n = cache.ndim
    perm = tuple(range(n - 3)) + (n - 2, n - 1, n - 3)
    return cache.transpose(perm).reshape(lead + (KV_WIDTH, cache.shape[-3]))


def _tokens_major(x_t):
    lead = x_t.shape[:-2]
    n = len(lead)
    x5 = x_t.reshape(lead + (NSA_KV_HEADS, HEAD_DIM, x_t.shape[-1]))
    return x5.transpose(tuple(range(n)) + (n + 2, n, n + 1))


def kernel(x_prompt, x_sample, cache_k_cmp, cache_v_cmp, cache_k_slc, cache_v_slc, cache_k_win, cache_v_win, state_gla, page_table, norm_g, w_in, phi_k_pe, phi_k_w1, phi_k_w2, phi_v_pe, phi_v_w1, phi_v_w2, gla_alpha_up, gla_alpha_b, gla_norm_g, w_branch_a, w_branch_b, w_out, final_norm_g):
    batch, seq, _ = x_prompt.shape
    dec_batch, ds, _ = x_sample.shape
    depth = norm_g.shape[0]
    n_pool, page_size = cache_k_cmp.shape[1:3]
    assert page_size == LANES
    n_pages = page_table.shape[1]
    past = n_pages * page_size
    chunks_per_page = page_size // CMP_STRIDE
    final_g = final_norm_g.reshape(1, D_MODEL)

    cmp_pools_t = (_tokens_minor(cache_k_cmp), _tokens_minor(cache_v_cmp))
    slc_pools_t = (_tokens_minor(cache_k_slc), _tokens_minor(cache_v_slc))
    win_bufs_t = (_tokens_minor(cache_k_win), _tokens_minor(cache_v_win))

    y_p = x_prompt.reshape(batch * seq, D_MODEL)
    y_s = x_sample.reshape(dec_batch * ds, D_MODEL)
    outs_p, outs_s = [], []
    for l in range(depth):
        lw = _layer_weights(w_in[l], phi_k_pe[l], phi_k_w1[l], phi_k_w2[l], phi_v_pe[l], phi_v_w1[l], phi_v_w2[l],
                            gla_alpha_up[l], gla_alpha_b[l], gla_norm_g[l], w_branch_a[l], w_branch_b[l], w_out[l])
        last = l == depth - 1

        qp, kvt_f, kvt_b, r = _project(y_p, norm_g[l], lw, batch, seq)
        kct, vc = prompt_summaries(kvt_f, lw["cw_k"], lw["w2k"], lw["cw_v"], lw["w2v"])
        o_nsa = prompt_attention(qp, kct, vc, kvt_b, r, batch, seq)
        o_gla, s_gla = prompt_gla(r, lw["aup"], lw["ab"], lw["gg"], batch, seq)
        res = mixer_output(y_p, o_nsa, r, o_gla, lw["w_ba"], lw["w_bb"], lw["w_out"], final_g, last)
        y_p = res[0]
        if last:
            y_p_out = res[1]
        keep = min(WINDOW, seq)
        kv5 = _tokens_major(kvt_f)
        outs_p.append((kv5[0], kv5[1], kv5[2], kv5[3], kv5[4][:, seq - keep:], kv5[5][:, seq - keep:], s_gla))

        qs, kvs_t, _, rs = _project(y_s, norm_g[l], lw, 1, dec_batch * ds)
        ab_pages = pool_halves(cmp_pools_t[0], cmp_pools_t[1], l, lw["cw_k"], lw["cw_v"])
        ab_pages = ab_pages.reshape(n_pool, chunks_per_page, -1)
        q_rows = qs.reshape(dec_batch, ds, NSA_KV_HEADS, NSA_GROUP, KV_WIDTH).transpose(0, 3, 2, 1, 4)
        q_rows = q_rows.reshape(dec_batch, NSA_HEADS * ds, KV_WIDTH)
        gl = rs[:, R_SMALL + GATE_LANE0:R_SMALL + GATE_LANE0 + 3 * NSA_HEADS]
        gl = gl.reshape(dec_batch, ds, 3, NSA_KV_HEADS, NSA_GROUP).transpose(0, 4, 3, 1, 2)
        gate_rows = jnp.pad(gl.reshape(dec_batch, NSA_HEADS * ds, 3), ((0, 0), (0, 0), (0, LANES - 3)))
        o_rows, kw_new, vw_new = sample_attention(page_table, q_rows, gate_rows, ab_pages, slc_pools_t, win_bufs_t,
                                                  kvs_t, lw["w2k"], lw["w2v"], l, past, ds)
        o_nsa_s = o_rows.reshape(dec_batch, NSA_GROUP, NSA_KV_HEADS, ds, KV_WIDTH).transpose(0, 3, 2, 1, 4)
        o_nsa_s = o_nsa_s.reshape(dec_batch * ds, PADDED_Q)
        o_gla_s, s_gla_s = sample_gla(rs, lw["aup"], lw["ab"], lw["gg"], state_gla, l, dec_batch, ds)
        res = mixer_output(y_s, o_nsa_s, rs, o_gla_s, lw["w_ba"], lw["w_bb"], lw["w_out"], final_g, last)
        y_s = res[0]
        if last:
            y_s_out = res[1]
        kvs5 = _tokens_major(kvs_t[:, 0]).reshape(6, dec_batch, ds, NSA_KV_HEADS, HEAD_DIM)
        outs_s.append((kvs5[0], kvs5[1], kvs5[2], kvs5[3], _tokens_major(kw_new), _tokens_major(vw_new), s_gla_s))

    stack = lambda outs: [jnp.stack(t) for t in zip(*outs)]
    return (y_p_out.reshape(batch, seq, D_MODEL), y_s_out.reshape(dec_batch, ds, D_MODEL),
            *stack(outs_p), *stack(outs_s))
```

```python
import functools

import numpy as np
import jax
import jax.numpy as jnp
from jax import lax
from jax.experimental import pallas as pl
from jax.experimental.pallas import tpu as pltpu

F32 = jnp.float32
BF16 = jnp.bfloat16

D_MODEL = 1024
NSA_HEADS = 8
NSA_KV_HEADS = 2
NSA_GROUP = NSA_HEADS // NSA_KV_HEADS
HEAD_DIM = 64
NSA_WIDTH = NSA_HEADS * HEAD_DIM
KV_WIDTH = NSA_KV_HEADS * HEAD_DIM
CMP_BLOCK = 32
CMP_STRIDE = 16
CMP_HIDDEN = 2 * HEAD_DIM
SEL_BLOCK = 64
SEL_TOPN = 16
WINDOW = 512
GLA_HEADS = 4
GLA_KEY_WIDTH = D_MODEL // 2
GLA_VAL_WIDTH = D_MODEL
GLA_DK = GLA_KEY_WIDTH // GLA_HEADS
GLA_DV = GLA_VAL_WIDTH // GLA_HEADS
GLA_GATE_RANK = 16
GLA_GATE_TAU = 16.0
GLA_CHUNK = 32
Q_BLOCK = 128
KEY_BLOCK = 256
NORM_EPS = 1e-6
NEG_INF = -1e30
TINY = 1e-30
FORCE_BONUS = 1e4
MASK_BIG = float(2.0 ** 100)
PADDED_Q = NSA_HEADS * KV_WIDTH
LANES = 128

_IN_SIZES = (NSA_WIDTH, 6 * KV_WIDTH, 3 * NSA_HEADS, NSA_WIDTH, GLA_KEY_WIDTH, GLA_KEY_WIDTH,
             GLA_VAL_WIDTH, GLA_GATE_RANK, GLA_VAL_WIDTH, 2 * D_MODEL)
_OFF = tuple(int(o) for o in np.cumsum((0,) + _IN_SIZES))

R_ZA, R_QG, R_KG, R_VG, R_ZB, R_MG, R_SMALL = 0, 1024, 1536, 2048, 3072, 4096, 6144
R_WIDTH = R_SMALL + LANES
GATE_LANE0 = 0
ALOW_LANE0 = 3 * NSA_HEADS
FEAT_POS_HI, FEAT_POS_LO, FEAT_ONE, FEAT_SEL0 = 0, 1, 2, 64

VMEM_LIMIT = 48 * 1024 * 1024


def _cparams(*sem):
    return pltpu.CompilerParams(dimension_semantics=sem, vmem_limit_bytes=VMEM_LIMIT)


def _tile(n, target, mult=8):
    if n <= target:
        return n
    t = (target // mult) * mult
    while t >= mult:
        if n % t == 0:
            return t
        t -= mult
    return n


def _sigmoid(x):
    return 1.0 / (1.0 + jnp.exp(-x))


def _silu(x):
    return x * _sigmoid(x)


def _log_sigmoid(x):
    return -(jnp.maximum(-x, 0.0) + jnp.log1p(jnp.exp(-jnp.abs(x))))


def _dot_nt(a, b):
    return lax.dot_general(a, b, (((1,), (1,)), ((), ())), preferred_element_type=F32)


def _dot(a, b):
    return jnp.dot(a, b, preferred_element_type=F32)


def _slope(h):
    return float(2.0 ** (-(h + 1)))


def _norm_kernel(x_ref, g_ref, o_ref):
    xf = x_ref[...]
    xn = xf * lax.rsqrt(jnp.mean(xf * xf, axis=-1, keepdims=True) + NORM_EPS)
    o_ref[...] = (xn * g_ref[...]).astype(o_ref.dtype)


def rms_norm_rows(x, g, out_dtype):
    n, d = x.shape
    tm = _tile(n, 512)
    return pl.pallas_call(
        _norm_kernel,
        grid=(n // tm,),
        in_specs=[pl.BlockSpec((tm, d), lambda i: (i, 0)), pl.BlockSpec((1, d), lambda i: (0, 0))],
        out_specs=pl.BlockSpec((tm, d), lambda i: (i, 0)),
        out_shape=jax.ShapeDtypeStruct((n, d), out_dtype),
        compiler_params=_cparams("parallel"),
        name="rms_norm",
    )(x, g.reshape(1, d))


def _mm_kernel(x_ref, w_ref, o_ref, *, scale):
    acc = _dot(x_ref[...], w_ref[...])
    if scale != 1.0:
        acc = acc * scale
    o_ref[...] = acc.astype(o_ref.dtype)


def matmul_rows(x, w, out_dtype, *, tn, scale=1.0):
    n, k = x.shape
    m = w.shape[1]
    tm = _tile(n, 2048)
    return pl.pallas_call(
        functools.partial(_mm_kernel, scale=scale),
        grid=(n // tm, m // tn),
        in_specs=[pl.BlockSpec((tm, k), lambda i, j: (i, 0)), pl.BlockSpec((k, tn), lambda i, j: (0, j))],
        out_specs=pl.BlockSpec((tm, tn), lambda i, j: (i, j)),
        out_shape=jax.ShapeDtypeStruct((n, m), out_dtype),
        compiler_params=_cparams("parallel", "parallel"),
        name="proj",
    )(x, w)


def _mm_t_kernel(wt_ref, x_ref, of_ref, ob_ref):
    acc = _dot_nt(wt_ref[...], x_ref[...])
    acc = acc.reshape(of_ref.shape)
    of_ref[...] = acc
    ob_ref[...] = acc.astype(ob_ref.dtype)


def kv_project_t(x, wt, batch, seq):
    k = x.shape[1]
    tm = _tile(seq, 512, LANES)
    nt = seq // tm
    n_slab = wt.shape[0] // KV_WIDTH
    out = pl.BlockSpec((n_slab, None, KV_WIDTH, tm), lambda b, i: (0, b, 0, i))
    return pl.pallas_call(
        _mm_t_kernel,
        grid=(batch, nt),
        in_specs=[pl.BlockSpec(wt.shape, lambda b, i: (0, 0)), pl.BlockSpec((tm, k), lambda b, i: (b * nt + i, 0))],
        out_specs=[out, out],
        out_shape=[jax.ShapeDtypeStruct((n_slab, batch, KV_WIDTH, seq), F32),
                   jax.ShapeDtypeStruct((n_slab, batch, KV_WIDTH, seq), BF16)],
        compiler_params=_cparams("parallel", "parallel"),
        name="proj_kv_t",
    )(wt, x)


def _chunk_rows(xs_ref, n_chunks):
    return jnp.concatenate([xs_ref[pl.ds(l, n_chunks, stride=CMP_STRIDE), :] for l in range(CMP_STRIDE)], axis=1)


def _halves(chunks, pea_ref, peb_ref, wa_ref, wb_ref):
    a = _dot((chunks + pea_ref[...]).astype(BF16), wa_ref[...])
    b = _dot((chunks + peb_ref[...]).astype(BF16), wb_ref[...])
    return a, b


def _summaries(a, b_next, w2bd):
    return _dot(_silu(a + b_next).astype(BF16), w2bd)


def _shift_up(x):
    n = x.shape[0]
    return pltpu.roll(x, n - 1, 0)


def _transpose_pages(src, xs_ref, n_pages):
    for p in range(n_pages):
        xs_ref[p * LANES:(p + 1) * LANES, :] = src(p).T


def _pool_ab_kernel(xk_ref, xv_ref, kpea, kpeb, kwa, kwb, vpea, vpeb, vwa, vwb, o_ref, xs_scr):
    n_pages = xk_ref.shape[0]
    hid = kwa.shape[1]
    for i, (x_ref, cw) in enumerate(((xk_ref, (kpea, kpeb, kwa, kwb)), (xv_ref, (vpea, vpeb, vwa, vwb)))):
        _transpose_pages(lambda p: x_ref[p], xs_scr, n_pages)
        a, b = _halves(_chunk_rows(xs_scr, n_pages * (LANES // CMP_STRIDE)), *cw)
        o_ref[:, (2 * i) * hid:(2 * i + 1) * hid] = a
        o_ref[:, (2 * i + 1) * hid:(2 * i + 2) * hid] = b


def pool_halves(pool_k_t, pool_v_t, layer, cw_k, cw_v):
    n_pool = pool_k_t.shape[1]
    pg = _tile(n_pool, 16, 1)
    rows = pg * (LANES // CMP_STRIDE)
    hid = cw_k[2].shape[1]
    full = lambda a: pl.BlockSpec(a.shape, lambda i: (0,) * a.ndim)
    page_spec = pl.BlockSpec((None, pg, KV_WIDTH, LANES), lambda i: (layer, i, 0, 0))
    return pl.pallas_call(
        _pool_ab_kernel,
        grid=(n_pool // pg,),
        in_specs=[page_spec, page_spec] + [full(c) for c in (*cw_k, *cw_v)],
        out_specs=pl.BlockSpec((rows, 4 * hid), lambda i: (i, 0)),
        out_shape=jax.ShapeDtypeStruct((n_pool * (LANES // CMP_STRIDE), 4 * hid), F32),
        scratch_shapes=[pltpu.VMEM((pg * LANES, KV_WIDTH), F32)],
        compiler_params=_cparams("parallel"),
        name="pool_halves",
    )(pool_k_t, pool_v_t, *cw_k, *cw_v)


def _pcmp_kernel(kt_ref, vt_ref, kpea, kpeb, kwa, kwb, kw2, vpea, vpeb, vwa, vwb, vw2, featc_ref,
                 kcf_ref, vc2_ref, xs_scr):
    seq = kt_ref.shape[1]
    n_blocks = seq // LANES
    n_chunks = seq // CMP_STRIDE
    half = KV_WIDTH // NSA_KV_HEADS
    for src_ref, (pea, peb, wa, wb, w2), is_key in ((kt_ref, (kpea, kpeb, kwa, kwb, kw2), True),
                                                   (vt_ref, (vpea, vpeb, vwa, vwb, vw2), False)):
        _transpose_pages(lambda p: src_ref[:, p * LANES:(p + 1) * LANES], xs_scr, n_blocks)
        a, b = _halves(_chunk_rows(xs_scr, n_chunks), pea, peb, wa, wb)
        rows = _summaries(a, _shift_up(b), w2[...])
        if is_key:
            kcf_ref[0:KV_WIDTH, :] = rows.T.astype(kcf_ref.dtype)
            kcf_ref[KV_WIDTH:, :] = featc_ref[...]
        else:
            lane = lax.broadcasted_iota(jnp.int32, rows.shape, 1)
            vc2_ref[0] = jnp.where(lane < half, rows, 1.0).astype(vc2_ref.dtype)
            vc2_ref[1] = jnp.where(lane >= half, rows, 1.0).astype(vc2_ref.dtype)


def _cmp_feature_rows(n_c):
    cend = np.arange(n_c) * CMP_STRIDE + (CMP_BLOCK - 1)
    f = np.zeros((LANES, n_c), np.float32)
    f[FEAT_POS_HI] = cend // SEL_BLOCK
    f[FEAT_POS_LO] = cend % SEL_BLOCK
    f[FEAT_ONE] = 1.0
    return f


def prompt_summaries(kvt_f, cw_k, w2k, cw_v, w2v):
    _, batch, _, seq = kvt_f.shape
    n_c = seq // CMP_STRIDE
    consts = [*cw_k, w2k, *cw_v, w2v, jnp.asarray(_cmp_feature_rows(n_c), BF16)]
    full = lambda a: pl.BlockSpec(a.shape, lambda b: (0,) * a.ndim)
    return pl.pallas_call(
        _pcmp_kernel,
        grid=(batch,),
        in_specs=[pl.BlockSpec((None, None, KV_WIDTH, seq), lambda b: (0, b, 0, 0)),
                  pl.BlockSpec((None, None, KV_WIDTH, seq), lambda b: (1, b, 0, 0))] + [full(c) for c in consts],
        out_specs=[pl.BlockSpec((None, 2 * KV_WIDTH, n_c), lambda b: (b, 0, 0)),
                   pl.BlockSpec((None, NSA_KV_HEADS, n_c, KV_WIDTH), lambda b: (b, 0, 0, 0))],
        out_shape=[jax.ShapeDtypeStruct((batch, 2 * KV_WIDTH, n_c), BF16),
                   jax.ShapeDtypeStruct((batch, NSA_KV_HEADS, n_c, KV_WIDTH), BF16)],
        scratch_shapes=[pltpu.VMEM((seq, KV_WIDTH), F32)],
        compiler_params=_cparams("parallel"),
        name="prompt_summaries",
    )(kvt_f, kvt_f, *consts)


def _feature_rows(seq):
    kpos = np.arange(seq)
    f = np.zeros((LANES, seq), np.float32)
    f[FEAT_POS_HI] = kpos // SEL_BLOCK
    f[FEAT_POS_LO] = kpos % SEL_BLOCK
    f[FEAT_ONE] = 1.0
    n_sel = seq // SEL_BLOCK
    f[FEAT_SEL0:FEAT_SEL0 + n_sel] = kpos[None, :] // SEL_BLOCK == np.arange(n_sel)[:, None]
    return f


def _rank_select_t(score_t, n_cand, topn):
    idx = lax.broadcasted_iota(jnp.int32, score_t.shape, 0)
    rank = jnp.zeros(score_t.shape, jnp.int32)
    for j in range(n_cand):
        row = score_t[j:j + 1, :]
        rank = rank + jnp.where(idx > j, jnp.where(row >= score_t, 1, 0), jnp.where(row > score_t, 1, 0))
    return jnp.where(rank < topn, 1.0, 0.0)


def _pattn_kernel(q_ref, kcf_ref, vc2_ref, kst_ref, vst_ref, kwt_ref, vwt_ref, feat_ref, sm_ref, ovt_ref,
                  o_ref, kfs_scr, kfw_scr, vs_scr, vw_scr, m_scr, acc_scr, *, n_sel, topn):
    qi = pl.program_id(1)
    t0 = qi * Q_BLOCK
    half = KV_WIDTH // NSA_KV_HEADS

    @pl.when(qi == 0)
    def _():
        kfs_scr[0:KV_WIDTH, :] = kst_ref[...]
        kfs_scr[KV_WIDTH:, :] = feat_ref[...]
        kfw_scr[0:KV_WIDTH, :] = kwt_ref[...]
        kfw_scr[KV_WIDTH:, :] = feat_ref[...]
        ones = jnp.ones((half, kst_ref.shape[1]), BF16)
        for src, dst in ((vst_ref, vs_scr), (vwt_ref, vw_scr)):
            dst[0, 0:half, :] = src[0:half, :]
            dst[0, half:, :] = ones
            dst[1, 0:half, :] = ones
            dst[1, half:, :] = src[half:, :]

    q = q_ref[...]
    qs = jnp.concatenate([q[:, h * KV_WIDTH:(h + 1) * KV_WIDTH] for h in range(NSA_HEADS)], axis=0)
    gates = _sigmoid(sm_ref[...])

    def gate_col(branch, h):
        j = GATE_LANE0 + branch * NSA_HEADS + h
        return gates[:, j:j + 1]

    lane = lax.broadcasted_iota(jnp.int32, (Q_BLOCK, LANES), 1)
    t0f = t0.astype(F32)
    sel_lanes = []

    def feat_coeffs(h, selected):
        sl = _slope(h)
        c = jnp.where(lane == FEAT_POS_HI, sl * SEL_BLOCK,
                      jnp.where(lane == FEAT_POS_LO, sl, jnp.where(lane == FEAT_ONE, -sl * t0f, 0.0)))
        if selected:
            c = jnp.where(lane >= FEAT_SEL0, (sel_lanes[h // NSA_GROUP] - 1.0) * MASK_BIG, c)
        return c.astype(BF16)

    def q_operand(selected):
        feats = jnp.concatenate([feat_coeffs(h, selected) for h in range(NSA_HEADS)], axis=0)
        return jnp.concatenate([qs, feats], axis=1)

    lane_o = lax.broadcasted_iota(jnp.int32, (Q_BLOCK, KV_WIDTH), 1)

    def split_sums(h, a):
        own = (lane_o < half) if h < NSA_GROUP else (lane_o >= half)
        total = jnp.where(own, pltpu.roll(a, half, 1), a)
        return jnp.where(own, a / jnp.maximum(total, TINY), 0.0), total

    qa_plain = q_operand(False)
    n_c = kcf_ref.shape[1]
    s_all = _dot(qa_plain, kcf_ref[...])
    tq = lax.broadcasted_iota(jnp.int32, (Q_BLOCK, n_c), 0) + t0
    cend = lax.broadcasted_iota(jnp.int32, (Q_BLOCK, n_c), 1) * CMP_STRIDE + (CMP_BLOCK - 1)
    vis_c = tq >= cend
    psum = [None] * NSA_KV_HEADS
    for h in range(NSA_HEADS):
        g = h // NSA_GROUP
        s = jnp.where(vis_c, s_all[h * Q_BLOCK:(h + 1) * Q_BLOCK], NEG_INF)
        m = jnp.max(s, axis=-1, keepdims=True)
        p = jnp.where(vis_c, jnp.exp(s - m), 0.0)
        o_cmp, total = split_sums(h, _dot(p.astype(BF16), vc2_ref[g]))
        p = p / jnp.concatenate([jnp.maximum(total, TINY)] * (n_c // LANES), axis=1)
        psum[g] = p if psum[g] is None else psum[g] + p
        o_ref[:, h * KV_WIDTH:(h + 1) * KV_WIDTH] = gate_col(0, h) * o_cmp

    s_idx = lax.broadcasted_iota(jnp.int32, (n_sel, Q_BLOCK), 0)
    tq_s = lax.broadcasted_iota(jnp.int32, (n_sel, Q_BLOCK), 1) + t0
    valid = s_idx * SEL_BLOCK <= tq_s
    forced = (s_idx == 0) | (s_idx == tq_s // SEL_BLOCK)
    pad_lo = jnp.zeros((FEAT_SEL0, Q_BLOCK), F32)
    pad_hi = jnp.zeros((LANES - FEAT_SEL0 - n_sel, Q_BLOCK), F32)
    for g in range(NSA_KV_HEADS):
        hi = psum[g].astype(BF16)
        lo = (psum[g] - hi.astype(F32)).astype(BF16)
        imp_t = _dot_nt(ovt_ref[...], hi) + _dot_nt(ovt_ref[...], lo)
        score_t = jnp.where(valid, imp_t + jnp.where(forced, FORCE_BONUS, 0.0), NEG_INF)
        sel_t = _rank_select_t(score_t, n_sel, topn)
        parts = [pad_lo, sel_t] + ([pad_hi] if pad_hi.shape[0] else [])
        sel_lanes.append(jnp.concatenate(parts, axis=0).T)

    row_i = lax.broadcasted_iota(jnp.int32, (Q_BLOCK, KEY_BLOCK), 0)
    lane_i = lax.broadcasted_iota(jnp.int32, (Q_BLOCK, KEY_BLOCK), 1)

    def reset():
        m_scr[...] = jnp.full(m_scr.shape, NEG_INF, F32)
        acc_scr[...] = jnp.zeros(acc_scr.shape, F32)

    grp_rows = NSA_GROUP * Q_BLOCK

    def flash_step(kb, qa, kf_ref, v_scr, masked):
        k0 = pl.multiple_of(kb * KEY_BLOCK, KEY_BLOCK)
        s_all = _dot(qa, kf_ref[:, pl.ds(k0, KEY_BLOCK)])
        if masked:
            dist = (t0 - k0) + (row_i - lane_i)
            keep = jnp.where(dist >= 0, dist, WINDOW) < WINDOW
        ps = []
        alphas = []
        for h in range(NSA_HEADS):
            rows = slice(h * Q_BLOCK, (h + 1) * Q_BLOCK)
            s = s_all[rows]
            if masked:
                s = jnp.where(keep, s, NEG_INF)
            m_prev = m_scr[rows]
            m_new = jnp.maximum(m_prev, jnp.max(s, axis=-1, keepdims=True))
            p = jnp.exp(s - jnp.concatenate([m_new] * (KEY_BLOCK // LANES), axis=1))
            if masked:
                p = jnp.where(keep, p, 0.0)
            alpha = jnp.exp(m_prev - m_new)
            m_scr[rows] = m_new
            ps.append(p.astype(BF16))
            alphas.append(alpha)
        p_all = jnp.concatenate(ps, axis=0)
        pv = [_dot_nt(p_all[g * grp_rows:(g + 1) * grp_rows], v_scr[g, :, pl.ds(k0, KEY_BLOCK)])
              for g in range(NSA_KV_HEADS)]
        acc_scr[...] = jnp.concatenate(alphas, axis=0) * acc_scr[...] + jnp.concatenate(pv, axis=0)

    def finish(branch):
        for h in range(NSA_HEADS):
            rows = slice(h * Q_BLOCK, (h + 1) * Q_BLOCK)
            cols = slice(h * KV_WIDTH, (h + 1) * KV_WIDTH)
            o_b, _ = split_sums(h, acc_scr[rows])
            o_ref[:, cols] = o_ref[:, cols] + gate_col(branch, h) * o_b

    def loop(lo, hi, qa, kf_ref, v_scr):
        n = jnp.maximum(hi - lo, 0)

        def pair(i, c):
            flash_step(lo + 2 * i, qa, kf_ref, v_scr, False)
            flash_step(lo + 2 * i + 1, qa, kf_ref, v_scr, False)
            return c

        lax.fori_loop(0, n // 2, pair, 0)

        @pl.when(n % 2 == 1)
        def _():
            flash_step(hi - 1, qa, kf_ref, v_scr, False)

    per_key_block = KEY_BLOCK // Q_BLOCK
    kb_diag = qi // per_key_block
    reset()
    qa = q_operand(True)
    loop(0, kb_diag, qa, kfs_scr, vs_scr)
    flash_step(kb_diag, qa, kfs_scr, vs_scr, True)
    finish(1)
    reset()
    qa = qa_plain
    kb_lo = jnp.maximum(qi - WINDOW // Q_BLOCK, 0) // per_key_block

    @pl.when(kb_lo < kb_diag)
    def _():
        flash_step(kb_lo, qa, kfw_scr, vw_scr, True)

    loop(kb_lo + 1, kb_diag, qa, kfw_scr, vw_scr)
    flash_step(kb_diag, qa, kfw_scr, vw_scr, True)
    finish(2)


def prompt_attention(qp, kcf, vc2, kvt_b, r, batch, seq):
    assert seq % KEY_BLOCK == 0 and seq // SEL_BLOCK <= LANES - FEAT_SEL0
    nqb = seq // Q_BLOCK
    n_c = kcf.shape[2]
    assert n_c % LANES == 0
    n_sel = seq // SEL_BLOCK
    topn = min(SEL_TOPN, n_sel)
    cs = np.arange(n_c)[:, None] * CMP_STRIDE
    ss = np.arange(n_sel)[None, :] * SEL_BLOCK
    ov = np.clip(np.minimum(cs + CMP_BLOCK, ss + SEL_BLOCK) - np.maximum(cs, ss), 0, None) / CMP_BLOCK
    ov[(seq - CMP_BLOCK) // CMP_STRIDE + 1:] = 0.0
    kv_spec = lambda idx: pl.BlockSpec((None, None, KV_WIDTH, seq), lambda b, i: (idx, b, 0, 0))
    return pl.pallas_call(
        functools.partial(_pattn_kernel, n_sel=n_sel, topn=topn),
        grid=(batch, nqb),
        in_specs=[
            pl.BlockSpec((Q_BLOCK, PADDED_Q), lambda b, i: (b * nqb + i, 0)),
            pl.BlockSpec((None, 2 * KV_WIDTH, n_c), lambda b, i: (b, 0, 0)),
            pl.BlockSpec((None, NSA_KV_HEADS, n_c, KV_WIDTH), lambda b, i: (b, 0, 0, 0)),
            kv_spec(2), kv_spec(3), kv_spec(4), kv_spec(5),
            pl.BlockSpec((LANES, seq), lambda b, i: (0, 0)),
            pl.BlockSpec((Q_BLOCK, LANES), lambda b, i: (b * nqb + i, R_SMALL // LANES)),
            pl.BlockSpec((n_sel, n_c), lambda b, i: (0, 0)),
        ],
        out_specs=pl.BlockSpec((Q_BLOCK, PADDED_Q), lambda b, i: (b * nqb + i, 0)),
        out_shape=jax.ShapeDtypeStruct((batch * seq, PADDED_Q), F32),
        scratch_shapes=[
            pltpu.VMEM((2 * KV_WIDTH, seq), BF16),
            pltpu.VMEM((2 * KV_WIDTH, seq), BF16),
            pltpu.VMEM((NSA_KV_HEADS, KV_WIDTH, seq), BF16),
            pltpu.VMEM((NSA_KV_HEADS, KV_WIDTH, seq), BF16),
            pltpu.VMEM((NSA_HEADS * Q_BLOCK, LANES), F32),
            pltpu.VMEM((NSA_HEADS * Q_BLOCK, KV_WIDTH), F32),
        ],
        compiler_params=_cparams("parallel", "arbitrary"),
        name="prompt_nsa",
    )(qp, kcf, vc2, kvt_b, kvt_b, kvt_b, kvt_b, jnp.asarray(_feature_rows(seq), BF16), r,
      jnp.asarray(ov.T, BF16))


def _gla_log_decay(sm, aup_ref, ab_ref):
    z = _dot(sm.astype(BF16), aup_ref[...]) + ab_ref[...]
    return _log_sigmoid(z) / GLA_GATE_TAU


def _gla_head_norm(o, gg_ref):
    on = o * lax.rsqrt(jnp.mean(o * o, axis=-1, keepdims=True) + NORM_EPS)
    return on * gg_ref[...]


def _pgla_kernel(qg_ref, kg_ref, vg_ref, sm_ref, aup_ref, ab_ref, gg_ref, og_ref, sfin_ref, st_scr, *, n_tiles):
    ti = pl.program_id(1)
    tt = qg_ref.shape[0]
    width = qg_ref.shape[1]

    @pl.when(ti == 0)
    def _():
        st_scr[...] = jnp.zeros(st_scr.shape, F32)

    la = _gla_log_decay(sm_ref[...], aup_ref, ab_ref)
    rin = lax.broadcasted_iota(jnp.int32, la.shape, 0)
    bg = la
    sh = 1
    while sh < tt:
        bg = bg + jnp.where(rin >= sh, pltpu.roll(bg, sh, 0), 0.0)
        sh *= 2

    def row_of_block(x, block, row):
        x3 = x.reshape(tt // block, block, width)
        return jnp.broadcast_to(x3[:, row:row + 1, :], x3.shape).reshape(tt, width)

    n_chunks = tt // GLA_CHUNK
    ends = bg.reshape(n_chunks, GLA_CHUNK, width)[:, GLA_CHUNK - 1:GLA_CHUNK, :]
    prev_end = jnp.concatenate([jnp.zeros((1, 1, width), F32), ends[:n_chunks - 1]], axis=0)
    b = (bg.reshape(n_chunks, GLA_CHUNK, width) - prev_end).reshape(tt, width)
    bg_end = bg[tt - 1:tt, :]
    scales = [(jnp.exp(b), jnp.exp(-b))]
    block = 2 * GLA_CHUNK
    while block <= tt:
        mid = row_of_block(bg, block, block // 2 - 1)
        scales.append((jnp.exp(jnp.minimum(bg - mid, 0.0)), jnp.exp(jnp.minimum(mid - bg, 0.0))))
        block *= 2
    e_in = jnp.exp(bg)
    e_out = jnp.exp(bg_end - bg)

    r_i = lax.broadcasted_iota(jnp.int32, (tt, tt), 0)
    c_i = lax.broadcasted_iota(jnp.int32, (tt, tt), 1)
    level = jnp.full((tt, tt), len(scales) - 1, jnp.int32)
    block = tt // 2
    lv = len(scales) - 2
    while block >= GLA_CHUNK:
        level = jnp.where(r_i // block == c_i // block, lv, level)
        block //= 2
        lv -= 1
    level = jnp.where(c_i <= r_i, level, -1)

    for h in range(GLA_HEADS):
        ks = slice(h * GLA_DK, (h + 1) * GLA_DK)
        vs = slice(h * GLA_DV, (h + 1) * GLA_DV)
        q = qg_ref[:, ks] * (GLA_DK ** -0.5)
        k = kg_ref[:, ks]
        v = vg_ref[:, vs].astype(BF16)
        a = jnp.zeros((tt, tt), F32)
        for lv, (sq, sk) in enumerate(scales):
            a_lv = _dot_nt((q * sq[:, ks]).astype(BF16), (k * sk[:, ks]).astype(BF16))
            a = jnp.where(level == lv, a_lv, a)
        st = st_scr[h]
        o = _dot(a.astype(BF16), v) + _dot_nt((q * e_in[:, ks]).astype(BF16), st.astype(BF16))
        kv_t = lax.dot_general(v, (k * e_out[:, ks]).astype(BF16), (((0,), (0,)), ((), ())),
                               preferred_element_type=F32)
        st_scr[h] = st * jnp.exp(bg_end[:, ks]) + kv_t
        og_ref[:, vs] = _gla_head_norm(o, gg_ref)

    @pl.when(ti == n_tiles - 1)
    def _():
        for h in range(GLA_HEADS):
            sfin_ref[h] = st_scr[h].T


def prompt_gla(r, aup, ab, gg, batch, seq):
    tt = _tile(seq, 256, GLA_CHUNK)
    nt = seq // tt
    row = lambda b, i: b * nt + i
    return pl.pallas_call(
        functools.partial(_pgla_kernel, n_tiles=nt),
        grid=(batch, nt),
        in_specs=[
            pl.BlockSpec((tt, GLA_KEY_WIDTH), lambda b, i: (row(b, i), R_QG // GLA_KEY_WIDTH)),
            pl.BlockSpec((tt, GLA_KEY_WIDTH), lambda b, i: (row(b, i), R_KG // GLA_KEY_WIDTH)),
            pl.BlockSpec((tt, GLA_VAL_WIDTH), lambda b, i: (row(b, i), R_VG // GLA_VAL_WIDTH)),
            pl.BlockSpec((tt, LANES), lambda b, i: (row(b, i), R_SMALL // LANES)),
            pl.BlockSpec(aup.shape, lambda b, i: (0, 0)),
            pl.BlockSpec(ab.shape, lambda b, i: (0, 0)),
            pl.BlockSpec(gg.shape, lambda b, i: (0, 0)),
        ],
        out_specs=[
            pl.BlockSpec((tt, GLA_VAL_WIDTH), lambda b, i: (row(b, i), 0)),
            pl.BlockSpec((None, GLA_HEADS, GLA_DK, GLA_DV), lambda b, i: (b, 0, 0, 0)),
        ],
        out_shape=[
            jax.ShapeDtypeStruct((batch * seq, GLA_VAL_WIDTH), F32),
            jax.ShapeDtypeStruct((batch, GLA_HEADS, GLA_DK, GLA_DV), F32),
        ],
        scratch_shapes=[pltpu.VMEM((GLA_HEADS, GLA_DV, GLA_DK), F32)],
        compiler_params=_cparams("parallel", "arbitrary"),
        name="prompt_gla",
    )(r, r, r, r, aup, ab, gg)


def _sgla_kernel(qg_ref, kg_ref, vg_ref, sm_ref, aup_ref, ab_ref, gg_ref, s_ref, og_ref, snew_ref, *, ds):
    rows = qg_ref.shape[0]
    nb = rows // ds
    la = _gla_log_decay(sm_ref[...], aup_ref, ab_ref)
    ri = lax.broadcasted_iota(jnp.int32, la.shape, 0) % ds
    b = la
    sh = 1
    while sh < ds:
        b = b + jnp.where(ri >= sh, pltpu.roll(b, sh, 0), 0.0)
        sh *= 2
    b_last = b
    for d in range(1, ds):
        b_last = jnp.where(ri == ds - 1 - d, pltpu.roll(b, rows - d, 0), b_last)
    e_b = jnp.exp(b)
    e_nb = jnp.exp(-b)
    e_tail = jnp.exp(b_last - b)
    e_last = jnp.exp(b_last)

    r_i = lax.broadcasted_iota(jnp.int32, (rows, rows), 0)
    c_i = lax.broadcasted_iota(jnp.int32, (rows, rows), 1)
    causal = (r_i // ds == c_i // ds) & (c_i <= r_i)
    row_b = lax.broadcasted_iota(jnp.int32, (rows, GLA_DV), 0) // ds

    for h in range(GLA_HEADS):
        ks = slice(h * GLA_DK, (h + 1) * GLA_DK)
        vs = slice(h * GLA_DV, (h + 1) * GLA_DV)
        q = qg_ref[:, ks] * (GLA_DK ** -0.5)
        k = kg_ref[:, ks]
        v = vg_ref[:, vs]
        q_dec = (q * e_b[:, ks]).astype(BF16)
        k_inv = (k * e_nb[:, ks]).astype(BF16)
        k_tail_t = (k * e_tail[:, ks]).T
        e_last_t = e_last[:, ks].T
        a = jnp.where(causal, _dot_nt(q_dec, k_inv), 0.0)
        o = _dot(a.astype(BF16), v.astype(BF16))
        for bb in range(nb):
            s_prev = s_ref[bb, h]
            o_inter = _dot(q_dec, s_prev.astype(BF16))
            o = o + jnp.where(row_b == bb, o_inter, 0.0)
            s_new = e_last_t[:, bb * ds:bb * ds + 1] * s_prev
            for j in range(ds):
                rr = bb * ds + j
                s_new = s_new + k_tail_t[:, rr:rr + 1] * v[rr:rr + 1, :]
            snew_ref[bb, h] = s_new
        og_ref[:, vs] = _gla_head_norm(o, gg_ref)


def sample_gla(r, aup, ab, gg, state, layer, dec_batch, ds):
    nb = _tile(dec_batch, 8, 1)
    rows = nb * ds
    return pl.pallas_call(
        functools.partial(_sgla_kernel, ds=ds),
        grid=(dec_batch // nb,),
        in_specs=[
            pl.BlockSpec((rows, GLA_KEY_WIDTH), lambda i: (i, R_QG // GLA_KEY_WIDTH)),
            pl.BlockSpec((rows, GLA_KEY_WIDTH), lambda i: (i, R_KG // GLA_KEY_WIDTH)),
            pl.BlockSpec((rows, GLA_VAL_WIDTH), lambda i: (i, R_VG // GLA_VAL_WIDTH)),
            pl.BlockSpec((rows, LANES), lambda i: (i, R_SMALL // LANES)),
            pl.BlockSpec(aup.shape, lambda i: (0, 0)),
            pl.BlockSpec(ab.shape, lambda i: (0, 0)),
            pl.BlockSpec(gg.shape, lambda i: (0, 0)),
            pl.BlockSpec((None, nb, GLA_HEADS, GLA_DK, GLA_DV), lambda i: (layer, i, 0, 0, 0)),
        ],
        out_specs=[
            pl.BlockSpec((rows, GLA_VAL_WIDTH), lambda i: (i, 0)),
            pl.BlockSpec((nb, GLA_HEADS, GLA_DK, GLA_DV), lambda i: (i, 0, 0, 0)),
        ],
        out_shape=[
            jax.ShapeDtypeStruct((dec_batch * ds, GLA_VAL_WIDTH), F32),
            jax.ShapeDtypeStruct((dec_batch, GLA_HEADS, GLA_DK, GLA_DV), F32),
        ],
        compiler_params=_cparams("parallel"),
        name="sample_gla",
    )(r, r, r, r, aup, ab, gg, state)


def _mix_kernel(x_ref, on_ref, za_ref, og_ref, zb_ref, mg_ref, wba_ref, wbb_ref, wo_ref, fg_ref, *o_refs):
    pa = _dot((on_ref[...] * _silu(za_ref[...])).astype(BF16), wba_ref[...])
    pb = _dot((og_ref[...] * _silu(zb_ref[...])).astype(BF16), wbb_ref[...])
    mix = _sigmoid(mg_ref[:, :D_MODEL]) * pa + _sigmoid(mg_ref[:, D_MODEL:]) * pb
    y = x_ref[...] + _dot(mix.astype(BF16), wo_ref[...])
    o_refs[0][...] = y
    if len(o_refs) > 1:
        yn = y * lax.rsqrt(jnp.mean(y * y, axis=-1, keepdims=True) + NORM_EPS)
        o_refs[1][...] = yn * fg_ref[...]


def mixer_output(x, o_nsa, r, o_gla, wba, wbb, wo, final_g, with_final_norm):
    n = x.shape[0]
    tm = _tile(n, 256)
    row = lambda w: pl.BlockSpec((tm, w), lambda i: (i, 0))
    wspec = lambda w: pl.BlockSpec(w.shape, lambda i: (0, 0))
    n_out = 2 if with_final_norm else 1
    return pl.pallas_call(
        _mix_kernel,
        grid=(n // tm,),
        in_specs=[
            row(D_MODEL), row(PADDED_Q),
            pl.BlockSpec((tm, PADDED_Q), lambda i: (i, R_ZA // PADDED_Q)),
            row(GLA_VAL_WIDTH),
            pl.BlockSpec((tm, GLA_VAL_WIDTH), lambda i: (i, R_ZB // GLA_VAL_WIDTH)),
            pl.BlockSpec((tm, 2 * D_MODEL), lambda i: (i, R_MG // (2 * D_MODEL))),
            wspec(wba), wspec(wbb), wspec(wo), wspec(final_g),
        ],
        out_specs=[row(D_MODEL)] * n_out,
        out_shape=[jax.ShapeDtypeStruct((n, D_MODEL), F32)] * n_out,
        compiler_params=_cparams("parallel"),
        name="mixer_output",
    )(x, o_nsa, r, o_gla, r, r, wba, wbb, wo, final_g)


def _select_blocks(score, lane_idx, n_cand, topn):
    rank = jnp.zeros(score.shape, jnp.int32)
    for j in range(n_cand):
        col = score[:, j:j + 1]
        rank = rank + jnp.where(col > score, 1, 0) + jnp.where(col == score, jnp.where(lane_idx > j, 1, 0), 0)
    return jnp.where(rank < topn, 1.0, 0.0)


def _softmax_two(parts, keeps):
    m = functools.reduce(jnp.maximum, [jnp.max(s, axis=-1, keepdims=True) for s in parts])
    ps = [jnp.where(kp, jnp.exp(s - m), 0.0) for s, kp in zip(parts, keeps)]
    total = functools.reduce(lambda a, c: a + c, [jnp.sum(p, axis=-1, keepdims=True) for p in ps])
    return ps, jnp.maximum(total, TINY)


def _smain_kernel(pt_ref, q_ref, gate_ref, slope_ref, ab_hbm, kt_hbm, vt_hbm, kw_ref, vw_ref, new_ref,
                  w2k_ref, w2v_ref, ov_ref, exp_ref, exn_ref, o_ref, kw_out_ref, vw_out_ref,
                  ab_buf, kt_buf, vt_buf, sems, *, layer, n_pages, past, ds, n_sel, topn):
    b = pl.program_id(0)
    slot = b % 2
    rows = q_ref.shape[0]
    grp_rows = NSA_KV_HEADS * ds
    n_new = new_ref.shape[2]
    b_loc = b % (n_new // ds)

    def page_copies(bb, sl):
        cps = []
        for j in range(n_pages):
            pg = pt_ref[bb, j]
            cps.append(pltpu.make_async_copy(ab_hbm.at[pg], ab_buf.at[sl, j], sems.at[sl, 0]))
            cps.append(pltpu.make_async_copy(kt_hbm.at[layer, pg], kt_buf.at[sl, j], sems.at[sl, 1]))
            cps.append(pltpu.make_async_copy(vt_hbm.at[layer, pg], vt_buf.at[sl, j], sems.at[sl, 2]))
        return cps

    @pl.when(b == 0)
    def _():
        for cp in page_copies(0, 0):
            cp.start()

    @pl.when(b + 1 < pl.num_programs(0))
    def _():
        for cp in page_copies(b + 1, 1 - slot):
            cp.start()

    for cp in page_copies(b, slot):
        cp.wait()
    kt_refs = [kt_buf.at[slot, j] for j in range(n_pages)]
    vt_refs = [vt_buf.at[slot, j] for j in range(n_pages)]

    qs = q_ref[...]
    slope = slope_ref[:, 0:1]
    ab = jnp.concatenate([ab_buf[slot, j] for j in range(n_pages)], axis=0)
    hid = ab.shape[1] // 4
    kc = _summaries(ab[:, 0:hid], _shift_up(ab[:, hid:2 * hid]), w2k_ref[...]).astype(BF16)
    vc = _summaries(ab[:, 2 * hid:3 * hid], _shift_up(ab[:, 3 * hid:]), w2v_ref[...]).astype(BF16)
    n_c = kc.shape[0]

    def irow(shape):
        return lax.broadcasted_iota(jnp.int32, shape, 0) % ds

    dist_c = (past + irow((rows, n_c))) - (lax.broadcasted_iota(jnp.int32, (rows, n_c), 1) * CMP_STRIDE
                                          + (CMP_BLOCK - 1))
    vis_c = dist_c >= 0
    s_c = jnp.where(vis_c, _dot_nt(qs, kc) - slope * dist_c.astype(F32), NEG_INF)
    m_c = jnp.max(s_c, axis=-1, keepdims=True)
    p_c = jnp.where(vis_c, jnp.exp(s_c - m_c), 0.0)
    p_c = p_c / jnp.maximum(jnp.sum(p_c, axis=-1, keepdims=True), TINY)
    o_cmp = _dot(p_c.astype(BF16), vc)

    hi = p_c.astype(BF16)
    lo = (p_c - hi.astype(F32)).astype(BF16)
    imp_h = _dot(hi, ov_ref[...]) + _dot(lo, ov_ref[...])
    imp = functools.reduce(lambda a, c: a + c,
                           [imp_h[r * grp_rows:(r + 1) * grp_rows] for r in range(NSA_GROUP)])
    s_idx = lax.broadcasted_iota(jnp.int32, imp.shape, 1)
    qpos = past + irow(imp.shape)
    valid = (s_idx < n_sel) & (s_idx * SEL_BLOCK <= qpos)
    forced = (s_idx == 0) | (s_idx == qpos // SEL_BLOCK)
    score = jnp.where(valid, imp + jnp.where(forced, FORCE_BONUS, 0.0), NEG_INF)
    sel = _select_blocks(score, s_idx, n_sel, topn)
    sel = jnp.concatenate([sel] * NSA_GROUP, axis=0).astype(BF16)
    keep_past = _dot(sel, exp_ref[...]) > 0.5
    sel_new = _dot(sel, exn_ref[...]) > 0.5

    slot = lax.broadcasted_iota(jnp.int32, (rows, n_new), 1)
    dist_n = irow((rows, n_new)) - slot % ds
    mine = (slot // ds == b_loc) & (dist_n >= 0)
    bias_n = slope * dist_n.astype(F32)

    s_p = jnp.concatenate([_dot(qs, r[...].astype(BF16)) for r in kt_refs], axis=1)
    dist_p = (past + irow(s_p.shape)) - lax.broadcasted_iota(jnp.int32, s_p.shape, 1)
    s_p = jnp.where(keep_past, s_p - slope * dist_p.astype(F32), NEG_INF)
    keep_n = mine & sel_new
    s_n = jnp.where(keep_n, _dot(qs, new_ref[2].astype(BF16)) - bias_n, NEG_INF)
    (p_p, p_n), total = _softmax_two([s_p, s_n], [keep_past, keep_n])
    acc = _dot_nt(p_n.astype(BF16), new_ref[3].astype(BF16))
    for j, r in enumerate(vt_refs):
        acc = acc + _dot_nt(p_p[:, j * LANES:(j + 1) * LANES].astype(BF16), r[...].astype(BF16))
    o_slc = acc / total

    kbuf = kw_ref[...]
    vbuf = vw_ref[...]
    w_buf = kbuf.shape[1]
    dist_w = (w_buf + irow((rows, w_buf))) - lax.broadcasted_iota(jnp.int32, (rows, w_buf), 1)
    keep_w = dist_w < WINDOW
    s_w = jnp.where(keep_w, _dot(qs, kbuf.astype(BF16)) - slope * dist_w.astype(F32), NEG_INF)
    s_wn = jnp.where(mine, _dot(qs, new_ref[4].astype(BF16)) - bias_n, NEG_INF)
    (p_w, p_wn), total_w = _softmax_two([s_w, s_wn], [keep_w, mine])
    o_win = (_dot_nt(p_w.astype(BF16), vbuf.astype(BF16)) + _dot_nt(p_wn.astype(BF16), new_ref[5].astype(BF16))) / total_w

    gates = _sigmoid(gate_ref[...])
    o_ref[...] = gates[:, 0:1] * o_cmp + gates[:, 1:2] * o_slc + gates[:, 2:3] * o_win

    lane_new = lax.broadcasted_iota(jnp.int32, (KV_WIDTH, n_new), 1)
    shift = (n_new - ds) - b_loc * ds
    for buf, slab, out_ref in ((kbuf, 4, kw_out_ref), (vbuf, 5, vw_out_ref)):
        rolled = pltpu.roll(buf, w_buf - ds, 1)
        own_last = pltpu.roll(new_ref[slab], shift, 1)
        out_ref[:, 0:w_buf - n_new] = rolled[:, 0:w_buf - n_new]
        out_ref[:, w_buf - n_new:w_buf] = jnp.where(lane_new >= n_new - ds, own_last, rolled[:, w_buf - n_new:w_buf])


def sample_attention(page_table, q_rows, gate_rows, ab_pages, slc_pools_t, win_bufs_t, new_t, w2k, w2v, layer, past, ds):
    assert past % SEL_BLOCK == 0 and past % LANES == 0
    dec_batch, n_pages = page_table.shape
    rows = q_rows.shape[1]
    w_buf = win_bufs_t[0].shape[3]
    n_new = min(LANES, dec_batch * ds)
    assert (dec_batch * ds) % n_new == 0 and n_new % ds == 0 and w_buf >= n_new
    n_c = n_pages * (LANES // CMP_STRIDE)
    n_cmp = (past + ds - CMP_BLOCK) // CMP_STRIDE + 1
    n_sel = -(-(past + ds) // SEL_BLOCK)
    topn = min(SEL_TOPN, n_sel)
    assert n_sel <= LANES and ds <= CMP_STRIDE
    cs = np.arange(n_c)[:, None] * CMP_STRIDE
    ss = np.arange(LANES)[None, :] * SEL_BLOCK
    ov = np.clip(np.minimum(cs + CMP_BLOCK, ss + SEL_BLOCK) - np.maximum(cs, ss), 0, None) / CMP_BLOCK
    ov[n_cmp:] = 0.0
    ov[:, n_sel:] = 0.0
    ex_past = (np.arange(past)[None, :] // SEL_BLOCK == np.arange(LANES)[:, None]).astype(np.float32)
    ex_new = ((past + np.arange(n_new) % ds)[None, :] // SEL_BLOCK == np.arange(LANES)[:, None]).astype(np.float32)
    head = np.arange(rows) // (NSA_KV_HEADS * ds) + NSA_GROUP * ((np.arange(rows) // ds) % NSA_KV_HEADS)
    slope = np.broadcast_to((2.0 ** -(head + 1.0))[:, None], (rows, LANES)).astype(np.float32)

    const = lambda shape: pl.BlockSpec(shape, lambda b, pt: (0,) * len(shape))
    in_specs = [pl.BlockSpec((None, rows, LANES), lambda b, pt: (b, 0, 0)),
                pl.BlockSpec((None, rows, LANES), lambda b, pt: (b, 0, 0)),
                const((rows, LANES))]
    operands = [q_rows, gate_rows, jnp.asarray(slope)]
    in_specs += [pl.BlockSpec(memory_space=pl.ANY)] * 3
    operands += [ab_pages, slc_pools_t[0], slc_pools_t[1]]
    for arr in win_bufs_t:
        in_specs.append(pl.BlockSpec((None, None, KV_WIDTH, w_buf), lambda b, pt: (layer, b, 0, 0)))
        operands.append(arr)
    per_tile = n_new // ds
    in_specs.append(pl.BlockSpec((6, None, KV_WIDTH, n_new), lambda b, pt: (0, 0, 0, b // per_tile)))
    operands.append(new_t)
    consts = [w2k, w2v, jnp.asarray(ov, BF16), jnp.asarray(ex_past, BF16), jnp.asarray(ex_new, BF16)]
    in_specs += [const(c.shape) for c in consts]
    operands += consts
    buf_out = pl.BlockSpec((None, KV_WIDTH, w_buf), lambda b, pt: (b, 0, 0))
    return pl.pallas_call(
        functools.partial(_smain_kernel, layer=layer, n_pages=n_pages, past=past, ds=ds, n_sel=n_sel, topn=topn),
        grid_spec=pltpu.PrefetchScalarGridSpec(
            num_scalar_prefetch=1,
            grid=(dec_batch,),
            in_specs=in_specs,
            out_specs=[pl.BlockSpec((None, rows, LANES), lambda b, pt: (b, 0, 0)), buf_out, buf_out],
            scratch_shapes=[pltpu.VMEM((2, n_pages) + ab_pages.shape[1:], F32),
                            pltpu.VMEM((2, n_pages, KV_WIDTH, LANES), F32),
                            pltpu.VMEM((2, n_pages, KV_WIDTH, LANES), F32),
                            pltpu.SemaphoreType.DMA((2, 3))],
        ),
        out_shape=[jax.ShapeDtypeStruct((dec_batch, rows, LANES), F32),
                   jax.ShapeDtypeStruct((dec_batch, KV_WIDTH, w_buf), F32),
                   jax.ShapeDtypeStruct((dec_batch, KV_WIDTH, w_buf), F32)],
        compiler_params=_cparams("arbitrary"),
        name="sample_nsa",
    )(page_table, *operands)


def _pad_heads_cols(w):
    k = w.shape[0]
    w4 = w.reshape(k, NSA_KV_HEADS, NSA_GROUP, HEAD_DIM)
    eye = jnp.eye(NSA_KV_HEADS, dtype=w.dtype)
    return jnp.einsum("kgrd,gp->kgrpd", w4, eye).reshape(k, PADDED_Q)


def _layer_weights(w_in, pk_pe, pk_w1, pk_w2, pv_pe, pv_w1, pv_w2, a_up, a_b, gla_g, w_ba, w_bb, w_out):
    o = _OFF
    col = lambda i: w_in[:, o[i]:o[i + 1]]
    small = jnp.concatenate([col(2), col(7), jnp.zeros((D_MODEL, LANES - 3 * NSA_HEADS - GLA_GATE_RANK), F32)], axis=1)
    w_q = _pad_heads_cols(col(0)).astype(BF16)
    w_kv_t = col(1).T.astype(BF16)
    w_r = jnp.concatenate([_pad_heads_cols(col(3)), col(4), col(5), col(6), col(8), col(9), small], axis=1).astype(BF16)
    half = CMP_BLOCK // 2

    def cmp_weights(pe, w1, w2):
        eye = jnp.eye(NSA_KV_HEADS, dtype=F32)
        w1h = w1.reshape(2, half, HEAD_DIM, CMP_HIDDEN)
        big = lambda w: jnp.einsum("ldh,gk->lgdkh", w, eye).reshape(half * KV_WIDTH, NSA_KV_HEADS * CMP_HIDDEN)
        peh = pe.reshape(2, half, 1, HEAD_DIM)
        pe_row = lambda p: jnp.broadcast_to(p, (half, NSA_KV_HEADS, HEAD_DIM)).reshape(1, half * KV_WIDTH)
        w2bd = jnp.einsum("hd,gk->ghkd", w2, eye).reshape(NSA_KV_HEADS * CMP_HIDDEN, KV_WIDTH)
        return (pe_row(peh[0]), pe_row(peh[1]), big(w1h[0]).astype(BF16), big(w1h[1]).astype(BF16)), w2bd.astype(BF16)

    cw_k, w2k = cmp_weights(pk_pe, pk_w1, pk_w2)
    cw_v, w2v = cmp_weights(pv_pe, pv_w1, pv_w2)
    aup = jnp.zeros((LANES, GLA_KEY_WIDTH), F32).at[ALOW_LANE0:ALOW_LANE0 + GLA_GATE_RANK].set(a_up).astype(BF16)
    w_ba_p = _pad_heads_cols(w_ba.T).T.astype(BF16)
    return dict(w_q=w_q, w_kv_t=w_kv_t, w_r=w_r, cw_k=cw_k, cw_v=cw_v, w2k=w2k, w2v=w2v, aup=aup,
                ab=a_b.reshape(1, GLA_KEY_WIDTH), gg=gla_g.reshape(1, GLA_DV),
                w_ba=w_ba_p, w_bb=w_bb.astype(BF16), w_out=w_out.astype(BF16))


def _project(x, g_norm, lw, batch, seq):
    h = rms_norm_rows(x, g_norm, BF16)
    qp = matmul_rows(h, lw["w_q"], BF16, tn=512, scale=HEAD_DIM ** -0.5)
    kvt_f, kvt_b = kv_project_t(h, lw["w_kv_t"], batch, seq)
    r = matmul_rows(h, lw["w_r"], F32, tn=R_WIDTH // 7)
    return qp, kvt_f, kvt_b, r


def _tokens_minor(cache):
    lead = cache.shape[:-3]
    n = cache.ndim
    perm = tuple(range(n - 3)) + (n - 2, n - 1, n - 3)
    return cache.transpose(perm).reshape(lead + (KV_WIDTH, cache.shape[-3]))


def _tokens_major(x_t):
    lead = x_t.shape[:-2]
    n = len(lead)
    x5 = x_t.reshape(lead + (NSA_KV_HEADS, HEAD_DIM, x_t.shape[-1]))
    return x5.transpose(tuple(range(n)) + (n + 2, n, n + 1))


def kernel(x_prompt, x_sample, cache_k_cmp, cache_v_cmp, cache_k_slc, cache_v_slc, cache_k_win, cache_v_win, state_gla, page_table, norm_g, w_in, phi_k_pe, phi_k_w1, phi_k_w2, phi_v_pe, phi_v_w1, phi_v_w2, gla_alpha_up, gla_alpha_b, gla_norm_g, w_branch_a, w_branch_b, w_out, final_norm_g):
    batch, seq, _ = x_prompt.shape
    dec_batch, ds, _ = x_sample.shape
    depth = norm_g.shape[0]
    n_pool, page_size = cache_k_cmp.shape[1:3]
    assert page_size == LANES
    n_pages = page_table.shape[1]
    past = n_pages * page_size
    chunks_per_page = page_size // CMP_STRIDE
    final_g = final_norm_g.reshape(1, D_MODEL)

    cmp_pools_t = (_tokens_minor(cache_k_cmp), _tokens_minor(cache_v_cmp))
    slc_pools_t = (_tokens_minor(cache_k_slc), _tokens_minor(cache_v_slc))
    win_bufs_t = (_tokens_minor(cache_k_win), _tokens_minor(cache_v_win))

    y_p = x_prompt.reshape(batch * seq, D_MODEL)
    y_s = x_sample.reshape(dec_batch * ds, D_MODEL)
    outs_p, outs_s = [], []
    for l in range(depth):
        lw = _layer_weights(w_in[l], phi_k_pe[l], phi_k_w1[l], phi_k_w2[l], phi_v_pe[l], phi_v_w1[l], phi_v_w2[l],
                            gla_alpha_up[l], gla_alpha_b[l], gla_norm_g[l], w_branch_a[l], w_branch_b[l], w_out[l])
        last = l == depth - 1

        qp, kvt_f, kvt_b, r = _project(y_p, norm_g[l], lw, batch, seq)
        kcf, vc2 = prompt_summaries(kvt_f, lw["cw_k"], lw["w2k"], lw["cw_v"], lw["w2v"])
        o_nsa = prompt_attention(qp, kcf, vc2, kvt_b, r, batch, seq)
        o_gla, s_gla = prompt_gla(r, lw["aup"], lw["ab"], lw["gg"], batch, seq)
        res = mixer_output(y_p, o_nsa, r, o_gla, lw["w_ba"], lw["w_bb"], lw["w_out"], final_g, last)
        y_p = res[0]
        if last:
            y_p_out = res[1]
        keep = min(WINDOW, seq)
        kv5 = _tokens_major(kvt_f)
        outs_p.append((kv5[0], kv5[1], kv5[2], kv5[3], kv5[4][:, seq - keep:], kv5[5][:, seq - keep:], s_gla))

        qs, kvs_t, _, rs = _project(y_s, norm_g[l], lw, 1, dec_batch * ds)
        ab_pages = pool_halves(cmp_pools_t[0], cmp_pools_t[1], l, lw["cw_k"], lw["cw_v"])
        ab_pages = ab_pages.reshape(n_pool, chunks_per_page, -1)
        q_rows = qs.reshape(dec_batch, ds, NSA_KV_HEADS, NSA_GROUP, KV_WIDTH).transpose(0, 3, 2, 1, 4)
        q_rows = q_rows.reshape(dec_batch, NSA_HEADS * ds, KV_WIDTH)
        gl = rs[:, R_SMALL + GATE_LANE0:R_SMALL + GATE_LANE0 + 3 * NSA_HEADS]
        gl = gl.reshape(dec_batch, ds, 3, NSA_KV_HEADS, NSA_GROUP).transpose(0, 4, 3, 1, 2)
        gate_rows = jnp.pad(gl.reshape(dec_batch, NSA_HEADS * ds, 3), ((0, 0), (0, 0), (0, LANES - 3)))
        o_rows, kw_new, vw_new = sample_attention(page_table, q_rows, gate_rows, ab_pages, slc_pools_t, win_bufs_t,
                                                  kvs_t, lw["w2k"], lw["w2v"], l, past, ds)
        o_nsa_s = o_rows.reshape(dec_batch, NSA_GROUP, NSA_KV_HEADS, ds, KV_WIDTH).transpose(0, 3, 2, 1, 4)
        o_nsa_s = o_nsa_s.reshape(dec_batch * ds, PADDED_Q)
        o_gla_s, s_gla_s = sample_gla(rs, lw["aup"], lw["ab"], lw["gg"], state_gla, l, dec_batch, ds)
        res = mixer_output(y_s, o_nsa_s, rs, o_gla_s, lw["w_ba"], lw["w_bb"], lw["w_out"], final_g, last)
        y_s = res[0]
        if last:
            y_s_out = res[1]
        kvs5 = _tokens_major(kvs_t[:, 0]).reshape(6, dec_batch, ds, NSA_KV_HEADS, HEAD_DIM)
        outs_s.append((kvs5[0], kvs5[1], kvs5[2], kvs5[3], _tokens_major(kw_new), _tokens_major(vw_new), s_gla_s))

    stack = lambda outs: [jnp.stack(t) for t in zip(*outs)]
    return (y_p_out.reshape(batch, seq, D_MODEL), y_s_out.reshape(dec_batch, ds, D_MODEL),
            *stack(outs_p), *stack(outs_s))
```

```python
import functools

import numpy as np
import jax
import jax.numpy as jnp
from jax import lax
from jax.experimental import pallas as pl
from jax.experimental.pallas import tpu as pltpu

F32 = jnp.float32
BF16 = jnp.bfloat16

D_MODEL = 1024
NSA_HEADS = 8
NSA_KV_HEADS = 2
NSA_GROUP = NSA_HEADS // NSA_KV_HEADS
HEAD_DIM = 64
NSA_WIDTH = NSA_HEADS * HEAD_DIM
KV_WIDTH = NSA_KV_HEADS * HEAD_DIM
CMP_BLOCK = 32
CMP_STRIDE = 16
CMP_HIDDEN = 2 * HEAD_DIM
SEL_BLOCK = 64
SEL_TOPN = 16
WINDOW = 512
GLA_HEADS = 4
GLA_KEY_WIDTH = D_MODEL // 2
GLA_VAL_WIDTH = D_MODEL
GLA_DK = GLA_KEY_WIDTH // GLA_HEADS
GLA_DV = GLA_VAL_WIDTH // GLA_HEADS
GLA_GATE_RANK = 16
GLA_GATE_TAU = 16.0
GLA_CHUNK = 32
Q_BLOCK = 128
KEY_BLOCK = 256
NORM_EPS = 1e-6
NEG_INF = -1e30
TINY = 1e-30
FORCE_BONUS = 1e4
MASK_BIG = float(2.0 ** 100)
PADDED_Q = NSA_HEADS * KV_WIDTH
LANES = 128

_IN_SIZES = (NSA_WIDTH, 6 * KV_WIDTH, 3 * NSA_HEADS, NSA_WIDTH, GLA_KEY_WIDTH, GLA_KEY_WIDTH,
             GLA_VAL_WIDTH, GLA_GATE_RANK, GLA_VAL_WIDTH, 2 * D_MODEL)
_OFF = tuple(int(o) for o in np.cumsum((0,) + _IN_SIZES))

R_ZA, R_QG, R_KG, R_VG, R_ZB, R_MG, R_SMALL = 0, 1024, 1536, 2048, 3072, 4096, 6144
R_WIDTH = R_SMALL + LANES
GATE_LANE0 = 0
ALOW_LANE0 = 3 * NSA_HEADS
FEAT_ROWS = KV_WIDTH // NSA_KV_HEADS

VMEM_LIMIT = 48 * 1024 * 1024


def _cparams(*sem):
    return pltpu.CompilerParams(dimension_semantics=sem, vmem_limit_bytes=VMEM_LIMIT)


def _tile(n, target, mult=8):
    if n <= target:
        return n
    t = (target // mult) * mult
    while t >= mult:
        if n % t == 0:
            return t
        t -= mult
    return n


def _sigmoid(x):
    return 1.0 / (1.0 + jnp.exp(-x))


def _silu(x):
    return x * _sigmoid(x)


def _log_sigmoid(x):
    return -(jnp.maximum(-x, 0.0) + jnp.log1p(jnp.exp(-jnp.abs(x))))


def _dot_nt(a, b):
    return lax.dot_general(a, b, (((1,), (1,)), ((), ())), preferred_element_type=F32)


def _dot(a, b):
    return jnp.dot(a, b, preferred_element_type=F32)


def _slope(h):
    return float(2.0 ** (-(h + 1)))


def _norm_kernel(x_ref, g_ref, o_ref):
    xf = x_ref[...]
    xn = xf * lax.rsqrt(jnp.mean(xf * xf, axis=-1, keepdims=True) + NORM_EPS)
    o_ref[...] = (xn * g_ref[...]).astype(o_ref.dtype)


def rms_norm_rows(x, g, out_dtype):
    n, d = x.shape
    tm = _tile(n, 512)
    return pl.pallas_call(
        _norm_kernel,
        grid=(n // tm,),
        in_specs=[pl.BlockSpec((tm, d), lambda i: (i, 0)), pl.BlockSpec((1, d), lambda i: (0, 0))],
        out_specs=pl.BlockSpec((tm, d), lambda i: (i, 0)),
        out_shape=jax.ShapeDtypeStruct((n, d), out_dtype),
        compiler_params=_cparams("parallel"),
        name="rms_norm",
    )(x, g.reshape(1, d))


def _mm_kernel(x_ref, w_ref, o_ref, *, scale):
    acc = _dot(x_ref[...], w_ref[...])
    if scale != 1.0:
        acc = acc * scale
    o_ref[...] = acc.astype(o_ref.dtype)


def matmul_rows(x, w, out_dtype, *, tn, scale=1.0):
    n, k = x.shape
    m = w.shape[1]
    tm = _tile(n, 2048)
    return pl.pallas_call(
        functools.partial(_mm_kernel, scale=scale),
        grid=(n // tm, m // tn),
        in_specs=[pl.BlockSpec((tm, k), lambda i, j: (i, 0)), pl.BlockSpec((k, tn), lambda i, j: (0, j))],
        out_specs=pl.BlockSpec((tm, tn), lambda i, j: (i, j)),
        out_shape=jax.ShapeDtypeStruct((n, m), out_dtype),
        compiler_params=_cparams("parallel", "parallel"),
        name="proj",
    )(x, w)


def _mm_t_kernel(wt_ref, x_ref, of_ref, ob_ref):
    acc = _dot_nt(wt_ref[...], x_ref[...])
    acc = acc.reshape(of_ref.shape)
    of_ref[...] = acc
    ob_ref[...] = acc.astype(ob_ref.dtype)


def kv_project_t(x, wt, batch, seq):
    k = x.shape[1]
    tm = _tile(seq, 512, LANES)
    nt = seq // tm
    n_slab = wt.shape[0] // KV_WIDTH
    out = pl.BlockSpec((n_slab, None, KV_WIDTH, tm), lambda b, i: (0, b, 0, i))
    return pl.pallas_call(
        _mm_t_kernel,
        grid=(batch, nt),
        in_specs=[pl.BlockSpec(wt.shape, lambda b, i: (0, 0)), pl.BlockSpec((tm, k), lambda b, i: (b * nt + i, 0))],
        out_specs=[out, out],
        out_shape=[jax.ShapeDtypeStruct((n_slab, batch, KV_WIDTH, seq), F32),
                   jax.ShapeDtypeStruct((n_slab, batch, KV_WIDTH, seq), BF16)],
        compiler_params=_cparams("parallel", "parallel"),
        name="proj_kv_t",
    )(wt, x)


def _chunk_rows(xs_ref, n_chunks):
    return jnp.concatenate([xs_ref[pl.ds(l, n_chunks, stride=CMP_STRIDE), :] for l in range(CMP_STRIDE)], axis=1)


def _halves(chunks, pea_ref, peb_ref, wa_ref, wb_ref):
    a = _dot((chunks + pea_ref[...]).astype(BF16), wa_ref[...])
    b = _dot((chunks + peb_ref[...]).astype(BF16), wb_ref[...])
    return a, b


def _summaries(a, b_next, w2bd):
    return _dot(_silu(a + b_next).astype(BF16), w2bd)


def _shift_up(x):
    n = x.shape[0]
    return pltpu.roll(x, n - 1, 0)


def _transpose_pages(src, xs_ref, n_pages):
    for p in range(n_pages):
        xs_ref[p * LANES:(p + 1) * LANES, :] = src(p).T


def _pool_ab_kernel(xk_ref, xv_ref, kpea, kpeb, kwa, kwb, vpea, vpeb, vwa, vwb, o_ref, xs_scr):
    n_pages = xk_ref.shape[0]
    hid = kwa.shape[1]
    for i, (x_ref, cw) in enumerate(((xk_ref, (kpea, kpeb, kwa, kwb)), (xv_ref, (vpea, vpeb, vwa, vwb)))):
        _transpose_pages(lambda p: x_ref[p], xs_scr, n_pages)
        a, b = _halves(_chunk_rows(xs_scr, n_pages * (LANES // CMP_STRIDE)), *cw)
        o_ref[:, (2 * i) * hid:(2 * i + 1) * hid] = a
        o_ref[:, (2 * i + 1) * hid:(2 * i + 2) * hid] = b


def pool_halves(pool_k_t, pool_v_t, layer, cw_k, cw_v):
    n_pool = pool_k_t.shape[1]
    pg = _tile(n_pool, 32, 1)
    rows = pg * (LANES // CMP_STRIDE)
    hid = cw_k[2].shape[1]
    full = lambda a: pl.BlockSpec(a.shape, lambda i: (0,) * a.ndim)
    page_spec = pl.BlockSpec((None, pg, KV_WIDTH, LANES), lambda i: (layer, i, 0, 0))
    return pl.pallas_call(
        _pool_ab_kernel,
        grid=(n_pool // pg,),
        in_specs=[page_spec, page_spec] + [full(c) for c in (*cw_k, *cw_v)],
        out_specs=pl.BlockSpec((rows, 4 * hid), lambda i: (i, 0)),
        out_shape=jax.ShapeDtypeStruct((n_pool * (LANES // CMP_STRIDE), 4 * hid), F32),
        scratch_shapes=[pltpu.VMEM((pg * LANES, KV_WIDTH), F32)],
        compiler_params=_cparams("parallel"),
        name="pool_halves",
    )(pool_k_t, pool_v_t, *cw_k, *cw_v)


def _pcmp_kernel(kt_ref, vt_ref, kpea, kpeb, kwa, kwb, kw2, vpea, vpeb, vwa, vwb, vw2, featc_ref,
                 kcf_ref, vc2_ref, xs_scr):
    seq = kt_ref.shape[1]
    n_blocks = seq // LANES
    n_chunks = seq // CMP_STRIDE
    half = KV_WIDTH // NSA_KV_HEADS
    for src_ref, (pea, peb, wa, wb, w2), is_key in ((kt_ref, (kpea, kpeb, kwa, kwb, kw2), True),
                                                   (vt_ref, (vpea, vpeb, vwa, vwb, vw2), False)):
        _transpose_pages(lambda p: src_ref[:, p * LANES:(p + 1) * LANES], xs_scr, n_blocks)
        a, b = _halves(_chunk_rows(xs_scr, n_chunks), pea, peb, wa, wb)
        rows = _summaries(a, _shift_up(b), w2[...])
        if is_key:
            rows_t = rows.T.astype(kcf_ref.dtype)
            kcf_ref[0, 0:half, :] = rows_t[0:half]
            kcf_ref[0, half:, :] = featc_ref[...]
            kcf_ref[1, 0:half, :] = featc_ref[...]
            kcf_ref[1, half:, :] = rows_t[half:]
        else:
            lane = lax.broadcasted_iota(jnp.int32, rows.shape, 1)
            vc2_ref[0] = jnp.where(lane < half, rows, 1.0).astype(vc2_ref.dtype)
            vc2_ref[1] = jnp.where(lane >= half, rows, 1.0).astype(vc2_ref.dtype)


def _feature_rows(pos):
    assert pos.max() // SEL_BLOCK < FEAT_ROWS
    f = np.zeros((FEAT_ROWS, pos.shape[0]), np.float32)
    f[1:] = pos[None, :] // SEL_BLOCK == np.arange(1, FEAT_ROWS)[:, None]
    f[0] = pos % SEL_BLOCK
    return f


def prompt_summaries(kvt_f, cw_k, w2k, cw_v, w2v):
    _, batch, _, seq = kvt_f.shape
    n_c = seq // CMP_STRIDE
    cend = np.minimum(np.arange(n_c) * CMP_STRIDE + (CMP_BLOCK - 1), seq - 1)
    consts = [*cw_k, w2k, *cw_v, w2v, jnp.asarray(_feature_rows(cend), BF16)]
    full = lambda a: pl.BlockSpec(a.shape, lambda b: (0,) * a.ndim)
    return pl.pallas_call(
        _pcmp_kernel,
        grid=(batch,),
        in_specs=[pl.BlockSpec((None, None, KV_WIDTH, seq), lambda b: (0, b, 0, 0)),
                  pl.BlockSpec((None, None, KV_WIDTH, seq), lambda b: (1, b, 0, 0))] + [full(c) for c in consts],
        out_specs=[pl.BlockSpec((None, NSA_KV_HEADS, KV_WIDTH, n_c), lambda b: (b, 0, 0, 0)),
                   pl.BlockSpec((None, NSA_KV_HEADS, n_c, KV_WIDTH), lambda b: (b, 0, 0, 0))],
        out_shape=[jax.ShapeDtypeStruct((batch, NSA_KV_HEADS, KV_WIDTH, n_c), BF16),
                   jax.ShapeDtypeStruct((batch, NSA_KV_HEADS, n_c, KV_WIDTH), BF16)],
        scratch_shapes=[pltpu.VMEM((seq, KV_WIDTH), F32)],
        compiler_params=_cparams("parallel"),
        name="prompt_summaries",
    )(kvt_f, kvt_f, *consts)


def _rank_select_t(score_t, n_cand, topn):
    idx = lax.broadcasted_iota(jnp.int32, score_t.shape, 0)
    rank = jnp.zeros(score_t.shape, jnp.int32)
    for j in range(n_cand):
        row = score_t[j:j + 1, :]
        rank = rank + jnp.where(idx > j, jnp.where(row >= score_t, 1, 0), jnp.where(row > score_t, 1, 0))
    return jnp.where(rank < topn, 1.0, 0.0)


def _pattn_kernel(q_ref, kcf_ref, vc2_ref, kst_ref, vst_ref, kwt_ref, vwt_ref, feat_ref, sm_ref, ovt_ref,
                  o_ref, kfs_scr, kfw_scr, vs_scr, vw_scr, m_scr, acc_scr, *, n_sel, topn):
    qi = pl.program_id(1)
    t0 = qi * Q_BLOCK
    half = KV_WIDTH // NSA_KV_HEADS

    @pl.when(qi == 0)
    def _():
        ones = jnp.ones((half, kst_ref.shape[1]), BF16)
        for src, dst, fill in ((kst_ref, kfs_scr, feat_ref[...]), (kwt_ref, kfw_scr, feat_ref[...]),
                               (vst_ref, vs_scr, ones), (vwt_ref, vw_scr, ones)):
            dst[0, 0:half, :] = src[0:half, :]
            dst[0, half:, :] = fill
            dst[1, 0:half, :] = fill
            dst[1, half:, :] = src[half:, :]

    q = q_ref[...]
    gates = _sigmoid(sm_ref[...])

    def gate_col(branch, h):
        j = GATE_LANE0 + branch * NSA_HEADS + h
        return gates[:, j:j + 1]

    lane = lax.broadcasted_iota(jnp.int32, (Q_BLOCK, LANES), 1)
    feat_f = jnp.where(lane >= half, lane - half, lane).astype(F32)
    sel_lanes = []

    def own_lanes(h):
        return (lane < half) if h < NSA_GROUP else (lane >= half)

    def q_operand(selected):
        parts = []
        for h in range(NSA_HEADS):
            c = jnp.where(feat_f == 0.0, _slope(h), (_slope(h) * SEL_BLOCK) * feat_f)
            if selected:
                c = c + (sel_lanes[h // NSA_GROUP] - 1.0) * MASK_BIG
            parts.append(jnp.where(own_lanes(h), q[:, h * KV_WIDTH:(h + 1) * KV_WIDTH], c.astype(BF16)))
        return jnp.concatenate(parts, axis=0)

    grp_rows = NSA_GROUP * Q_BLOCK

    def scores(qa, kf):
        return jnp.concatenate([_dot(qa[g * grp_rows:(g + 1) * grp_rows], kf(g)) for g in range(NSA_KV_HEADS)], axis=0)

    def split_sums(h, a):
        own = own_lanes(h)
        total = jnp.where(own, pltpu.roll(a, half, 1), a)
        return jnp.where(own, a / jnp.maximum(total, TINY), 0.0), total

    qa_plain = q_operand(False)
    n_c = kcf_ref.shape[2]
    s_all = scores(qa_plain, lambda g: kcf_ref[g])
    tq = lax.broadcasted_iota(jnp.int32, (Q_BLOCK, n_c), 0) + t0
    cend = lax.broadcasted_iota(jnp.int32, (Q_BLOCK, n_c), 1) * CMP_STRIDE + (CMP_BLOCK - 1)
    vis_c = tq >= cend
    psum = [None] * NSA_KV_HEADS
    for h in range(NSA_HEADS):
        g = h // NSA_GROUP
        s = jnp.where(vis_c, s_all[h * Q_BLOCK:(h + 1) * Q_BLOCK], NEG_INF)
        m = jnp.max(s, axis=-1, keepdims=True)
        p = jnp.where(vis_c, jnp.exp(s - m), 0.0)
        o_cmp, total = split_sums(h, _dot(p.astype(BF16), vc2_ref[g]))
        p = p / jnp.concatenate([jnp.maximum(total, TINY)] * (n_c // LANES), axis=1)
        psum[g] = p if psum[g] is None else psum[g] + p
        o_ref[:, h * KV_WIDTH:(h + 1) * KV_WIDTH] = gate_col(0, h) * o_cmp

    s_idx = lax.broadcasted_iota(jnp.int32, (n_sel, Q_BLOCK), 0)
    tq_s = lax.broadcasted_iota(jnp.int32, (n_sel, Q_BLOCK), 1) + t0
    valid = s_idx * SEL_BLOCK <= tq_s
    forced = (s_idx == 0) | (s_idx == tq_s // SEL_BLOCK)
    for g in range(NSA_KV_HEADS):
        hi = psum[g].astype(BF16)
        lo = (psum[g] - hi.astype(F32)).astype(BF16)
        imp_t = _dot_nt(ovt_ref[...], hi) + _dot_nt(ovt_ref[...], lo)
        score_t = jnp.where(valid, imp_t + jnp.where(forced, FORCE_BONUS, 0.0), NEG_INF)
        sel_t = _rank_select_t(score_t, n_sel, topn)
        sel_rows = [sel_t] + ([jnp.zeros((half - n_sel, Q_BLOCK), F32)] if half > n_sel else [])
        other = [jnp.zeros((half, Q_BLOCK), F32)]
        sel_lanes.append(jnp.concatenate(other + sel_rows if g == 0 else sel_rows + other, axis=0).T)

    row_i = lax.broadcasted_iota(jnp.int32, (Q_BLOCK, KEY_BLOCK), 0)
    lane_i = lax.broadcasted_iota(jnp.int32, (Q_BLOCK, KEY_BLOCK), 1)

    def reset():
        m_scr[...] = jnp.full(m_scr.shape, NEG_INF, F32)
        acc_scr[...] = jnp.zeros(acc_scr.shape, F32)

    def flash_step(kb, qa, kf_ref, v_scr, masked):
        k0 = pl.multiple_of(kb * KEY_BLOCK, KEY_BLOCK)
        s_all = scores(qa, lambda g: kf_ref[g, :, pl.ds(k0, KEY_BLOCK)])
        if masked:
            dist = (t0 - k0) + (row_i - lane_i)
            keep = jnp.where(dist >= 0, dist, WINDOW) < WINDOW
        ps = []
        alphas = []
        for h in range(NSA_HEADS):
            rows = slice(h * Q_BLOCK, (h + 1) * Q_BLOCK)
            s = s_all[rows]
            if masked:
                s = jnp.where(keep, s, NEG_INF)
            m_prev = m_scr[rows]
            m_new = jnp.maximum(m_prev, jnp.max(s, axis=-1, keepdims=True))
            p = jnp.exp(s - jnp.concatenate([m_new] * (KEY_BLOCK // LANES), axis=1))
            if masked:
                p = jnp.where(keep, p, 0.0)
            alpha = jnp.exp(m_prev - m_new)
            m_scr[rows] = m_new
            ps.append(p.astype(BF16))
            alphas.append(alpha)
        p_all = jnp.concatenate(ps, axis=0)
        pv = [_dot_nt(p_all[g * grp_rows:(g + 1) * grp_rows], v_scr[g, :, pl.ds(k0, KEY_BLOCK)])
              for g in range(NSA_KV_HEADS)]
        acc_scr[...] = jnp.concatenate(alphas, axis=0) * acc_scr[...] + jnp.concatenate(pv, axis=0)

    def finish(branch):
        for h in range(NSA_HEADS):
            rows = slice(h * Q_BLOCK, (h + 1) * Q_BLOCK)
            cols = slice(h * KV_WIDTH, (h + 1) * KV_WIDTH)
            o_b, _ = split_sums(h, acc_scr[rows])
            o_ref[:, cols] = o_ref[:, cols] + gate_col(branch, h) * o_b

    def loop(lo, hi, qa, kf_ref, v_scr):
        n = jnp.maximum(hi - lo, 0)

        def pair(i, c):
            flash_step(lo + 2 * i, qa, kf_ref, v_scr, False)
            flash_step(lo + 2 * i + 1, qa, kf_ref, v_scr, False)
            return c

        lax.fori_loop(0, n // 2, pair, 0)

        @pl.when(n % 2 == 1)
        def _():
            flash_step(hi - 1, qa, kf_ref, v_scr, False)

    per_key_block = KEY_BLOCK // Q_BLOCK
    kb_diag = qi // per_key_block
    reset()
    qa = q_operand(True)
    loop(0, kb_diag, qa, kfs_scr, vs_scr)
    flash_step(kb_diag, qa, kfs_scr, vs_scr, True)
    finish(1)
    reset()
    qa = qa_plain
    kb_lo = jnp.maximum(qi - WINDOW // Q_BLOCK, 0) // per_key_block

    @pl.when(kb_lo < kb_diag)
    def _():
        flash_step(kb_lo, qa, kfw_scr, vw_scr, True)

    loop(kb_lo + 1, kb_diag, qa, kfw_scr, vw_scr)
    flash_step(kb_diag, qa, kfw_scr, vw_scr, True)
    finish(2)


def prompt_attention(qp, kcf, vc2, kvt_b, r, batch, seq):
    assert seq % KEY_BLOCK == 0 and min(SEL_TOPN, seq // SEL_BLOCK) >= 2
    nqb = seq // Q_BLOCK
    n_c = kcf.shape[3]
    assert n_c % LANES == 0
    n_sel = seq // SEL_BLOCK
    topn = min(SEL_TOPN, n_sel)
    cs = np.arange(n_c)[:, None] * CMP_STRIDE
    ss = np.arange(n_sel)[None, :] * SEL_BLOCK
    ov = np.clip(np.minimum(cs + CMP_BLOCK, ss + SEL_BLOCK) - np.maximum(cs, ss), 0, None) / CMP_BLOCK
    ov[(seq - CMP_BLOCK) // CMP_STRIDE + 1:] = 0.0
    kv_spec = lambda idx: pl.BlockSpec((None, None, KV_WIDTH, seq), lambda b, i: (idx, b, 0, 0))
    return pl.pallas_call(
        functools.partial(_pattn_kernel, n_sel=n_sel, topn=topn),
        grid=(batch, nqb),
        in_specs=[
            pl.BlockSpec((Q_BLOCK, PADDED_Q), lambda b, i: (b * nqb + i, 0)),
            pl.BlockSpec((None, NSA_KV_HEADS, KV_WIDTH, n_c), lambda b, i: (b, 0, 0, 0)),
            pl.BlockSpec((None, NSA_KV_HEADS, n_c, KV_WIDTH), lambda b, i: (b, 0, 0, 0)),
            kv_spec(2), kv_spec(3), kv_spec(4), kv_spec(5),
            pl.BlockSpec((FEAT_ROWS, seq), lambda b, i: (0, 0)),
            pl.BlockSpec((Q_BLOCK, LANES), lambda b, i: (b * nqb + i, R_SMALL // LANES)),
            pl.BlockSpec((n_sel, n_c), lambda b, i: (0, 0)),
        ],
        out_specs=pl.BlockSpec((Q_BLOCK, PADDED_Q), lambda b, i: (b * nqb + i, 0)),
        out_shape=jax.ShapeDtypeStruct((batch * seq, PADDED_Q), F32),
        scratch_shapes=[
            pltpu.VMEM((NSA_KV_HEADS, KV_WIDTH, seq), BF16),
            pltpu.VMEM((NSA_KV_HEADS, KV_WIDTH, seq), BF16),
            pltpu.VMEM((NSA_KV_HEADS, KV_WIDTH, seq), BF16),
            pltpu.VMEM((NSA_KV_HEADS, KV_WIDTH, seq), BF16),
            pltpu.VMEM((NSA_HEADS * Q_BLOCK, LANES), F32),
            pltpu.VMEM((NSA_HEADS * Q_BLOCK, KV_WIDTH), F32),
        ],
        compiler_params=_cparams("parallel", "arbitrary"),
        name="prompt_nsa",
    )(qp, kcf, vc2, kvt_b, kvt_b, kvt_b, kvt_b, jnp.asarray(_feature_rows(np.arange(seq)), BF16), r,
      jnp.asarray(ov.T, BF16))


def _gla_log_decay(sm, aup_ref, ab_ref):
    z = _dot(sm.astype(BF16), aup_ref[...]) + ab_ref[...]
    return _log_sigmoid(z) / GLA_GATE_TAU


def _gla_head_norm(o, gg_ref):
    on = o * lax.rsqrt(jnp.mean(o * o, axis=-1, keepdims=True) + NORM_EPS)
    return on * gg_ref[...]


def _pgla_kernel(qg_ref, kg_ref, vg_ref, sm_ref, aup_ref, ab_ref, gg_ref, og_ref, sfin_ref, st_scr, *, n_tiles):
    ti = pl.program_id(1)
    tt = qg_ref.shape[0]
    width = qg_ref.shape[1]

    @pl.when(ti == 0)
    def _():
        st_scr[...] = jnp.zeros(st_scr.shape, F32)

    la = _gla_log_decay(sm_ref[...], aup_ref, ab_ref)
    rin = lax.broadcasted_iota(jnp.int32, la.shape, 0)
    bg = la
    sh = 1
    while sh < tt:
        bg = bg + jnp.where(rin >= sh, pltpu.roll(bg, sh, 0), 0.0)
        sh *= 2

    def row_of_block(x, block, row):
        x3 = x.reshape(tt // block, block, width)
        return jnp.broadcast_to(x3[:, row:row + 1, :], x3.shape).reshape(tt, width)

    n_chunks = tt // GLA_CHUNK
    ends = bg.reshape(n_chunks, GLA_CHUNK, width)[:, GLA_CHUNK - 1:GLA_CHUNK, :]
    prev_end = jnp.concatenate([jnp.zeros((1, 1, width), F32), ends[:n_chunks - 1]], axis=0)
    b = (bg.reshape(n_chunks, GLA_CHUNK, width) - prev_end).reshape(tt, width)
    bg_end = bg[tt - 1:tt, :]
    scales = [(jnp.exp(b), jnp.exp(-b))]
    block = 2 * GLA_CHUNK
    while block <= tt:
        mid = row_of_block(bg, block, block // 2 - 1)
        scales.append((jnp.exp(jnp.minimum(bg - mid, 0.0)), jnp.exp(jnp.minimum(mid - bg, 0.0))))
        block *= 2
    e_in = jnp.exp(bg)
    e_out = jnp.exp(bg_end - bg)

    r_i = lax.broadcasted_iota(jnp.int32, (tt, tt), 0)
    c_i = lax.broadcasted_iota(jnp.int32, (tt, tt), 1)
    level = jnp.full((tt, tt), len(scales) - 1, jnp.int32)
    block = tt // 2
    lv = len(scales) - 2
    while block >= GLA_CHUNK:
        level = jnp.where(r_i // block == c_i // block, lv, level)
        block //= 2
        lv -= 1
    level = jnp.where(c_i <= r_i, level, -1)

    for h in range(GLA_HEADS):
        ks = slice(h * GLA_DK, (h + 1) * GLA_DK)
        vs = slice(h * GLA_DV, (h + 1) * GLA_DV)
        q = qg_ref[:, ks] * (GLA_DK ** -0.5)
        k = kg_ref[:, ks]
        v = vg_ref[:, vs].astype(BF16)
        a = jnp.zeros((tt, tt), F32)
        for lv, (sq, sk) in enumerate(scales):
            a_lv = _dot_nt((q * sq[:, ks]).astype(BF16), (k * sk[:, ks]).astype(BF16))
            a = jnp.where(level == lv, a_lv, a)
        st = st_scr[h]
        o = _dot(a.astype(BF16), v) + _dot_nt((q * e_in[:, ks]).astype(BF16), st.astype(BF16))
        kv_t = lax.dot_general(v, (k * e_out[:, ks]).astype(BF16), (((0,), (0,)), ((), ())),
                               preferred_element_type=F32)
        st_scr[h] = st * jnp.exp(bg_end[:, ks]) + kv_t
        og_ref[:, vs] = _gla_head_norm(o, gg_ref)

    @pl.when(ti == n_tiles - 1)
    def _():
        for h in range(GLA_HEADS):
            sfin_ref[h] = st_scr[h].T


def prompt_gla(r, aup, ab, gg, batch, seq):
    tt = _tile(seq, 256, GLA_CHUNK)
    nt = seq // tt
    row = lambda b, i: b * nt + i
    return pl.pallas_call(
        functools.partial(_pgla_kernel, n_tiles=nt),
        grid=(batch, nt),
        in_specs=[
            pl.BlockSpec((tt, GLA_KEY_WIDTH), lambda b, i: (row(b, i), R_QG // GLA_KEY_WIDTH)),
            pl.BlockSpec((tt, GLA_KEY_WIDTH), lambda b, i: (row(b, i), R_KG // GLA_KEY_WIDTH)),
            pl.BlockSpec((tt, GLA_VAL_WIDTH), lambda b, i: (row(b, i), R_VG // GLA_VAL_WIDTH)),
            pl.BlockSpec((tt, LANES), lambda b, i: (row(b, i), R_SMALL // LANES)),
            pl.BlockSpec(aup.shape, lambda b, i: (0, 0)),
            pl.BlockSpec(ab.shape, lambda b, i: (0, 0)),
            pl.BlockSpec(gg.shape, lambda b, i: (0, 0)),
        ],
        out_specs=[
            pl.BlockSpec((tt, GLA_VAL_WIDTH), lambda b, i: (row(b, i), 0)),
            pl.BlockSpec((None, GLA_HEADS, GLA_DK, GLA_DV), lambda b, i: (b, 0, 0, 0)),
        ],
        out_shape=[
            jax.ShapeDtypeStruct((batch * seq, GLA_VAL_WIDTH), F32),
            jax.ShapeDtypeStruct((batch, GLA_HEADS, GLA_DK, GLA_DV), F32),
        ],
        scratch_shapes=[pltpu.VMEM((GLA_HEADS, GLA_DV, GLA_DK), F32)],
        compiler_params=_cparams("parallel", "arbitrary"),
        name="prompt_gla",
    )(r, r, r, r, aup, ab, gg)


def _sgla_kernel(qg_ref, kg_ref, vg_ref, sm_ref, aup_ref, ab_ref, gg_ref, s_ref, og_ref, snew_ref, *, ds):
    rows = qg_ref.shape[0]
    nb = rows // ds
    la = _gla_log_decay(sm_ref[...], aup_ref, ab_ref)
    ri = lax.broadcasted_iota(jnp.int32, la.shape, 0) % ds
    b = la
    sh = 1
    while sh < ds:
        b = b + jnp.where(ri >= sh, pltpu.roll(b, sh, 0), 0.0)
        sh *= 2
    b_last = b
    for d in range(1, ds):
        b_last = jnp.where(ri == ds - 1 - d, pltpu.roll(b, rows - d, 0), b_last)
    e_b = jnp.exp(b)
    e_nb = jnp.exp(-b)
    e_tail = jnp.exp(b_last - b)
    e_last = jnp.exp(b_last)

    r_i = lax.broadcasted_iota(jnp.int32, (rows, rows), 0)
    c_i = lax.broadcasted_iota(jnp.int32, (rows, rows), 1)
    causal = (r_i // ds == c_i // ds) & (c_i <= r_i)
    row_b = lax.broadcasted_iota(jnp.int32, (rows, GLA_DV), 0) // ds

    for h in range(GLA_HEADS):
        ks = slice(h * GLA_DK, (h + 1) * GLA_DK)
        vs = slice(h * GLA_DV, (h + 1) * GLA_DV)
        q = qg_ref[:, ks] * (GLA_DK ** -0.5)
        k = kg_ref[:, ks]
        v = vg_ref[:, vs]
        q_dec = (q * e_b[:, ks]).astype(BF16)
        k_inv = (k * e_nb[:, ks]).astype(BF16)
        k_tail_t = (k * e_tail[:, ks]).T
        e_last_t = e_last[:, ks].T
        a = jnp.where(causal, _dot_nt(q_dec, k_inv), 0.0)
        o = _dot(a.astype(BF16), v.astype(BF16))
        for bb in range(nb):
            s_prev = s_ref[bb, h]
            o_inter = _dot(q_dec, s_prev.astype(BF16))
            o = o + jnp.where(row_b == bb, o_inter, 0.0)
            s_new = e_last_t[:, bb * ds:bb * ds + 1] * s_prev
            for j in range(ds):
                rr = bb * ds + j
                s_new = s_new + k_tail_t[:, rr:rr + 1] * v[rr:rr + 1, :]
            snew_ref[bb, h] = s_new
        og_ref[:, vs] = _gla_head_norm(o, gg_ref)


def sample_gla(r, aup, ab, gg, state, layer, dec_batch, ds):
    nb = _tile(dec_batch, 8, 1)
    rows = nb * ds
    return pl.pallas_call(
        functools.partial(_sgla_kernel, ds=ds),
        grid=(dec_batch // nb,),
        in_specs=[
            pl.BlockSpec((rows, GLA_KEY_WIDTH), lambda i: (i, R_QG // GLA_KEY_WIDTH)),
            pl.BlockSpec((rows, GLA_KEY_WIDTH), lambda i: (i, R_KG // GLA_KEY_WIDTH)),
            pl.BlockSpec((rows, GLA_VAL_WIDTH), lambda i: (i, R_VG // GLA_VAL_WIDTH)),
            pl.BlockSpec((rows, LANES), lambda i: (i, R_SMALL // LANES)),
            pl.BlockSpec(aup.shape, lambda i: (0, 0)),
            pl.BlockSpec(ab.shape, lambda i: (0, 0)),
            pl.BlockSpec(gg.shape, lambda i: (0, 0)),
            pl.BlockSpec((None, nb, GLA_HEADS, GLA_DK, GLA_DV), lambda i: (layer, i, 0, 0, 0)),
        ],
        out_specs=[
            pl.BlockSpec((rows, GLA_VAL_WIDTH), lambda i: (i, 0)),
            pl.BlockSpec((nb, GLA_HEADS, GLA_DK, GLA_DV), lambda i: (i, 0, 0, 0)),
        ],
        out_shape=[
            jax.ShapeDtypeStruct((dec_batch * ds, GLA_VAL_WIDTH), F32),
            jax.ShapeDtypeStruct((dec_batch, GLA_HEADS, GLA_DK, GLA_DV), F32),
        ],
        compiler_params=_cparams("parallel"),
        name="sample_gla",
    )(r, r, r, r, aup, ab, gg, state)


def _mix_kernel(x_ref, on_ref, za_ref, og_ref, zb_ref, mg_ref, wba_ref, wbb_ref, wo_ref, fg_ref, *o_refs):
    pa = _dot((on_ref[...] * _silu(za_ref[...])).astype(BF16), wba_ref[...])
    pb = _dot((og_ref[...] * _silu(zb_ref[...])).astype(BF16), wbb_ref[...])
    mix = _sigmoid(mg_ref[:, :D_MODEL]) * pa + _sigmoid(mg_ref[:, D_MODEL:]) * pb
    y = x_ref[...] + _dot(mix.astype(BF16), wo_ref[...])
    o_refs[0][...] = y
    if len(o_refs) > 1:
        yn = y * lax.rsqrt(jnp.mean(y * y, axis=-1, keepdims=True) + NORM_EPS)
        o_refs[1][...] = yn * fg_ref[...]


def mixer_output(x, o_nsa, r, o_gla, wba, wbb, wo, final_g, with_final_norm):
    n = x.shape[0]
    tm = _tile(n, 256)
    row = lambda w: pl.BlockSpec((tm, w), lambda i: (i, 0))
    wspec = lambda w: pl.BlockSpec(w.shape, lambda i: (0, 0))
    n_out = 2 if with_final_norm else 1
    return pl.pallas_call(
        _mix_kernel,
        grid=(n // tm,),
        in_specs=[
            row(D_MODEL), row(PADDED_Q),
            pl.BlockSpec((tm, PADDED_Q), lambda i: (i, R_ZA // PADDED_Q)),
            row(GLA_VAL_WIDTH),
            pl.BlockSpec((tm, GLA_VAL_WIDTH), lambda i: (i, R_ZB // GLA_VAL_WIDTH)),
            pl.BlockSpec((tm, 2 * D_MODEL), lambda i: (i, R_MG // (2 * D_MODEL))),
            wspec(wba), wspec(wbb), wspec(wo), wspec(final_g),
        ],
        out_specs=[row(D_MODEL)] * n_out,
        out_shape=[jax.ShapeDtypeStruct((n, D_MODEL), F32)] * n_out,
        compiler_params=_cparams("parallel"),
        name="mixer_output",
    )(x, o_nsa, r, o_gla, r, r, wba, wbb, wo, final_g)


def _select_blocks(score, lane_idx, n_cand, topn):
    rank = jnp.zeros(score.shape, jnp.int32)
    for j in range(n_cand):
        col = score[:, j:j + 1]
        rank = rank + jnp.where(col > score, 1, 0) + jnp.where(col == score, jnp.where(lane_idx > j, 1, 0), 0)
    return jnp.where(rank < topn, 1.0, 0.0)


def _softmax_two(parts, keeps):
    m = functools.reduce(jnp.maximum, [jnp.max(s, axis=-1, keepdims=True) for s in parts])
    ps = [jnp.where(kp, jnp.exp(s - m), 0.0) for s, kp in zip(parts, keeps)]
    total = functools.reduce(lambda a, c: a + c, [jnp.sum(p, axis=-1, keepdims=True) for p in ps])
    return ps, jnp.maximum(total, TINY)


def _smain_kernel(pt_ref, q_ref, gate_ref, slope_ref, ab_hbm, kt_hbm, vt_hbm, kw_ref, vw_ref, new_ref,
                  w2k_ref, w2v_ref, ov_ref, exp_ref, exn_ref, o_ref, kw_out_ref, vw_out_ref,
                  ab_buf, kt_buf, vt_buf, sems, *, layer, n_pages, past, ds, n_sel, topn):
    b = pl.program_id(0)
    slot = b % 2
    rows = q_ref.shape[0]
    grp_rows = NSA_KV_HEADS * ds
    n_new = new_ref.shape[2]
    b_loc = b % (n_new // ds)

    def page_copies(bb, sl):
        cps = []
        for j in range(n_pages):
            pg = pt_ref[bb, j]
            cps.append(pltpu.make_async_copy(ab_hbm.at[pg], ab_buf.at[sl, j], sems.at[sl, 0]))
            cps.append(pltpu.make_async_copy(kt_hbm.at[layer, pg], kt_buf.at[sl, j], sems.at[sl, 1]))
            cps.append(pltpu.make_async_copy(vt_hbm.at[layer, pg], vt_buf.at[sl, j], sems.at[sl, 2]))
        return cps

    @pl.when(b == 0)
    def _():
        for cp in page_copies(0, 0):
            cp.start()

    @pl.when(b + 1 < pl.num_programs(0))
    def _():
        for cp in page_copies(b + 1, 1 - slot):
            cp.start()

    for cp in page_copies(b, slot):
        cp.wait()
    kt_refs = [kt_buf.at[slot, j] for j in range(n_pages)]
    vt_refs = [vt_buf.at[slot, j] for j in range(n_pages)]

    qs = q_ref[...]
    slope = slope_ref[:, 0:1]
    ab = jnp.concatenate([ab_buf[slot, j] for j in range(n_pages)], axis=0)
    hid = ab.shape[1] // 4
    kc = _summaries(ab[:, 0:hid], _shift_up(ab[:, hid:2 * hid]), w2k_ref[...]).astype(BF16)
    vc = _summaries(ab[:, 2 * hid:3 * hid], _shift_up(ab[:, 3 * hid:]), w2v_ref[...]).astype(BF16)
    n_c = kc.shape[0]

    def irow(shape):
        return lax.broadcasted_iota(jnp.int32, shape, 0) % ds

    dist_c = (past + irow((rows, n_c))) - (lax.broadcasted_iota(jnp.int32, (rows, n_c), 1) * CMP_STRIDE
                                          + (CMP_BLOCK - 1))
    vis_c = dist_c >= 0
    s_c = jnp.where(vis_c, _dot_nt(qs, kc) - slope * dist_c.astype(F32), NEG_INF)
    m_c = jnp.max(s_c, axis=-1, keepdims=True)
    p_c = jnp.where(vis_c, jnp.exp(s_c - m_c), 0.0)
    p_c = p_c / jnp.maximum(jnp.sum(p_c, axis=-1, keepdims=True), TINY)
    o_cmp = _dot(p_c.astype(BF16), vc)

    hi = p_c.astype(BF16)
    lo = (p_c - hi.astype(F32)).astype(BF16)
    imp_h = _dot(hi, ov_ref[...]) + _dot(lo, ov_ref[...])
    imp = functools.reduce(lambda a, c: a + c,
                           [imp_h[r * grp_rows:(r + 1) * grp_rows] for r in range(NSA_GROUP)])
    s_idx = lax.broadcasted_iota(jnp.int32, imp.shape, 1)
    qpos = past + irow(imp.shape)
    valid = (s_idx < n_sel) & (s_idx * SEL_BLOCK <= qpos)
    forced = (s_idx == 0) | (s_idx == qpos // SEL_BLOCK)
    score = jnp.where(valid, imp + jnp.where(forced, FORCE_BONUS, 0.0), NEG_INF)
    sel = _select_blocks(score, s_idx, n_sel, topn)
    sel = jnp.concatenate([sel] * NSA_GROUP, axis=0).astype(BF16)
    keep_past = _dot(sel, exp_ref[...]) > 0.5
    sel_new = _dot(sel, exn_ref[...]) > 0.5

    slot = lax.broadcasted_iota(jnp.int32, (rows, n_new), 1)
    dist_n = irow((rows, n_new)) - slot % ds
    mine = (slot // ds == b_loc) & (dist_n >= 0)
    bias_n = slope * dist_n.astype(F32)

    s_p = jnp.concatenate([_dot(qs, r[...].astype(BF16)) for r in kt_refs], axis=1)
    dist_p = (past + irow(s_p.shape)) - lax.broadcasted_iota(jnp.int32, s_p.shape, 1)
    s_p = jnp.where(keep_past, s_p - slope * dist_p.astype(F32), NEG_INF)
    keep_n = mine & sel_new
    s_n = jnp.where(keep_n, _dot(qs, new_ref[2].astype(BF16)) - bias_n, NEG_INF)
    (p_p, p_n), total = _softmax_two([s_p, s_n], [keep_past, keep_n])
    acc = _dot_nt(p_n.astype(BF16), new_ref[3].astype(BF16))
    for j, r in enumerate(vt_refs):
        acc = acc + _dot_nt(p_p[:, j * LANES:(j + 1) * LANES].astype(BF16), r[...].astype(BF16))
    o_slc = acc / total

    kbuf = kw_ref[...]
    vbuf = vw_ref[...]
    w_buf = kbuf.shape[1]
    dist_w = (w_buf + irow((rows, w_buf))) - lax.broadcasted_iota(jnp.int32, (rows, w_buf), 1)
    keep_w = dist_w < WINDOW
    s_w = jnp.where(keep_w, _dot(qs, kbuf.astype(BF16)) - slope * dist_w.astype(F32), NEG_INF)
    s_wn = jnp.where(mine, _dot(qs, new_ref[4].astype(BF16)) - bias_n, NEG_INF)
    (p_w, p_wn), total_w = _softmax_two([s_w, s_wn], [keep_w, mine])
    o_win = (_dot_nt(p_w.astype(BF16), vbuf.astype(BF16)) + _dot_nt(p_wn.astype(BF16), new_ref[5].astype(BF16))) / total_w

    gates = _sigmoid(gate_ref[...])
    o_ref[...] = gates[:, 0:1] * o_cmp + gates[:, 1:2] * o_slc + gates[:, 2:3] * o_win

    lane_new = lax.broadcasted_iota(jnp.int32, (KV_WIDTH, n_new), 1)
    shift = (n_new - ds) - b_loc * ds
    for buf, slab, out_ref in ((kbuf, 4, kw_out_ref), (vbuf, 5, vw_out_ref)):
        rolled = pltpu.roll(buf, w_buf - ds, 1)
        own_last = pltpu.roll(new_ref[slab], shift, 1)
        out_ref[:, 0:w_buf - n_new] = rolled[:, 0:w_buf - n_new]
        out_ref[:, w_buf - n_new:w_buf] = jnp.where(lane_new >= n_new - ds, own_last, rolled[:, w_buf - n_new:w_buf])


def sample_attention(page_table, q_rows, gate_rows, ab_pages, slc_pools_t, win_bufs_t, new_t, w2k, w2v, layer, past, ds):
    assert past % SEL_BLOCK == 0 and past % LANES == 0
    dec_batch, n_pages = page_table.shape
    rows = q_rows.shape[1]
    w_buf = win_bufs_t[0].shape[3]
    n_new = min(LANES, dec_batch * ds)
    assert (dec_batch * ds) % n_new == 0 and n_new % ds == 0 and w_buf >= n_new
    n_c = n_pages * (LANES // CMP_STRIDE)
    n_cmp = (past + ds - CMP_BLOCK) // CMP_STRIDE + 1
    n_sel = -(-(past + ds) // SEL_BLOCK)
    topn = min(SEL_TOPN, n_sel)
    assert n_sel <= LANES and ds <= CMP_STRIDE
    cs = np.arange(n_c)[:, None] * CMP_STRIDE
    ss = np.arange(LANES)[None, :] * SEL_BLOCK
    ov = np.clip(np.minimum(cs + CMP_BLOCK, ss + SEL_BLOCK) - np.maximum(cs, ss), 0, None) / CMP_BLOCK
    ov[n_cmp:] = 0.0
    ov[:, n_sel:] = 0.0
    ex_past = (np.arange(past)[None, :] // SEL_BLOCK == np.arange(LANES)[:, None]).astype(np.float32)
    ex_new = ((past + np.arange(n_new) % ds)[None, :] // SEL_BLOCK == np.arange(LANES)[:, None]).astype(np.float32)
    head = np.arange(rows) // (NSA_KV_HEADS * ds) + NSA_GROUP * ((np.arange(rows) // ds) % NSA_KV_HEADS)
    slope = np.broadcast_to((2.0 ** -(head + 1.0))[:, None], (rows, LANES)).astype(np.float32)

    const = lambda shape: pl.BlockSpec(shape, lambda b, pt: (0,) * len(shape))
    in_specs = [pl.BlockSpec((None, rows, LANES), lambda b, pt: (b, 0, 0)),
                pl.BlockSpec((None, rows, LANES), lambda b, pt: (b, 0, 0)),
                const((rows, LANES))]
    operands = [q_rows, gate_rows, jnp.asarray(slope)]
    in_specs += [pl.BlockSpec(memory_space=pl.ANY)] * 3
    operands += [ab_pages, slc_pools_t[0], slc_pools_t[1]]
    for arr in win_bufs_t:
        in_specs.append(pl.BlockSpec((None, None, KV_WIDTH, w_buf), lambda b, pt: (layer, b, 0, 0)))
        operands.append(arr)
    per_tile = n_new // ds
    in_specs.append(pl.BlockSpec((6, None, KV_WIDTH, n_new), lambda b, pt: (0, 0, 0, b // per_tile)))
    operands.append(new_t)
    consts = [w2k, w2v, jnp.asarray(ov, BF16), jnp.asarray(ex_past, BF16), jnp.asarray(ex_new, BF16)]
    in_specs += [const(c.shape) for c in consts]
    operands += consts
    buf_out = pl.BlockSpec((None, KV_WIDTH, w_buf), lambda b, pt: (b, 0, 0))
    return pl.pallas_call(
        functools.partial(_smain_kernel, layer=layer, n_pages=n_pages, past=past, ds=ds, n_sel=n_sel, topn=topn),
        grid_spec=pltpu.PrefetchScalarGridSpec(
            num_scalar_prefetch=1,
            grid=(dec_batch,),
            in_specs=in_specs,
            out_specs=[pl.BlockSpec((None, rows, LANES), lambda b, pt: (b, 0, 0)), buf_out, buf_out],
            scratch_shapes=[pltpu.VMEM((2, n_pages) + ab_pages.shape[1:], F32),
                            pltpu.VMEM((2, n_pages, KV_WIDTH, LANES), F32),
                            pltpu.VMEM((2, n_pages, KV_WIDTH, LANES), F32),
                            pltpu.SemaphoreType.DMA((2, 3))],
        ),
        out_shape=[jax.ShapeDtypeStruct((dec_batch, rows, LANES), F32),
                   jax.ShapeDtypeStruct((dec_batch, KV_WIDTH, w_buf), F32),
                   jax.ShapeDtypeStruct((dec_batch, KV_WIDTH, w_buf), F32)],
        compiler_params=_cparams("arbitrary"),
        name="sample_nsa",
    )(page_table, *operands)


def _pad_heads_cols(w):
    k = w.shape[0]
    w4 = w.reshape(k, NSA_KV_HEADS, NSA_GROUP, HEAD_DIM)
    eye = jnp.eye(NSA_KV_HEADS, dtype=w.dtype)
    return jnp.einsum("kgrd,gp->kgrpd", w4, eye).reshape(k, PADDED_Q)


def _layer_weights(w_in, pk_pe, pk_w1, pk_w2, pv_pe, pv_w1, pv_w2, a_up, a_b, gla_g, w_ba, w_bb, w_out):
    o = _OFF
    col = lambda i: w_in[:, o[i]:o[i + 1]]
    small = jnp.concatenate([col(2), col(7), jnp.zeros((D_MODEL, LANES - 3 * NSA_HEADS - GLA_GATE_RANK), F32)], axis=1)
    w_q = _pad_heads_cols(col(0)).astype(BF16)
    w_kv_t = col(1).T.astype(BF16)
    w_r = jnp.concatenate([_pad_heads_cols(col(3)), col(4), col(5), col(6), col(8), col(9), small], axis=1).astype(BF16)
    half = CMP_BLOCK // 2

    def cmp_weights(pe, w1, w2):
        eye = jnp.eye(NSA_KV_HEADS, dtype=F32)
        w1h = w1.reshape(2, half, HEAD_DIM, CMP_HIDDEN)
        big = lambda w: jnp.einsum("ldh,gk->lgdkh", w, eye).reshape(half * KV_WIDTH, NSA_KV_HEADS * CMP_HIDDEN)
        peh = pe.reshape(2, half, 1, HEAD_DIM)
        pe_row = lambda p: jnp.broadcast_to(p, (half, NSA_KV_HEADS, HEAD_DIM)).reshape(1, half * KV_WIDTH)
        w2bd = jnp.einsum("hd,gk->ghkd", w2, eye).reshape(NSA_KV_HEADS * CMP_HIDDEN, KV_WIDTH)
        return (pe_row(peh[0]), pe_row(peh[1]), big(w1h[0]).astype(BF16), big(w1h[1]).astype(BF16)), w2bd.astype(BF16)

    cw_k, w2k = cmp_weights(pk_pe, pk_w1, pk_w2)
    cw_v, w2v = cmp_weights(pv_pe, pv_w1, pv_w2)
    aup = jnp.zeros((LANES, GLA_KEY_WIDTH), F32).at[ALOW_LANE0:ALOW_LANE0 + GLA_GATE_RANK].set(a_up).astype(BF16)
    w_ba_p = _pad_heads_cols(w_ba.T).T.astype(BF16)
    return dict(w_q=w_q, w_kv_t=w_kv_t, w_r=w_r, cw_k=cw_k, cw_v=cw_v, w2k=w2k, w2v=w2v, aup=aup,
                ab=a_b.reshape(1, GLA_KEY_WIDTH), gg=gla_g.reshape(1, GLA_DV),
                w_ba=w_ba_p, w_bb=w_bb.astype(BF16), w_out=w_out.astype(BF16))


def _project(x, g_norm, lw, batch, seq):
    h = rms_norm_rows(x, g_norm, BF16)
    qp = matmul_rows(h, lw["w_q"], BF16, tn=512, scale=HEAD_DIM ** -0.5)
    kvt_f, kvt_b = kv_project_t(h, lw["w_kv_t"], batch, seq)
    r = matmul_rows(h, lw["w_r"], F32, tn=R_WIDTH // 7)
    return qp, kvt_f, kvt_b, r


def _tokens_minor(cache):
    lead = cache.shape[:-3]
    n = cache.ndim
    perm = tuple(range(n - 3)) + (n - 2, n - 1, n - 3)
    return cache.transpose(perm).reshape(lead + (KV_WIDTH, cache.shape[-3]))


def _tokens_major(x_t):
    lead = x_t.shape[:-2]
    n = len(lead)
    x5 = x_t.reshape(lead + (NSA_KV_HEADS, HEAD_DIM, x_t.shape[-1]))
    return x5.transpose(tuple(range(n)) + (n + 2, n, n + 1))


def kernel(x_prompt, x_sample, cache_k_cmp, cache_v_cmp, cache_k_slc, cache_v_slc, cache_k_win, cache_v_win, state_gla, page_table, norm_g, w_in, phi_k_pe, phi_k_w1, phi_k_w2, phi_v_pe, phi_v_w1, phi_v_w2, gla_alpha_up, gla_alpha_b, gla_norm_g, w_branch_a, w_branch_b, w_out, final_norm_g):
    batch, seq, _ = x_prompt.shape
    dec_batch, ds, _ = x_sample.shape
    depth = norm_g.shape[0]
    n_pool, page_size = cache_k_cmp.shape[1:3]
    assert page_size == LANES
    n_pages = page_table.shape[1]
    past = n_pages * page_size
    chunks_per_page = page_size // CMP_STRIDE
    final_g = final_norm_g.reshape(1, D_MODEL)

    cmp_pools_t = (_tokens_minor(cache_k_cmp), _tokens_minor(cache_v_cmp))
    slc_pools_t = (_tokens_minor(cache_k_slc), _tokens_minor(cache_v_slc))
    win_bufs_t = (_tokens_minor(cache_k_win), _tokens_minor(cache_v_win))

    y_p = x_prompt.reshape(batch * seq, D_MODEL)
    y_s = x_sample.reshape(dec_batch * ds, D_MODEL)
    outs_p, outs_s = [], []
    for l in range(depth):
        lw = _layer_weights(w_in[l], phi_k_pe[l], phi_k_w1[l], phi_k_w2[l], phi_v_pe[l], phi_v_w1[l], phi_v_w2[l],
                            gla_alpha_up[l], gla_alpha_b[l], gla_norm_g[l], w_branch_a[l], w_branch_b[l], w_out[l])
        last = l == depth - 1

        qp, kvt_f, kvt_b, r = _project(y_p, norm_g[l], lw, batch, seq)
        kcf, vc2 = prompt_summaries(kvt_f, lw["cw_k"], lw["w2k"], lw["cw_v"], lw["w2v"])
        o_nsa = prompt_attention(qp, kcf, vc2, kvt_b, r, batch, seq)
        o_gla, s_gla = prompt_gla(r, lw["aup"], lw["ab"], lw["gg"], batch, seq)
        res = mixer_output(y_p, o_nsa, r, o_gla, lw["w_ba"], lw["w_bb"], lw["w_out"], final_g, last)
        y_p = res[0]
        if last:
            y_p_out = res[1]
        keep = min(WINDOW, seq)
        kv5 = _tokens_major(kvt_f)
        outs_p.append((kv5[0], kv5[1], kv5[2], kv5[3], kv5[4][:, seq - keep:], kv5[5][:, seq - keep:], s_gla))

        qs, kvs_t, _, rs = _project(y_s, norm_g[l], lw, 1, dec_batch * ds)
        ab_pages = pool_halves(cmp_pools_t[0], cmp_pools_t[1], l, lw["cw_k"], lw["cw_v"])
        ab_pages = ab_pages.reshape(n_pool, chunks_per_page, -1)
        q_rows = qs.reshape(dec_batch, ds, NSA_KV_HEADS, NSA_GROUP, KV_WIDTH).transpose(0, 3, 2, 1, 4)
        q_rows = q_rows.reshape(dec_batch, NSA_HEADS * ds, KV_WIDTH)
        gl = rs[:, R_SMALL + GATE_LANE0:R_SMALL + GATE_LANE0 + 3 * NSA_HEADS]
        gl = gl.reshape(dec_batch, ds, 3, NSA_KV_HEADS, NSA_GROUP).transpose(0, 4, 3, 1, 2)
        gate_rows = jnp.pad(gl.reshape(dec_batch, NSA_HEADS * ds, 3), ((0, 0), (0, 0), (0, LANES - 3)))
        o_rows, kw_new, vw_new = sample_attention(page_table, q_rows, gate_rows, ab_pages, slc_pools_t, win_bufs_t,
                                                  kvs_t, lw["w2k"], lw["w2v"], l, past, ds)
        o_nsa_s = o_rows.reshape(dec_batch, NSA_GROUP, NSA_KV_HEADS, ds, KV_WIDTH).transpose(0, 3, 2, 1, 4)
        o_nsa_s = o_nsa_s.reshape(dec_batch * ds, PADDED_Q)
        o_gla_s, s_gla_s = sample_gla(rs, lw["aup"], lw["ab"], lw["gg"], state_gla, l, dec_batch, ds)
        res = mixer_output(y_s, o_nsa_s, rs, o_gla_s, lw["w_ba"], lw["w_bb"], lw["w_out"], final_g, last)
        y_s = res[0]
        if last:
            y_s_out = res[1]
        kvs5 = _tokens_major(kvs_t[:, 0]).reshape(6, dec_batch, ds, NSA_KV_HEADS, HEAD_DIM)
        outs_s.append((kvs5[0], kvs5[1], kvs5[2], kvs5[3], _tokens_major(kw_new), _tokens_major(vw_new), s_gla_s))

    stack = lambda outs: [jnp.stack(t) for t in zip(*outs)]
    return (y_p_out.reshape(batch, seq, D_MODEL), y_s_out.reshape(dec_batch, ds, D_MODEL),
            *stack(outs_p), *stack(outs_s))
```

```python
import functools

import numpy as np
import jax
import jax.numpy as jnp
from jax import lax
from jax.experimental import pallas as pl
from jax.experimental.pallas import tpu as pltpu

F32 = jnp.float32
BF16 = jnp.bfloat16

D_MODEL = 1024
NSA_HEADS = 8
NSA_KV_HEADS = 2
NSA_GROUP = NSA_HEADS // NSA_KV_HEADS
HEAD_DIM = 64
NSA_WIDTH = NSA_HEADS * HEAD_DIM
KV_WIDTH = NSA_KV_HEADS * HEAD_DIM
CMP_BLOCK = 32
CMP_STRIDE = 16
CMP_HIDDEN = 2 * HEAD_DIM
SEL_BLOCK = 64
SEL_TOPN = 16
WINDOW = 512
GLA_HEADS = 4
GLA_KEY_WIDTH = D_MODEL // 2
GLA_VAL_WIDTH = D_MODEL
GLA_DK = GLA_KEY_WIDTH // GLA_HEADS
GLA_DV = GLA_VAL_WIDTH // GLA_HEADS
GLA_GATE_RANK = 16
GLA_GATE_TAU = 16.0
GLA_CHUNK = 32
Q_BLOCK = 128
KEY_BLOCK = 256
NORM_EPS = 1e-6
NEG_INF = -1e30
TINY = 1e-30
FORCE_BONUS = 1e4
MASK_BIG = float(2.0 ** 100)
PADDED_Q = NSA_HEADS * KV_WIDTH
LANES = 128

_IN_SIZES = (NSA_WIDTH, 6 * KV_WIDTH, 3 * NSA_HEADS, NSA_WIDTH, GLA_KEY_WIDTH, GLA_KEY_WIDTH,
             GLA_VAL_WIDTH, GLA_GATE_RANK, GLA_VAL_WIDTH, 2 * D_MODEL)
_OFF = tuple(int(o) for o in np.cumsum((0,) + _IN_SIZES))

R_ZA, R_QG, R_KG, R_VG, R_ZB, R_MG, R_SMALL = 0, 1024, 1536, 2048, 3072, 4096, 6144
R_WIDTH = R_SMALL + LANES
GATE_LANE0 = 0
ALOW_LANE0 = 3 * NSA_HEADS
FEAT_ROWS = KV_WIDTH // NSA_KV_HEADS

VMEM_LIMIT = 48 * 1024 * 1024


def _cparams(*sem):
    return pltpu.CompilerParams(dimension_semantics=sem, vmem_limit_bytes=VMEM_LIMIT)


def _tile(n, target, mult=8):
    if n <= target:
        return n
    t = (target // mult) * mult
    while t >= mult:
        if n % t == 0:
            return t
        t -= mult
    return n


def _sigmoid(x):
    return 1.0 / (1.0 + jnp.exp(-x))


def _silu(x):
    return x * _sigmoid(x)


def _log_sigmoid(x):
    return -(jnp.maximum(-x, 0.0) + jnp.log1p(jnp.exp(-jnp.abs(x))))


def _dot_nt(a, b):
    return lax.dot_general(a, b, (((1,), (1,)), ((), ())), preferred_element_type=F32)


def _dot(a, b):
    return jnp.dot(a, b, preferred_element_type=F32)


def _slope(h):
    return float(2.0 ** (-(h + 1)))


def _norm_kernel(x_ref, g_ref, o_ref):
    xf = x_ref[...]
    xn = xf * lax.rsqrt(jnp.mean(xf * xf, axis=-1, keepdims=True) + NORM_EPS)
    o_ref[...] = (xn * g_ref[...]).astype(o_ref.dtype)


def rms_norm_rows(x, g, out_dtype):
    n, d = x.shape
    tm = _tile(n, 512)
    return pl.pallas_call(
        _norm_kernel,
        grid=(n // tm,),
        in_specs=[pl.BlockSpec((tm, d), lambda i: (i, 0)), pl.BlockSpec((1, d), lambda i: (0, 0))],
        out_specs=pl.BlockSpec((tm, d), lambda i: (i, 0)),
        out_shape=jax.ShapeDtypeStruct((n, d), out_dtype),
        compiler_params=_cparams("parallel"),
        name="rms_norm",
    )(x, g.reshape(1, d))


def _mm_kernel(x_ref, w_ref, o_ref, *, scale):
    acc = _dot(x_ref[...], w_ref[...])
    if scale != 1.0:
        acc = acc * scale
    o_ref[...] = acc.astype(o_ref.dtype)


def matmul_rows(x, w, out_dtype, *, tn, scale=1.0):
    n, k = x.shape
    m = w.shape[1]
    tm = _tile(n, 2048)
    return pl.pallas_call(
        functools.partial(_mm_kernel, scale=scale),
        grid=(n // tm, m // tn),
        in_specs=[pl.BlockSpec((tm, k), lambda i, j: (i, 0)), pl.BlockSpec((k, tn), lambda i, j: (0, j))],
        out_specs=pl.BlockSpec((tm, tn), lambda i, j: (i, j)),
        out_shape=jax.ShapeDtypeStruct((n, m), out_dtype),
        compiler_params=_cparams("parallel", "parallel"),
        name="proj",
    )(x, w)


def _mm_t_kernel(wt_ref, x_ref, of_ref, ob_ref):
    acc = _dot_nt(wt_ref[...], x_ref[...])
    acc = acc.reshape(of_ref.shape)
    of_ref[...] = acc
    ob_ref[...] = acc.astype(ob_ref.dtype)


def kv_project_t(x, wt, batch, seq):
    k = x.shape[1]
    tm = _tile(seq, 512, LANES)
    nt = seq // tm
    n_slab = wt.shape[0] // KV_WIDTH
    out = pl.BlockSpec((n_slab, None, KV_WIDTH, tm), lambda b, i: (0, b, 0, i))
    return pl.pallas_call(
        _mm_t_kernel,
        grid=(batch, nt),
        in_specs=[pl.BlockSpec(wt.shape, lambda b, i: (0, 0)), pl.BlockSpec((tm, k), lambda b, i: (b * nt + i, 0))],
        out_specs=[out, out],
        out_shape=[jax.ShapeDtypeStruct((n_slab, batch, KV_WIDTH, seq), F32),
                   jax.ShapeDtypeStruct((n_slab, batch, KV_WIDTH, seq), BF16)],
        compiler_params=_cparams("parallel", "parallel"),
        name="proj_kv_t",
    )(wt, x)


def _chunk_rows(xs_ref, n_chunks):
    return jnp.concatenate([xs_ref[pl.ds(l, n_chunks, stride=CMP_STRIDE), :] for l in range(CMP_STRIDE)], axis=1)


def _halves(chunks, pea_ref, peb_ref, wa_ref, wb_ref):
    a = _dot((chunks + pea_ref[...]).astype(BF16), wa_ref[...])
    b = _dot((chunks + peb_ref[...]).astype(BF16), wb_ref[...])
    return a, b


def _summaries(a, b_next, w2bd):
    return _dot(_silu(a + b_next).astype(BF16), w2bd)


def _shift_up(x):
    n = x.shape[0]
    return pltpu.roll(x, n - 1, 0)


def _transpose_pages(src, xs_ref, n_pages):
    for p in range(n_pages):
        xs_ref[p * LANES:(p + 1) * LANES, :] = src(p).T


def _pool_ab_kernel(xk_ref, xv_ref, kpea, kpeb, kwa, kwb, vpea, vpeb, vwa, vwb, o_ref, xs_scr):
    n_pages = xk_ref.shape[0]
    hid = kwa.shape[1]
    for i, (x_ref, cw) in enumerate(((xk_ref, (kpea, kpeb, kwa, kwb)), (xv_ref, (vpea, vpeb, vwa, vwb)))):
        _transpose_pages(lambda p: x_ref[p], xs_scr, n_pages)
        a, b = _halves(_chunk_rows(xs_scr, n_pages * (LANES // CMP_STRIDE)), *cw)
        o_ref[:, (2 * i) * hid:(2 * i + 1) * hid] = a
        o_ref[:, (2 * i + 1) * hid:(2 * i + 2) * hid] = b


def pool_halves(pool_k_t, pool_v_t, layer, cw_k, cw_v):
    n_pool = pool_k_t.shape[1]
    pg = _tile(n_pool, 64, 1)
    rows = pg * (LANES // CMP_STRIDE)
    hid = cw_k[2].shape[1]
    full = lambda a: pl.BlockSpec(a.shape, lambda i: (0,) * a.ndim)
    page_spec = pl.BlockSpec((None, pg, KV_WIDTH, LANES), lambda i: (layer, i, 0, 0))
    return pl.pallas_call(
        _pool_ab_kernel,
        grid=(n_pool // pg,),
        in_specs=[page_spec, page_spec] + [full(c) for c in (*cw_k, *cw_v)],
        out_specs=pl.BlockSpec((rows, 4 * hid), lambda i: (i, 0)),
        out_shape=jax.ShapeDtypeStruct((n_pool * (LANES // CMP_STRIDE), 4 * hid), F32),
        scratch_shapes=[pltpu.VMEM((pg * LANES, KV_WIDTH), F32)],
        compiler_params=_cparams("parallel"),
        name="pool_halves",
    )(pool_k_t, pool_v_t, *cw_k, *cw_v)


def _pcmp_kernel(kt_ref, vt_ref, kpea, kpeb, kwa, kwb, kw2, vpea, vpeb, vwa, vwb, vw2, featc_ref,
                 kcf_ref, vc2_ref, xs_scr):
    seq = kt_ref.shape[1]
    n_blocks = seq // LANES
    n_chunks = seq // CMP_STRIDE
    half = KV_WIDTH // NSA_KV_HEADS
    for src_ref, (pea, peb, wa, wb, w2), is_key in ((kt_ref, (kpea, kpeb, kwa, kwb, kw2), True),
                                                   (vt_ref, (vpea, vpeb, vwa, vwb, vw2), False)):
        _transpose_pages(lambda p: src_ref[:, p * LANES:(p + 1) * LANES], xs_scr, n_blocks)
        a, b = _halves(_chunk_rows(xs_scr, n_chunks), pea, peb, wa, wb)
        rows = _summaries(a, _shift_up(b), w2[...])
        if is_key:
            rows_t = rows.T.astype(kcf_ref.dtype)
            kcf_ref[0, 0:half, :] = rows_t[0:half]
            kcf_ref[0, half:, :] = featc_ref[...]
            kcf_ref[1, 0:half, :] = featc_ref[...]
            kcf_ref[1, half:, :] = rows_t[half:]
        else:
            lane = lax.broadcasted_iota(jnp.int32, rows.shape, 1)
            vc2_ref[0] = jnp.where(lane < half, rows, 1.0).astype(vc2_ref.dtype)
            vc2_ref[1] = jnp.where(lane >= half, rows, 1.0).astype(vc2_ref.dtype)


def _feature_rows(pos):
    assert pos.max() // SEL_BLOCK < FEAT_ROWS
    f = np.zeros((FEAT_ROWS, pos.shape[0]), np.float32)
    f[1:] = pos[None, :] // SEL_BLOCK == np.arange(1, FEAT_ROWS)[:, None]
    f[0] = pos % SEL_BLOCK
    return f


def prompt_summaries(kvt_f, cw_k, w2k, cw_v, w2v):
    _, batch, _, seq = kvt_f.shape
    n_c = seq // CMP_STRIDE
    cend = np.minimum(np.arange(n_c) * CMP_STRIDE + (CMP_BLOCK - 1), seq - 1)
    consts = [*cw_k, w2k, *cw_v, w2v, jnp.asarray(_feature_rows(cend), BF16)]
    full = lambda a: pl.BlockSpec(a.shape, lambda b: (0,) * a.ndim)
    return pl.pallas_call(
        _pcmp_kernel,
        grid=(batch,),
        in_specs=[pl.BlockSpec((None, None, KV_WIDTH, seq), lambda b: (0, b, 0, 0)),
                  pl.BlockSpec((None, None, KV_WIDTH, seq), lambda b: (1, b, 0, 0))] + [full(c) for c in consts],
        out_specs=[pl.BlockSpec((None, NSA_KV_HEADS, KV_WIDTH, n_c), lambda b: (b, 0, 0, 0)),
                   pl.BlockSpec((None, NSA_KV_HEADS, n_c, KV_WIDTH), lambda b: (b, 0, 0, 0))],
        out_shape=[jax.ShapeDtypeStruct((batch, NSA_KV_HEADS, KV_WIDTH, n_c), BF16),
                   jax.ShapeDtypeStruct((batch, NSA_KV_HEADS, n_c, KV_WIDTH), BF16)],
        scratch_shapes=[pltpu.VMEM((seq, KV_WIDTH), F32)],
        compiler_params=_cparams("parallel"),
        name="prompt_summaries",
    )(kvt_f, kvt_f, *consts)


def _rank_select_t(score_t, n_cand, topn):
    idx = lax.broadcasted_iota(jnp.int32, score_t.shape, 0)
    rank = jnp.zeros(score_t.shape, jnp.int32)
    for j in range(n_cand):
        row = score_t[j:j + 1, :]
        rank = rank + jnp.where(idx > j, jnp.where(row >= score_t, 1, 0), jnp.where(row > score_t, 1, 0))
    return jnp.where(rank < topn, 1.0, 0.0)


def _pattn_kernel(q_ref, kcf_ref, vc2_ref, kst_ref, vst_ref, kwt_ref, vwt_ref, feat_ref, sm_ref, ovt_ref,
                  o_ref, kfs_scr, kfw_scr, vs_scr, vw_scr, m_scr, acc_scr, *, n_sel, topn):
    qi = pl.program_id(1)
    t0 = qi * Q_BLOCK
    half = KV_WIDTH // NSA_KV_HEADS

    @pl.when(qi == 0)
    def _():
        ones = jnp.ones((half, kst_ref.shape[1]), BF16)
        for src, dst, fill in ((kst_ref, kfs_scr, feat_ref[...]), (kwt_ref, kfw_scr, feat_ref[...]),
                               (vst_ref, vs_scr, ones), (vwt_ref, vw_scr, ones)):
            dst[0, 0:half, :] = src[0:half, :]
            dst[0, half:, :] = fill
            dst[1, 0:half, :] = fill
            dst[1, half:, :] = src[half:, :]

    q = q_ref[...]
    gates = _sigmoid(sm_ref[...])

    def gate_col(branch, h):
        j = GATE_LANE0 + branch * NSA_HEADS + h
        return gates[:, j:j + 1]

    lane = lax.broadcasted_iota(jnp.int32, (Q_BLOCK, LANES), 1)
    feat_f = jnp.where(lane >= half, lane - half, lane).astype(F32)
    sel_lanes = []

    def own_lanes(h):
        return (lane < half) if h < NSA_GROUP else (lane >= half)

    def q_operand(selected):
        parts = []
        for h in range(NSA_HEADS):
            c = jnp.where(feat_f == 0.0, _slope(h), (_slope(h) * SEL_BLOCK) * feat_f)
            if selected:
                c = c + (sel_lanes[h // NSA_GROUP] - 1.0) * MASK_BIG
            parts.append(jnp.where(own_lanes(h), q[:, h * KV_WIDTH:(h + 1) * KV_WIDTH], c.astype(BF16)))
        return jnp.concatenate(parts, axis=0)

    grp_rows = NSA_GROUP * Q_BLOCK

    def scores(qa, kf):
        return jnp.concatenate([_dot(qa[g * grp_rows:(g + 1) * grp_rows], kf(g)) for g in range(NSA_KV_HEADS)], axis=0)

    def split_sums(h, a):
        own = own_lanes(h)
        total = jnp.where(own, pltpu.roll(a, half, 1), a)
        return jnp.where(own, a / jnp.maximum(total, TINY), 0.0), total

    qa_plain = q_operand(False)

    def compressed_and_select():
        n_c = kcf_ref.shape[2]
        s_all = scores(qa_plain, lambda g: kcf_ref[g])
        tq = lax.broadcasted_iota(jnp.int32, (Q_BLOCK, n_c), 0) + t0
        cend = lax.broadcasted_iota(jnp.int32, (Q_BLOCK, n_c), 1) * CMP_STRIDE + (CMP_BLOCK - 1)
        vis_c = tq >= cend
        lane_tiles = n_c // LANES
        floor_m = jnp.full((Q_BLOCK, LANES), NEG_INF, F32)
        ps = []
        for h in range(NSA_HEADS):
            s = jnp.where(vis_c, s_all[h * Q_BLOCK:(h + 1) * Q_BLOCK], NEG_INF)
            m = jnp.maximum(floor_m, jnp.max(s, axis=-1, keepdims=True))
            ps.append(jnp.where(vis_c, jnp.exp(s - jnp.concatenate([m] * lane_tiles, axis=1)), 0.0))
        p_all = jnp.concatenate([p.astype(BF16) for p in ps], axis=0)
        pv = [_dot(p_all[g * grp_rows:(g + 1) * grp_rows], vc2_ref[g]) for g in range(NSA_KV_HEADS)]
        psum = [None] * NSA_KV_HEADS
        for h in range(NSA_HEADS):
            g = h // NSA_GROUP
            cols = slice(h * KV_WIDTH, (h + 1) * KV_WIDTH)
            o_cmp, total = split_sums(h, pv[g][(h % NSA_GROUP) * Q_BLOCK:(h % NSA_GROUP + 1) * Q_BLOCK])
            p = ps[h] / jnp.concatenate([jnp.maximum(total, TINY)] * lane_tiles, axis=1)
            psum[g] = p if psum[g] is None else psum[g] + p
            o_ref[:, cols] = o_ref[:, cols] + gate_col(0, h) * o_cmp

        s_idx = lax.broadcasted_iota(jnp.int32, (n_sel, Q_BLOCK), 0)
        tq_s = lax.broadcasted_iota(jnp.int32, (n_sel, Q_BLOCK), 1) + t0
        valid = s_idx * SEL_BLOCK <= tq_s
        forced = (s_idx == 0) | (s_idx == tq_s // SEL_BLOCK)
        for g in range(NSA_KV_HEADS):
            hi = psum[g].astype(BF16)
            lo = (psum[g] - hi.astype(F32)).astype(BF16)
            imp_t = _dot_nt(ovt_ref[...], hi) + _dot_nt(ovt_ref[...], lo)
            score_t = jnp.where(valid, imp_t + jnp.where(forced, FORCE_BONUS, 0.0), NEG_INF)
            sel_t = _rank_select_t(score_t, n_sel, topn)
            sel_rows = [sel_t] + ([jnp.zeros((half - n_sel, Q_BLOCK), F32)] if half > n_sel else [])
            other = [jnp.zeros((half, Q_BLOCK), F32)]
            sel_lanes.append(jnp.concatenate(other + sel_rows if g == 0 else sel_rows + other, axis=0).T)

    row_i = lax.broadcasted_iota(jnp.int32, (Q_BLOCK, KEY_BLOCK), 0)
    lane_i = lax.broadcasted_iota(jnp.int32, (Q_BLOCK, KEY_BLOCK), 1)

    def reset():
        m_scr[...] = jnp.full(m_scr.shape, NEG_INF, F32)
        acc_scr[...] = jnp.zeros(acc_scr.shape, F32)

    def flash_step(kb, qa, kf_ref, v_scr, masked):
        k0 = pl.multiple_of(kb * KEY_BLOCK, KEY_BLOCK)
        s_all = scores(qa, lambda g: kf_ref[g, :, pl.ds(k0, KEY_BLOCK)])
        if masked:
            dist = (t0 - k0) + (row_i - lane_i)
            keep = jnp.where(dist >= 0, dist, WINDOW) < WINDOW
        ps = []
        alphas = []
        for h in range(NSA_HEADS):
            rows = slice(h * Q_BLOCK, (h + 1) * Q_BLOCK)
            s = s_all[rows]
            if masked:
                s = jnp.where(keep, s, NEG_INF)
            m_prev = m_scr[rows]
            m_new = jnp.maximum(m_prev, jnp.max(s, axis=-1, keepdims=True))
            p = jnp.exp(s - jnp.concatenate([m_new] * (KEY_BLOCK // LANES), axis=1))
            if masked:
                p = jnp.where(keep, p, 0.0)
            alpha = jnp.exp(m_prev - m_new)
            m_scr[rows] = m_new
            ps.append(p.astype(BF16))
            alphas.append(alpha)
        p_all = jnp.concatenate(ps, axis=0)
        pv = [_dot_nt(p_all[g * grp_rows:(g + 1) * grp_rows], v_scr[g, :, pl.ds(k0, KEY_BLOCK)])
              for g in range(NSA_KV_HEADS)]
        acc_scr[...] = jnp.concatenate(alphas, axis=0) * acc_scr[...] + jnp.concatenate(pv, axis=0)

    def finish(branch, first):
        for h in range(NSA_HEADS):
            rows = slice(h * Q_BLOCK, (h + 1) * Q_BLOCK)
            cols = slice(h * KV_WIDTH, (h + 1) * KV_WIDTH)
            o_b, _ = split_sums(h, acc_scr[rows])
            gated = gate_col(branch, h) * o_b
            o_ref[:, cols] = gated if first else o_ref[:, cols] + gated

    def loop(lo, hi, qa, kf_ref, v_scr):
        n = jnp.maximum(hi - lo, 0)

        def pair(i, c):
            flash_step(lo + 2 * i, qa, kf_ref, v_scr, False)
            flash_step(lo + 2 * i + 1, qa, kf_ref, v_scr, False)
            return c

        lax.fori_loop(0, n // 2, pair, 0)

        @pl.when(n % 2 == 1)
        def _():
            flash_step(hi - 1, qa, kf_ref, v_scr, False)

    per_key_block = KEY_BLOCK // Q_BLOCK
    kb_diag = qi // per_key_block
    kb_lo = jnp.maximum(qi - WINDOW // Q_BLOCK, 0) // per_key_block

    reset()

    @pl.when(kb_lo < kb_diag)
    def _():
        flash_step(kb_lo, qa_plain, kfw_scr, vw_scr, True)

    loop(kb_lo + 1, kb_diag, qa_plain, kfw_scr, vw_scr)
    flash_step(kb_diag, qa_plain, kfw_scr, vw_scr, True)
    finish(2, True)
    compressed_and_select()
    reset()
    qa = q_operand(True)
    loop(0, kb_diag, qa, kfs_scr, vs_scr)
    flash_step(kb_diag, qa, kfs_scr, vs_scr, True)
    finish(1, False)


def prompt_attention(qp, kcf, vc2, kvt_b, r, batch, seq):
    assert seq % KEY_BLOCK == 0 and min(SEL_TOPN, seq // SEL_BLOCK) >= 2
    nqb = seq // Q_BLOCK
    n_c = kcf.shape[3]
    assert n_c % LANES == 0
    n_sel = seq // SEL_BLOCK
    topn = min(SEL_TOPN, n_sel)
    cs = np.arange(n_c)[:, None] * CMP_STRIDE
    ss = np.arange(n_sel)[None, :] * SEL_BLOCK
    ov = np.clip(np.minimum(cs + CMP_BLOCK, ss + SEL_BLOCK) - np.maximum(cs, ss), 0, None) / CMP_BLOCK
    ov[(seq - CMP_BLOCK) // CMP_STRIDE + 1:] = 0.0
    kv_spec = lambda idx: pl.BlockSpec((None, None, KV_WIDTH, seq), lambda b, i: (idx, b, 0, 0))
    return pl.pallas_call(
        functools.partial(_pattn_kernel, n_sel=n_sel, topn=topn),
        grid=(batch, nqb),
        in_specs=[
            pl.BlockSpec((Q_BLOCK, PADDED_Q), lambda b, i: (b * nqb + i, 0)),
            pl.BlockSpec((None, NSA_KV_HEADS, KV_WIDTH, n_c), lambda b, i: (b, 0, 0, 0)),
            pl.BlockSpec((None, NSA_KV_HEADS, n_c, KV_WIDTH), lambda b, i: (b, 0, 0, 0)),
            kv_spec(2), kv_spec(3), kv_spec(4), kv_spec(5),
            pl.BlockSpec((FEAT_ROWS, seq), lambda b, i: (0, 0)),
            pl.BlockSpec((Q_BLOCK, LANES), lambda b, i: (b * nqb + i, R_SMALL // LANES)),
            pl.BlockSpec((n_sel, n_c), lambda b, i: (0, 0)),
        ],
        out_specs=pl.BlockSpec((Q_BLOCK, PADDED_Q), lambda b, i: (b * nqb + i, 0)),
        out_shape=jax.ShapeDtypeStruct((batch * seq, PADDED_Q), F32),
        scratch_shapes=[
            pltpu.VMEM((NSA_KV_HEADS, KV_WIDTH, seq), BF16),
            pltpu.VMEM((NSA_KV_HEADS, KV_WIDTH, seq), BF16),
            pltpu.VMEM((NSA_KV_HEADS, KV_WIDTH, seq), BF16),
            pltpu.VMEM((NSA_KV_HEADS, KV_WIDTH, seq), BF16),
            pltpu.VMEM((NSA_HEADS * Q_BLOCK, LANES), F32),
            pltpu.VMEM((NSA_HEADS * Q_BLOCK, KV_WIDTH), F32),
        ],
        compiler_params=_cparams("parallel", "arbitrary"),
        name="prompt_nsa",
    )(qp, kcf, vc2, kvt_b, kvt_b, kvt_b, kvt_b, jnp.asarray(_feature_rows(np.arange(seq)), BF16), r,
      jnp.asarray(ov.T, BF16))


def _gla_log_decay(sm, aup_ref, ab_ref):
    z = _dot(sm.astype(BF16), aup_ref[...]) + ab_ref[...]
    return _log_sigmoid(z) / GLA_GATE_TAU


def _gla_head_norm(o, gg_ref):
    on = o * lax.rsqrt(jnp.mean(o * o, axis=-1, keepdims=True) + NORM_EPS)
    return on * gg_ref[...]


def _pgla_kernel(qg_ref, kg_ref, vg_ref, sm_ref, aup_ref, ab_ref, gg_ref, og_ref, sfin_ref, st_scr, *, n_tiles):
    ti = pl.program_id(1)
    tt = qg_ref.shape[0]
    width = qg_ref.shape[1]

    @pl.when(ti == 0)
    def _():
        st_scr[...] = jnp.zeros(st_scr.shape, F32)

    la = _gla_log_decay(sm_ref[...], aup_ref, ab_ref)
    rin = lax.broadcasted_iota(jnp.int32, la.shape, 0)
    bg = la
    sh = 1
    while sh < tt:
        bg = bg + jnp.where(rin >= sh, pltpu.roll(bg, sh, 0), 0.0)
        sh *= 2

    def row_of_block(x, block, row):
        x3 = x.reshape(tt // block, block, width)
        return jnp.broadcast_to(x3[:, row:row + 1, :], x3.shape).reshape(tt, width)

    n_chunks = tt // GLA_CHUNK
    ends = bg.reshape(n_chunks, GLA_CHUNK, width)[:, GLA_CHUNK - 1:GLA_CHUNK, :]
    prev_end = jnp.concatenate([jnp.zeros((1, 1, width), F32), ends[:n_chunks - 1]], axis=0)
    b = (bg.reshape(n_chunks, GLA_CHUNK, width) - prev_end).reshape(tt, width)
    bg_end = bg[tt - 1:tt, :]
    scales = [(jnp.exp(b), jnp.exp(-b))]
    block = 2 * GLA_CHUNK
    while block <= tt:
        mid = row_of_block(bg, block, block // 2 - 1)
        scales.append((jnp.exp(jnp.minimum(bg - mid, 0.0)), jnp.exp(jnp.minimum(mid - bg, 0.0))))
        block *= 2
    e_in = jnp.exp(bg)
    e_out = jnp.exp(bg_end - bg)

    r_i = lax.broadcasted_iota(jnp.int32, (tt, tt), 0)
    c_i = lax.broadcasted_iota(jnp.int32, (tt, tt), 1)
    level = jnp.full((tt, tt), len(scales) - 1, jnp.int32)
    block = tt // 2
    lv = len(scales) - 2
    while block >= GLA_CHUNK:
        level = jnp.where(r_i // block == c_i // block, lv, level)
        block //= 2
        lv -= 1
    level = jnp.where(c_i <= r_i, level, -1)

    for h in range(GLA_HEADS):
        ks = slice(h * GLA_DK, (h + 1) * GLA_DK)
        vs = slice(h * GLA_DV, (h + 1) * GLA_DV)
        q = qg_ref[:, ks] * (GLA_DK ** -0.5)
        k = kg_ref[:, ks]
        v = vg_ref[:, vs].astype(BF16)
        a = jnp.zeros((tt, tt), F32)
        for lv, (sq, sk) in enumerate(scales):
            a_lv = _dot_nt((q * sq[:, ks]).astype(BF16), (k * sk[:, ks]).astype(BF16))
            a = jnp.where(level == lv, a_lv, a)
        st = st_scr[h]
        o = _dot(a.astype(BF16), v) + _dot_nt((q * e_in[:, ks]).astype(BF16), st.astype(BF16))
        kv_t = lax.dot_general(v, (k * e_out[:, ks]).astype(BF16), (((0,), (0,)), ((), ())),
                               preferred_element_type=F32)
        st_scr[h] = st * jnp.exp(bg_end[:, ks]) + kv_t
        og_ref[:, vs] = _gla_head_norm(o, gg_ref)

    @pl.when(ti == n_tiles - 1)
    def _():
        for h in range(GLA_HEADS):
            sfin_ref[h] = st_scr[h].T


def prompt_gla(r, aup, ab, gg, batch, seq):
    tt = _tile(seq, 256, GLA_CHUNK)
    nt = seq // tt
    row = lambda b, i: b * nt + i
    return pl.pallas_call(
        functools.partial(_pgla_kernel, n_tiles=nt),
        grid=(batch, nt),
        in_specs=[
            pl.BlockSpec((tt, GLA_KEY_WIDTH), lambda b, i: (row(b, i), R_QG // GLA_KEY_WIDTH)),
            pl.BlockSpec((tt, GLA_KEY_WIDTH), lambda b, i: (row(b, i), R_KG // GLA_KEY_WIDTH)),
            pl.BlockSpec((tt, GLA_VAL_WIDTH), lambda b, i: (row(b, i), R_VG // GLA_VAL_WIDTH)),
            pl.BlockSpec((tt, LANES), lambda b, i: (row(b, i), R_SMALL // LANES)),
            pl.BlockSpec(aup.shape, lambda b, i: (0, 0)),
            pl.BlockSpec(ab.shape, lambda b, i: (0, 0)),
            pl.BlockSpec(gg.shape, lambda b, i: (0, 0)),
        ],
        out_specs=[
            pl.BlockSpec((tt, GLA_VAL_WIDTH), lambda b, i: (row(b, i), 0)),
            pl.BlockSpec((None, GLA_HEADS, GLA_DK, GLA_DV), lambda b, i: (b, 0, 0, 0)),
        ],
        out_shape=[
            jax.ShapeDtypeStruct((batch * seq, GLA_VAL_WIDTH), F32),
            jax.ShapeDtypeStruct((batch, GLA_HEADS, GLA_DK, GLA_DV), F32),
        ],
        scratch_shapes=[pltpu.VMEM((GLA_HEADS, GLA_DV, GLA_DK), F32)],
        compiler_params=_cparams("parallel", "arbitrary"),
        name="prompt_gla",
    )(r, r, r, r, aup, ab, gg)


def _sgla_kernel(qg_ref, kg_ref, vg_ref, sm_ref, aup_ref, ab_ref, gg_ref, s_ref, og_ref, snew_ref, *, ds):
    rows = qg_ref.shape[0]
    nb = rows // ds
    la = _gla_log_decay(sm_ref[...], aup_ref, ab_ref)
    ri = lax.broadcasted_iota(jnp.int32, la.shape, 0) % ds
    b = la
    sh = 1
    while sh < ds:
        b = b + jnp.where(ri >= sh, pltpu.roll(b, sh, 0), 0.0)
        sh *= 2
    b_last = b
    for d in range(1, ds):
        b_last = jnp.where(ri == ds - 1 - d, pltpu.roll(b, rows - d, 0), b_last)
    e_b = jnp.exp(b)
    e_nb = jnp.exp(-b)
    e_tail = jnp.exp(b_last - b)
    e_last = jnp.exp(b_last)

    r_i = lax.broadcasted_iota(jnp.int32, (rows, rows), 0)
    c_i = lax.broadcasted_iota(jnp.int32, (rows, rows), 1)
    causal = (r_i // ds == c_i // ds) & (c_i <= r_i)
    row_b = lax.broadcasted_iota(jnp.int32, (rows, GLA_DV), 0) // ds

    for h in range(GLA_HEADS):
        ks = slice(h * GLA_DK, (h + 1) * GLA_DK)
        vs = slice(h * GLA_DV, (h + 1) * GLA_DV)
        q = qg_ref[:, ks] * (GLA_DK ** -0.5)
        k = kg_ref[:, ks]
        v = vg_ref[:, vs]
        q_dec = (q * e_b[:, ks]).astype(BF16)
        k_inv = (k * e_nb[:, ks]).astype(BF16)
        k_tail_t = (k * e_tail[:, ks]).T
        e_last_t = e_last[:, ks].T
        a = jnp.where(causal, _dot_nt(q_dec, k_inv), 0.0)
        o = _dot(a.astype(BF16), v.astype(BF16))
        for bb in range(nb):
            s_prev = s_ref[bb, h]
            o_inter = _dot(q_dec, s_prev.astype(BF16))
            o = o + jnp.where(row_b == bb, o_inter, 0.0)
            s_new = e_last_t[:, bb * ds:bb * ds + 1] * s_prev
            for j in range(ds):
                rr = bb * ds + j
                s_new = s_new + k_tail_t[:, rr:rr + 1] * v[rr:rr + 1, :]
            snew_ref[bb, h] = s_new
        og_ref[:, vs] = _gla_head_norm(o, gg_ref)


def sample_gla(r, aup, ab, gg, state, layer, dec_batch, ds):
    nb = _tile(dec_batch, 8, 1)
    rows = nb * ds
    return pl.pallas_call(
        functools.partial(_sgla_kernel, ds=ds),
        grid=(dec_batch // nb,),
        in_specs=[
            pl.BlockSpec((rows, GLA_KEY_WIDTH), lambda i: (i, R_QG // GLA_KEY_WIDTH)),
            pl.BlockSpec((rows, GLA_KEY_WIDTH), lambda i: (i, R_KG // GLA_KEY_WIDTH)),
            pl.BlockSpec((rows, GLA_VAL_WIDTH), lambda i: (i, R_VG // GLA_VAL_WIDTH)),
            pl.BlockSpec((rows, LANES), lambda i: (i, R_SMALL // LANES)),
            pl.BlockSpec(aup.shape, lambda i: (0, 0)),
            pl.BlockSpec(ab.shape, lambda i: (0, 0)),
            pl.BlockSpec(gg.shape, lambda i: (0, 0)),
            pl.BlockSpec((None, nb, GLA_HEADS, GLA_DK, GLA_DV), lambda i: (layer, i, 0, 0, 0)),
        ],
        out_specs=[
            pl.BlockSpec((rows, GLA_VAL_WIDTH), lambda i: (i, 0)),
            pl.BlockSpec((nb, GLA_HEADS, GLA_DK, GLA_DV), lambda i: (i, 0, 0, 0)),
        ],
        out_shape=[
            jax.ShapeDtypeStruct((dec_batch * ds, GLA_VAL_WIDTH), F32),
            jax.ShapeDtypeStruct((dec_batch, GLA_HEADS, GLA_DK, GLA_DV), F32),
        ],
        compiler_params=_cparams("parallel"),
        name="sample_gla",
    )(r, r, r, r, aup, ab, gg, state)


def _mix_kernel(x_ref, on_ref, za_ref, og_ref, zb_ref, mg_ref, wba_ref, wbb_ref, wo_ref, fg_ref, *o_refs):
    pa = _dot((on_ref[...] * _silu(za_ref[...])).astype(BF16), wba_ref[...])
    pb = _dot((og_ref[...] * _silu(zb_ref[...])).astype(BF16), wbb_ref[...])
    mix = _sigmoid(mg_ref[:, :D_MODEL]) * pa + _sigmoid(mg_ref[:, D_MODEL:]) * pb
    y = x_ref[...] + _dot(mix.astype(BF16), wo_ref[...])
    o_refs[0][...] = y
    if len(o_refs) > 1:
        yn = y * lax.rsqrt(jnp.mean(y * y, axis=-1, keepdims=True) + NORM_EPS)
        o_refs[1][...] = yn * fg_ref[...]


def mixer_output(x, o_nsa, r, o_gla, wba, wbb, wo, final_g, with_final_norm):
    n = x.shape[0]
    tm = _tile(n, 256)
    row = lambda w: pl.BlockSpec((tm, w), lambda i: (i, 0))
    wspec = lambda w: pl.BlockSpec(w.shape, lambda i: (0, 0))
    n_out = 2 if with_final_norm else 1
    return pl.pallas_call(
        _mix_kernel,
        grid=(n // tm,),
        in_specs=[
            row(D_MODEL), row(PADDED_Q),
            pl.BlockSpec((tm, PADDED_Q), lambda i: (i, R_ZA // PADDED_Q)),
            row(GLA_VAL_WIDTH),
            pl.BlockSpec((tm, GLA_VAL_WIDTH), lambda i: (i, R_ZB // GLA_VAL_WIDTH)),
            pl.BlockSpec((tm, 2 * D_MODEL), lambda i: (i, R_MG // (2 * D_MODEL))),
            wspec(wba), wspec(wbb), wspec(wo), wspec(final_g),
        ],
        out_specs=[row(D_MODEL)] * n_out,
        out_shape=[jax.ShapeDtypeStruct((n, D_MODEL), F32)] * n_out,
        compiler_params=_cparams("parallel"),
        name="mixer_output",
    )(x, o_nsa, r, o_gla, r, r, wba, wbb, wo, final_g)


def _select_blocks(score, lane_idx, n_cand, topn):
    rank = jnp.zeros(score.shape, jnp.int32)
    for j in range(n_cand):
        col = score[:, j:j + 1]
        rank = rank + jnp.where(col > score, 1, 0) + jnp.where(col == score, jnp.where(lane_idx > j, 1, 0), 0)
    return jnp.where(rank < topn, 1.0, 0.0)


def _softmax_two(parts, keeps):
    m = functools.reduce(jnp.maximum, [jnp.max(s, axis=-1, keepdims=True) for s in parts])
    ps = [jnp.where(kp, jnp.exp(s - m), 0.0) for s, kp in zip(parts, keeps)]
    total = functools.reduce(lambda a, c: a + c, [jnp.sum(p, axis=-1, keepdims=True) for p in ps])
    return ps, jnp.maximum(total, TINY)


def _smain_kernel(pt_ref, q_ref, gate_ref, slope_ref, ab_hbm, kt_hbm, vt_hbm, kw_ref, vw_ref, new_ref,
                  w2k_ref, w2v_ref, ov_ref, exp_ref, exn_ref, o_ref, kw_out_ref, vw_out_ref,
                  ab_buf, kt_buf, vt_buf, sems, *, layer, n_pages, past, ds, n_sel, topn):
    step = pl.program_id(0)
    buf_slot = step % 2
    rows_per_step = q_ref.shape[0]

    def page_copies(st, sl):
        cps = []
        for r in range(rows_per_step):
            for j in range(n_pages):
                pg = pt_ref[st * rows_per_step + r, j]
                cps.append(pltpu.make_async_copy(ab_hbm.at[pg], ab_buf.at[sl, r, j], sems.at[sl, 0]))
                cps.append(pltpu.make_async_copy(kt_hbm.at[layer, pg], kt_buf.at[sl, r, j], sems.at[sl, 1]))
                cps.append(pltpu.make_async_copy(vt_hbm.at[layer, pg], vt_buf.at[sl, r, j], sems.at[sl, 2]))
        return cps

    @pl.when(step == 0)
    def _():
        for cp in page_copies(0, 0):
            cp.start()

    @pl.when(step + 1 < pl.num_programs(0))
    def _():
        for cp in page_copies(step + 1, 1 - buf_slot):
            cp.start()

    for cp in page_copies(step, buf_slot):
        cp.wait()

    active = [_sample_row(step * rows_per_step + r, q_ref.at[r], gate_ref.at[r], slope_ref,
                          ab_buf.at[buf_slot, r], kt_buf.at[buf_slot, r], vt_buf.at[buf_slot, r],
                          kw_ref.at[r], vw_ref.at[r], new_ref, w2k_ref, w2v_ref, ov_ref, exp_ref, exn_ref,
                          o_ref.at[r], kw_out_ref.at[r], vw_out_ref.at[r],
                          n_pages=n_pages, past=past, ds=ds, n_sel=n_sel, topn=topn)
              for r in range(rows_per_step)]
    while active:
        active = [row for row in active if next(row, True) is None]


def _sample_row(b, q_ref, gate_ref, slope_ref, ab_ref, kt_ref, vt_ref, kw_ref, vw_ref, new_ref,
                w2k_ref, w2v_ref, ov_ref, exp_ref, exn_ref, o_ref, kw_out_ref, vw_out_ref,
                *, n_pages, past, ds, n_sel, topn):
    rows = q_ref.shape[0]
    grp_rows = NSA_KV_HEADS * ds
    n_new = new_ref.shape[2]
    b_loc = b % (n_new // ds)
    kt_refs = [kt_ref.at[j] for j in range(n_pages)]
    vt_refs = [vt_ref.at[j] for j in range(n_pages)]

    qs = q_ref[...]
    slope = slope_ref[:, 0:1]
    ab = jnp.concatenate([ab_ref[j] for j in range(n_pages)], axis=0)
    hid = ab.shape[1] // 4
    kc = _summaries(ab[:, 0:hid], _shift_up(ab[:, hid:2 * hid]), w2k_ref[...]).astype(BF16)
    vc = _summaries(ab[:, 2 * hid:3 * hid], _shift_up(ab[:, 3 * hid:]), w2v_ref[...]).astype(BF16)
    n_c = kc.shape[0]

    def irow(shape):
        return lax.broadcasted_iota(jnp.int32, shape, 0) % ds

    yield
    dist_c = (past + irow((rows, n_c))) - (lax.broadcasted_iota(jnp.int32, (rows, n_c), 1) * CMP_STRIDE
                                          + (CMP_BLOCK - 1))
    vis_c = dist_c >= 0
    s_c = jnp.where(vis_c, _dot_nt(qs, kc) - slope * dist_c.astype(F32), NEG_INF)
    m_c = jnp.max(s_c, axis=-1, keepdims=True)
    p_c = jnp.where(vis_c, jnp.exp(s_c - m_c), 0.0)
    p_c = p_c / jnp.maximum(jnp.sum(p_c, axis=-1, keepdims=True), TINY)
    o_cmp = _dot(p_c.astype(BF16), vc)

    yield
    hi = p_c.astype(BF16)
    lo = (p_c - hi.astype(F32)).astype(BF16)
    imp_h = _dot(hi, ov_ref[...]) + _dot(lo, ov_ref[...])
    imp = functools.reduce(lambda a, c: a + c,
                           [imp_h[r * grp_rows:(r + 1) * grp_rows] for r in range(NSA_GROUP)])
    s_idx = lax.broadcasted_iota(jnp.int32, imp.shape, 1)
    qpos = past + irow(imp.shape)
    valid = (s_idx < n_sel) & (s_idx * SEL_BLOCK <= qpos)
    forced = (s_idx == 0) | (s_idx == qpos // SEL_BLOCK)
    score = jnp.where(valid, imp + jnp.where(forced, FORCE_BONUS, 0.0), NEG_INF)
    sel = _select_blocks(score, s_idx, n_sel, topn)
    sel = jnp.concatenate([sel] * NSA_GROUP, axis=0).astype(BF16)
    keep_past = _dot(sel, exp_ref[...]) > 0.5
    sel_new = _dot(sel, exn_ref[...]) > 0.5

    yield
    new_lane = lax.broadcasted_iota(jnp.int32, (rows, n_new), 1)
    dist_n = irow((rows, n_new)) - new_lane % ds
    mine = (new_lane // ds == b_loc) & (dist_n >= 0)
    bias_n = slope * dist_n.astype(F32)

    s_p = jnp.concatenate([_dot(qs, r[...].astype(BF16)) for r in kt_refs], axis=1)
    dist_p = (past + irow(s_p.shape)) - lax.broadcasted_iota(jnp.int32, s_p.shape, 1)
    s_p = jnp.where(keep_past, s_p - slope * dist_p.astype(F32), NEG_INF)
    keep_n = mine & sel_new
    s_n = jnp.where(keep_n, _dot(qs, new_ref[2].astype(BF16)) - bias_n, NEG_INF)
    (p_p, p_n), total = _softmax_two([s_p, s_n], [keep_past, keep_n])
    acc = _dot_nt(p_n.astype(BF16), new_ref[3].astype(BF16))
    for j, r in enumerate(vt_refs):
        acc = acc + _dot_nt(p_p[:, j * LANES:(j + 1) * LANES].astype(BF16), r[...].astype(BF16))
    o_slc = acc / total

    yield
    kbuf = kw_ref[...]
    vbuf = vw_ref[...]
    w_buf = kbuf.shape[1]
    dist_w = (w_buf + irow((rows, w_buf))) - lax.broadcasted_iota(jnp.int32, (rows, w_buf), 1)
    keep_w = dist_w < WINDOW
    s_w = jnp.where(keep_w, _dot(qs, kbuf.astype(BF16)) - slope * dist_w.astype(F32), NEG_INF)
    s_wn = jnp.where(mine, _dot(qs, new_ref[4].astype(BF16)) - bias_n, NEG_INF)
    (p_w, p_wn), total_w = _softmax_two([s_w, s_wn], [keep_w, mine])
    o_win = (_dot_nt(p_w.astype(BF16), vbuf.astype(BF16)) + _dot_nt(p_wn.astype(BF16), new_ref[5].astype(BF16))) / total_w

    gates = _sigmoid(gate_ref[...])
    o_ref[...] = gates[:, 0:1] * o_cmp + gates[:, 1:2] * o_slc + gates[:, 2:3] * o_win

    yield
    lane_new = lax.broadcasted_iota(jnp.int32, (KV_WIDTH, n_new), 1)
    shift = (n_new - ds) - b_loc * ds
    for buf, slab, out_ref in ((kbuf, 4, kw_out_ref), (vbuf, 5, vw_out_ref)):
        rolled = pltpu.roll(buf, w_buf - ds, 1)
        own_last = pltpu.roll(new_ref[slab], shift, 1)
        out_ref[:, 0:w_buf - n_new] = rolled[:, 0:w_buf - n_new]
        out_ref[:, w_buf - n_new:w_buf] = jnp.where(lane_new >= n_new - ds, own_last, rolled[:, w_buf - n_new:w_buf])


def sample_attention(page_table, q_rows, gate_rows, ab_pages, slc_pools_t, win_bufs_t, new_t, w2k, w2v, layer, past, ds):
    assert past % SEL_BLOCK == 0 and past % LANES == 0
    dec_batch, n_pages = page_table.shape
    rows = q_rows.shape[1]
    w_buf = win_bufs_t[0].shape[3]
    n_new = min(LANES, dec_batch * ds)
    assert (dec_batch * ds) % n_new == 0 and n_new % ds == 0 and w_buf >= n_new
    n_c = n_pages * (LANES // CMP_STRIDE)
    n_cmp = (past + ds - CMP_BLOCK) // CMP_STRIDE + 1
    n_sel = -(-(past + ds) // SEL_BLOCK)
    topn = min(SEL_TOPN, n_sel)
    assert n_sel <= LANES and ds <= CMP_STRIDE
    cs = np.arange(n_c)[:, None] * CMP_STRIDE
    ss = np.arange(LANES)[None, :] * SEL_BLOCK
    ov = np.clip(np.minimum(cs + CMP_BLOCK, ss + SEL_BLOCK) - np.maximum(cs, ss), 0, None) / CMP_BLOCK
    ov[n_cmp:] = 0.0
    ov[:, n_sel:] = 0.0
    ex_past = (np.arange(past)[None, :] // SEL_BLOCK == np.arange(LANES)[:, None]).astype(np.float32)
    ex_new = ((past + np.arange(n_new) % ds)[None, :] // SEL_BLOCK == np.arange(LANES)[:, None]).astype(np.float32)
    head = np.arange(rows) // (NSA_KV_HEADS * ds) + NSA_GROUP * ((np.arange(rows) // ds) % NSA_KV_HEADS)
    slope = np.broadcast_to((2.0 ** -(head + 1.0))[:, None], (rows, LANES)).astype(np.float32)

    rps = max(r for r in (2, 1) if dec_batch % r == 0 and (n_new // ds) % r == 0)
    const = lambda shape: pl.BlockSpec(shape, lambda s, pt: (0,) * len(shape))
    in_specs = [pl.BlockSpec((rps, rows, LANES), lambda s, pt: (s, 0, 0)),
                pl.BlockSpec((rps, rows, LANES), lambda s, pt: (s, 0, 0)),
                const((rows, LANES))]
    operands = [q_rows, gate_rows, jnp.asarray(slope)]
    in_specs += [pl.BlockSpec(memory_space=pl.ANY)] * 3
    operands += [ab_pages, slc_pools_t[0], slc_pools_t[1]]
    for arr in win_bufs_t:
        in_specs.append(pl.BlockSpec((None, rps, KV_WIDTH, w_buf), lambda s, pt: (layer, s, 0, 0)))
        operands.append(arr)
    steps_per_tile = n_new // ds // rps
    in_specs.append(pl.BlockSpec((6, None, KV_WIDTH, n_new), lambda s, pt: (0, 0, 0, s // steps_per_tile)))
    operands.append(new_t)
    consts = [w2k, w2v, jnp.asarray(ov, BF16), jnp.asarray(ex_past, BF16), jnp.asarray(ex_new, BF16)]
    in_specs += [const(c.shape) for c in consts]
    operands += consts
    buf_out = pl.BlockSpec((rps, KV_WIDTH, w_buf), lambda s, pt: (s, 0, 0))
    return pl.pallas_call(
        functools.partial(_smain_kernel, layer=layer, n_pages=n_pages, past=past, ds=ds, n_sel=n_sel, topn=topn),
        grid_spec=pltpu.PrefetchScalarGridSpec(
            num_scalar_prefetch=1,
            grid=(dec_batch // rps,),
            in_specs=in_specs,
            out_specs=[pl.BlockSpec((rps, rows, LANES), lambda s, pt: (s, 0, 0)), buf_out, buf_out],
            scratch_shapes=[pltpu.VMEM((2, rps, n_pages) + ab_pages.shape[1:], F32),
                            pltpu.VMEM((2, rps, n_pages, KV_WIDTH, LANES), F32),
                            pltpu.VMEM((2, rps, n_pages, KV_WIDTH, LANES), F32),
                            pltpu.SemaphoreType.DMA((2, 3))],
        ),
        out_shape=[jax.ShapeDtypeStruct((dec_batch, rows, LANES), F32),
                   jax.ShapeDtypeStruct((dec_batch, KV_WIDTH, w_buf), F32),
                   jax.ShapeDtypeStruct((dec_batch, KV_WIDTH, w_buf), F32)],
        compiler_params=_cparams("arbitrary"),
        name="sample_nsa",
    )(page_table, *operands)


def _pad_heads_cols(w):
    k = w.shape[0]
    w4 = w.reshape(k, NSA_KV_HEADS, NSA_GROUP, HEAD_DIM)
    eye = jnp.eye(NSA_KV_HEADS, dtype=w.dtype)
    return jnp.einsum("kgrd,gp->kgrpd", w4, eye).reshape(k, PADDED_Q)


def _layer_weights(w_in, pk_pe, pk_w1, pk_w2, pv_pe, pv_w1, pv_w2, a_up, a_b, gla_g, w_ba, w_bb, w_out):
    o = _OFF
    col = lambda i: w_in[:, o[i]:o[i + 1]]
    small = jnp.concatenate([col(2), col(7), jnp.zeros((D_MODEL, LANES - 3 * NSA_HEADS - GLA_GATE_RANK), F32)], axis=1)
    w_q = _pad_heads_cols(col(0)).astype(BF16)
    w_kv_t = col(1).T.astype(BF16)
    w_r = jnp.concatenate([_pad_heads_cols(col(3)), col(4), col(5), col(6), col(8), col(9), small], axis=1).astype(BF16)
    half = CMP_BLOCK // 2

    def cmp_weights(pe, w1, w2):
        eye = jnp.eye(NSA_KV_HEADS, dtype=F32)
        w1h = w1.reshape(2, half, HEAD_DIM, CMP_HIDDEN)
        big = lambda w: jnp.einsum("ldh,gk->lgdkh", w, eye).reshape(half * KV_WIDTH, NSA_KV_HEADS * CMP_HIDDEN)
        peh = pe.reshape(2, half, 1, HEAD_DIM)
        pe_row = lambda p: jnp.broadcast_to(p, (half, NSA_KV_HEADS, HEAD_DIM)).reshape(1, half * KV_WIDTH)
        w2bd = jnp.einsum("hd,gk->ghkd", w2, eye).reshape(NSA_KV_HEADS * CMP_HIDDEN, KV_WIDTH)
        return (pe_row(peh[0]), pe_row(peh[1]), big(w1h[0]).astype(BF16), big(w1h[1]).astype(BF16)), w2bd.astype(BF16)

    cw_k, w2k = cmp_weights(pk_pe, pk_w1, pk_w2)
    cw_v, w2v = cmp_weights(pv_pe, pv_w1, pv_w2)
    aup = jnp.zeros((LANES, GLA_KEY_WIDTH), F32).at[ALOW_LANE0:ALOW_LANE0 + GLA_GATE_RANK].set(a_up).astype(BF16)
    w_ba_p = _pad_heads_cols(w_ba.T).T.astype(BF16)
    return dict(w_q=w_q, w_kv_t=w_kv_t, w_r=w_r, cw_k=cw_k, cw_v=cw_v, w2k=w2k, w2v=w2v, aup=aup,
                ab=a_b.reshape(1, GLA_KEY_WIDTH), gg=gla_g.reshape(1, GLA_DV),
                w_ba=w_ba_p, w_bb=w_bb.astype(BF16), w_out=w_out.astype(BF16))


def _project(x, g_norm, lw, batch, seq):
    h = rms_norm_rows(x, g_norm, BF16)
    qp = matmul_rows(h, lw["w_q"], BF16, tn=512, scale=HEAD_DIM ** -0.5)
    kvt_f, kvt_b = kv_project_t(h, lw["w_kv_t"], batch, seq)
    r = matmul_rows(h, lw["w_r"], F32, tn=R_WIDTH // 7)
    return qp, kvt_f, kvt_b, r


def _tokens_minor(cache):
    lead = cache.shape[:-3]
    n = cache.ndim
    perm = tuple(range(n - 3)) + (n - 2, n - 1, n - 3)
    return cache.transpose(perm).reshape(lead + (KV_WIDTH, cache.shape[-3]))


def _tokens_major(x_t):
    lead = x_t.shape[:-2]
    n = len(lead)
    x5 = x_t.reshape(lead + (NSA_KV_HEADS, HEAD_DIM, x_t.shape[-1]))
    return x5.transpose(tuple(range(n)) + (n + 2, n, n + 1))


def kernel(x_prompt, x_sample, cache_k_cmp, cache_v_cmp, cache_k_slc, cache_v_slc, cache_k_win, cache_v_win, state_gla, page_table, norm_g, w_in, phi_k_pe, phi_k_w1, phi_k_w2, phi_v_pe, phi_v_w1, phi_v_w2, gla_alpha_up, gla_alpha_b, gla_norm_g, w_branch_a, w_branch_b, w_out, final_norm_g):
    batch, seq, _ = x_prompt.shape
    dec_batch, ds, _ = x_sample.shape
    depth = norm_g.shape[0]
    n_pool, page_size = cache_k_cmp.shape[1:3]
    assert page_size == LANES
    n_pages = page_table.shape[1]
    past = n_pages * page_size
    chunks_per_page = page_size // CMP_STRIDE
    final_g = final_norm_g.reshape(1, D_MODEL)

    cmp_pools_t = (_tokens_minor(cache_k_cmp), _tokens_minor(cache_v_cmp))
    slc_pools_t = (_tokens_minor(cache_k_slc), _tokens_minor(cache_v_slc))
    win_bufs_t = (_tokens_minor(cache_k_win), _tokens_minor(cache_v_win))

    y_p = x_prompt.reshape(batch * seq, D_MODEL)
    y_s = x_sample.reshape(dec_batch * ds, D_MODEL)
    outs_p, outs_s = [], []
    for l in range(depth):
        lw = _layer_weights(w_in[l], phi_k_pe[l], phi_k_w1[l], phi_k_w2[l], phi_v_pe[l], phi_v_w1[l], phi_v_w2[l],
                            gla_alpha_up[l], gla_alpha_b[l], gla_norm_g[l], w_branch_a[l], w_branch_b[l], w_out[l])
        last = l == depth - 1

        qp, kvt_f, kvt_b, r = _project(y_p, norm_g[l], lw, batch, seq)
        kcf, vc2 = prompt_summaries(kvt_f, lw["cw_k"], lw["w2k"], lw["cw_v"], lw["w2v"])
        o_nsa = prompt_attention(qp, kcf, vc2, kvt_b, r, batch, seq)
        o_gla, s_gla = prompt_gla(r, lw["aup"], lw["ab"], lw["gg"], batch, seq)
        res = mixer_output(y_p, o_nsa, r, o_gla, lw["w_ba"], lw["w_bb"], lw["w_out"], final_g, last)
        y_p = res[0]
        if last:
            y_p_out = res[1]
        keep = min(WINDOW, seq)
        kv5 = _tokens_major(kvt_f)
        outs_p.append((kv5[0], kv5[1], kv5[2], kv5[3], kv5[4][:, seq - keep:], kv5[5][:, seq - keep:], s_gla))

        qs, kvs_t, _, rs = _project(y_s, norm_g[l], lw, 1, dec_batch * ds)
        ab_pages = pool_halves(cmp_pools_t[0], cmp_pools_t[1], l, lw["cw_k"], lw["cw_v"])
        ab_pages = ab_pages.reshape(n_pool, chunks_per_page, -1)
        q_rows = qs.reshape(dec_batch, ds, NSA_KV_HEADS, NSA_GROUP, KV_WIDTH).transpose(0, 3, 2, 1, 4)
        q_rows = q_rows.reshape(dec_batch, NSA_HEADS * ds, KV_WIDTH)
        gl = rs[:, R_SMALL + GATE_LANE0:R_SMALL + GATE_LANE0 + 3 * NSA_HEADS]
        gl = gl.reshape(dec_batch, ds, 3, NSA_KV_HEADS, NSA_GROUP).transpose(0, 4, 3, 1, 2)
        gate_rows = jnp.pad(gl.reshape(dec_batch, NSA_HEADS * ds, 3), ((0, 0), (0, 0), (0, LANES - 3)))
        o_rows, kw_new, vw_new = sample_attention(page_table, q_rows, gate_rows, ab_pages, slc_pools_t, win_bufs_t,
                                                  kvs_t, lw["w2k"], lw["w2v"], l, past, ds)
        o_nsa_s = o_rows.reshape(dec_batch, NSA_GROUP, NSA_KV_HEADS, ds, KV_WIDTH).transpose(0, 3, 2, 1, 4)
        o_nsa_s = o_nsa_s.reshape(dec_batch * ds, PADDED_Q)
        o_gla_s, s_gla_s = sample_gla(rs, lw["aup"], lw["ab"], lw["gg"], state_gla, l, dec_batch, ds)
        res = mixer_output(y_s, o_nsa_s, rs, o_gla_s, lw["w_ba"], lw["w_bb"], lw["w_out"], final_g, last)
        y_s = res[0]
        if last:
            y_s_out = res[1]
        kvs5 = _tokens_major(kvs_t[:, 0]).reshape(6, dec_batch, ds, NSA_KV_HEADS, HEAD_DIM)
        outs_s.append((kvs5[0], kvs5[1], kvs5[2], kvs5[3], _tokens_major(kw_new), _tokens_major(vw_new), s_gla_s))

    stack = lambda outs: [jnp.stack(t) for t in zip(*outs)]
    return (y_p_out.reshape(batch, seq, D_MODEL), y_s_out.reshape(dec_batch, ds, D_MODEL),
            *stack(outs_p), *stack(outs_s))
```

```python
import functools

import numpy as np
import jax
import jax.numpy as jnp
from jax import lax
from jax.experimental import pallas as pl
from jax.experimental.pallas import tpu as pltpu

F32 = jnp.float32
BF16 = jnp.bfloat16

D_MODEL = 1024
NSA_HEADS = 8
NSA_KV_HEADS = 2
NSA_GROUP = NSA_HEADS // NSA_KV_HEADS
HEAD_DIM = 64
NSA_WIDTH = NSA_HEADS * HEAD_DIM
KV_WIDTH = NSA_KV_HEADS * HEAD_DIM
CMP_BLOCK = 32
CMP_STRIDE = 16
CMP_HIDDEN = 2 * HEAD_DIM
SEL_BLOCK = 64
SEL_TOPN = 16
WINDOW = 512
GLA_HEADS = 4
GLA_KEY_WIDTH = D_MODEL // 2
GLA_VAL_WIDTH = D_MODEL
GLA_DK = GLA_KEY_WIDTH // GLA_HEADS
GLA_DV = GLA_VAL_WIDTH // GLA_HEADS
GLA_GATE_RANK = 16
GLA_GATE_TAU = 16.0
GLA_CHUNK = 32
Q_BLOCK = 128
KEY_BLOCK = 256
NORM_EPS = 1e-6
NEG_INF = -1e30
TINY = 1e-30
FORCE_BONUS = 1e4
MASK_BIG = float(2.0 ** 100)
PADDED_Q = NSA_HEADS * KV_WIDTH
LANES = 128

_IN_SIZES = (NSA_WIDTH, 6 * KV_WIDTH, 3 * NSA_HEADS, NSA_WIDTH, GLA_KEY_WIDTH, GLA_KEY_WIDTH,
             GLA_VAL_WIDTH, GLA_GATE_RANK, GLA_VAL_WIDTH, 2 * D_MODEL)
_OFF = tuple(int(o) for o in np.cumsum((0,) + _IN_SIZES))

R_QG, R_KG, R_VG, R_SMALL = 0, 512, 1024, 2048
RG_WIDTH = R_SMALL + LANES
Z_ZA, Z_ZB, Z_MG = 0, 1024, 2048
RZ_WIDTH = Z_MG + 2 * D_MODEL
GATE_LANE0 = 0
ALOW_LANE0 = 3 * NSA_HEADS
FEAT_ROWS = KV_WIDTH // NSA_KV_HEADS

VMEM_LIMIT = 48 * 1024 * 1024


def _cparams(*sem):
    return pltpu.CompilerParams(dimension_semantics=sem, vmem_limit_bytes=VMEM_LIMIT)


def _tile(n, target, mult=8):
    if n <= target:
        return n
    t = (target // mult) * mult
    while t >= mult:
        if n % t == 0:
            return t
        t -= mult
    return n


def _sigmoid(x):
    return 1.0 / (1.0 + jnp.exp(-x))


def _silu(x):
    return x * _sigmoid(x)


def _log_sigmoid(x):
    return -(jnp.maximum(-x, 0.0) + jnp.log1p(jnp.exp(-jnp.abs(x))))


def _dot_nt(a, b):
    return lax.dot_general(a, b, (((1,), (1,)), ((), ())), preferred_element_type=F32)


def _dot(a, b):
    return jnp.dot(a, b, preferred_element_type=F32)


def _slope(h):
    return float(2.0 ** (-(h + 1)))


def _norm_kernel(x_ref, g_ref, o_ref):
    xf = x_ref[...]
    xn = xf * lax.rsqrt(jnp.mean(xf * xf, axis=-1, keepdims=True) + NORM_EPS)
    o_ref[...] = (xn * g_ref[...]).astype(o_ref.dtype)


def rms_norm_rows(x, g, out_dtype):
    n, d = x.shape
    tm = _tile(n, 512)
    return pl.pallas_call(
        _norm_kernel,
        grid=(n // tm,),
        in_specs=[pl.BlockSpec((tm, d), lambda i: (i, 0)), pl.BlockSpec((1, d), lambda i: (0, 0))],
        out_specs=pl.BlockSpec((tm, d), lambda i: (i, 0)),
        out_shape=jax.ShapeDtypeStruct((n, d), out_dtype),
        compiler_params=_cparams("parallel"),
        name="rms_norm",
    )(x, g.reshape(1, d))


def _mm_kernel(x_ref, w_ref, o_ref, *, scale):
    acc = _dot(x_ref[...], w_ref[...])
    if scale != 1.0:
        acc = acc * scale
    o_ref[...] = acc.astype(o_ref.dtype)


def matmul_rows(x, w, out_dtype, *, tn, scale=1.0, tm_target=2048):
    n, k = x.shape
    m = w.shape[1]
    tm = _tile(n, tm_target)
    return pl.pallas_call(
        functools.partial(_mm_kernel, scale=scale),
        grid=(n // tm, m // tn),
        in_specs=[pl.BlockSpec((tm, k), lambda i, j: (i, 0)), pl.BlockSpec((k, tn), lambda i, j: (0, j))],
        out_specs=pl.BlockSpec((tm, tn), lambda i, j: (i, j)),
        out_shape=jax.ShapeDtypeStruct((n, m), out_dtype),
        compiler_params=_cparams("parallel", "parallel"),
        name="proj",
    )(x, w)


def _mm_t_kernel(wt_ref, x_ref, *o_refs):
    ob_ref = o_refs[-1]
    acc = _dot_nt(wt_ref[...], x_ref[...]).reshape(ob_ref.shape)
    if len(o_refs) == 2:
        o_refs[0][...] = acc
    else:
        for j, of_ref in enumerate(o_refs[:-1]):
            of_ref[...] = acc[j]
    ob_ref[...] = acc.astype(ob_ref.dtype)


def kv_project_t(x, wt, batch, seq, split_f32):
    k = x.shape[1]
    tm = _tile(seq, 512, LANES)
    nt = seq // tm
    n_slab = wt.shape[0] // KV_WIDTH
    stacked = pl.BlockSpec((n_slab, None, KV_WIDTH, tm), lambda b, i: (0, b, 0, i))
    if split_f32:
        f_specs = [pl.BlockSpec((None, KV_WIDTH, tm), lambda b, i: (b, 0, i))] * n_slab
        f_shapes = [jax.ShapeDtypeStruct((batch, KV_WIDTH, seq), F32)] * n_slab
    else:
        f_specs = [stacked]
        f_shapes = [jax.ShapeDtypeStruct((n_slab, batch, KV_WIDTH, seq), F32)]
    outs = pl.pallas_call(
        _mm_t_kernel,
        grid=(batch, nt),
        in_specs=[pl.BlockSpec(wt.shape, lambda b, i: (0, 0)), pl.BlockSpec((tm, k), lambda b, i: (b * nt + i, 0))],
        out_specs=f_specs + [stacked],
        out_shape=f_shapes + [jax.ShapeDtypeStruct((n_slab, batch, KV_WIDTH, seq), BF16)],
        compiler_params=_cparams("parallel", "parallel"),
        name="proj_kv_t",
    )(wt, x)
    return (outs[:-1] if split_f32 else outs[0]), outs[-1]


def _chunk_rows(xs_ref, n_chunks):
    return jnp.concatenate([xs_ref[pl.ds(l, n_chunks, stride=CMP_STRIDE), :] for l in range(CMP_STRIDE)], axis=1)


def _halves(chunks, pea_ref, peb_ref, wa_ref, wb_ref):
    a = _dot((chunks + pea_ref[...]).astype(BF16), wa_ref[...])
    b = _dot((chunks + peb_ref[...]).astype(BF16), wb_ref[...])
    return a, b


def _summaries(a, b_next, w2bd):
    return _dot(_silu(a + b_next).astype(BF16), w2bd)


def _shift_up(x):
    n = x.shape[0]
    return pltpu.roll(x, n - 1, 0)


def _transpose_pages(src, xs_ref, n_pages):
    for p in range(n_pages):
        xs_ref[p * LANES:(p + 1) * LANES, :] = src(p).T


def _pool_ab_kernel(xk_ref, xv_ref, kpea, kpeb, kwa, kwb, vpea, vpeb, vwa, vwb, o_ref, xs_scr):
    n_pages = xk_ref.shape[0]
    hid = kwa.shape[1]
    for i, (x_ref, cw) in enumerate(((xk_ref, (kpea, kpeb, kwa, kwb)), (xv_ref, (vpea, vpeb, vwa, vwb)))):
        _transpose_pages(lambda p: x_ref[p], xs_scr, n_pages)
        a, b = _halves(_chunk_rows(xs_scr, n_pages * (LANES // CMP_STRIDE)), *cw)
        o_ref[:, (2 * i) * hid:(2 * i + 1) * hid] = a
        o_ref[:, (2 * i + 1) * hid:(2 * i + 2) * hid] = b


def pool_halves(pool_k_t, pool_v_t, layer, cw_k, cw_v):
    n_pool = pool_k_t.shape[1]
    pg = _tile(n_pool, 64, 1)
    rows = pg * (LANES // CMP_STRIDE)
    hid = cw_k[2].shape[1]
    full = lambda a: pl.BlockSpec(a.shape, lambda i: (0,) * a.ndim)
    page_spec = pl.BlockSpec((None, pg, KV_WIDTH, LANES), lambda i: (layer, i, 0, 0))
    return pl.pallas_call(
        _pool_ab_kernel,
        grid=(n_pool // pg,),
        in_specs=[page_spec, page_spec] + [full(c) for c in (*cw_k, *cw_v)],
        out_specs=pl.BlockSpec((rows, 4 * hid), lambda i: (i, 0)),
        out_shape=jax.ShapeDtypeStruct((n_pool * (LANES // CMP_STRIDE), 4 * hid), F32),
        scratch_shapes=[pltpu.VMEM((pg * LANES, KV_WIDTH), F32)],
        compiler_params=_cparams("parallel"),
        name="pool_halves",
    )(pool_k_t, pool_v_t, *cw_k, *cw_v)


def _pcmp_kernel(kt_ref, vt_ref, kpea, kpeb, kwa, kwb, kw2, vpea, vpeb, vwa, vwb, vw2, featc_ref,
                 kcf_ref, vc2_ref, xs_scr):
    seq = kt_ref.shape[1]
    n_blocks = seq // LANES
    n_chunks = seq // CMP_STRIDE
    half = KV_WIDTH // NSA_KV_HEADS
    for src_ref, (pea, peb, wa, wb, w2), is_key in ((kt_ref, (kpea, kpeb, kwa, kwb, kw2), True),
                                                   (vt_ref, (vpea, vpeb, vwa, vwb, vw2), False)):
        _transpose_pages(lambda p: src_ref[:, p * LANES:(p + 1) * LANES], xs_scr, n_blocks)
        a, b = _halves(_chunk_rows(xs_scr, n_chunks), pea, peb, wa, wb)
        rows = _summaries(a, _shift_up(b), w2[...])
        if is_key:
            rows_t = rows.T.astype(kcf_ref.dtype)
            kcf_ref[0, 0:half, :] = rows_t[0:half]
            kcf_ref[0, half:, :] = featc_ref[...]
            kcf_ref[1, 0:half, :] = featc_ref[...]
            kcf_ref[1, half:, :] = rows_t[half:]
        else:
            lane = lax.broadcasted_iota(jnp.int32, rows.shape, 1)
            vc2_ref[0] = jnp.where(lane < half, rows, 1.0).astype(vc2_ref.dtype)
            vc2_ref[1] = jnp.where(lane >= half, rows, 1.0).astype(vc2_ref.dtype)


def _feature_rows(pos):
    assert pos.max() // SEL_BLOCK < FEAT_ROWS
    f = np.zeros((FEAT_ROWS, pos.shape[0]), np.float32)
    f[1:] = pos[None, :] // SEL_BLOCK == np.arange(1, FEAT_ROWS)[:, None]
    f[0] = pos % SEL_BLOCK
    return f


def prompt_summaries(kt_cmp, vt_cmp, cw_k, w2k, cw_v, w2v):
    batch, _, seq = kt_cmp.shape
    n_c = seq // CMP_STRIDE
    cend = np.minimum(np.arange(n_c) * CMP_STRIDE + (CMP_BLOCK - 1), seq - 1)
    consts = [*cw_k, w2k, *cw_v, w2v, jnp.asarray(_feature_rows(cend), BF16)]
    full = lambda a: pl.BlockSpec(a.shape, lambda b: (0,) * a.ndim)
    return pl.pallas_call(
        _pcmp_kernel,
        grid=(batch,),
        in_specs=[pl.BlockSpec((None, KV_WIDTH, seq), lambda b: (b, 0, 0))] * 2 + [full(c) for c in consts],
        out_specs=[pl.BlockSpec((None, NSA_KV_HEADS, KV_WIDTH, n_c), lambda b: (b, 0, 0, 0)),
                   pl.BlockSpec((None, NSA_KV_HEADS, n_c, KV_WIDTH), lambda b: (b, 0, 0, 0))],
        out_shape=[jax.ShapeDtypeStruct((batch, NSA_KV_HEADS, KV_WIDTH, n_c), BF16),
                   jax.ShapeDtypeStruct((batch, NSA_KV_HEADS, n_c, KV_WIDTH), BF16)],
        scratch_shapes=[pltpu.VMEM((seq, KV_WIDTH), F32)],
        compiler_params=_cparams("parallel"),
        name="prompt_summaries",
    )(kt_cmp, vt_cmp, *consts)


def _rank_select_t(score_t, n_cand, topn):
    idx = lax.broadcasted_iota(jnp.int32, score_t.shape, 0)
    rank = jnp.zeros(score_t.shape, jnp.int32)
    for j in range(n_cand):
        row = score_t[j:j + 1, :]
        rank = rank + jnp.where(idx > j, jnp.where(row >= score_t, 1, 0), jnp.where(row > score_t, 1, 0))
    return jnp.where(rank < topn, 1.0, 0.0)


def _pattn_kernel(q_ref, kcf_ref, vc2_ref, kst_ref, vst_ref, kwt_ref, vwt_ref, feat_ref, sm_ref, ovt_ref,
                  o_ref, kfs_scr, kfw_scr, vs_scr, vw_scr, m_scr, acc_scr, *, n_sel, topn):
    qi = pl.program_id(1)
    t0 = qi * Q_BLOCK
    half = KV_WIDTH // NSA_KV_HEADS

    @pl.when(qi == 0)
    def _():
        ones = jnp.ones((half, kst_ref.shape[1]), BF16)
        for src, dst, fill in ((kst_ref, kfs_scr, feat_ref[...]), (kwt_ref, kfw_scr, feat_ref[...]),
                               (vst_ref, vs_scr, ones), (vwt_ref, vw_scr, ones)):
            dst[0, 0:half, :] = src[0:half, :]
            dst[0, half:, :] = fill
            dst[1, 0:half, :] = fill
            dst[1, half:, :] = src[half:, :]

    q = q_ref[...]
    gates = _sigmoid(sm_ref[...])

    def gate_col(branch, h):
        j = GATE_LANE0 + branch * NSA_HEADS + h
        return gates[:, j:j + 1]

    lane = lax.broadcasted_iota(jnp.int32, (Q_BLOCK, LANES), 1)
    feat_f = jnp.where(lane >= half, lane - half, lane).astype(F32)
    sel_lanes = []

    def own_lanes(h):
        return (lane < half) if h < NSA_GROUP else (lane >= half)

    def q_operand(selected):
        parts = []
        for h in range(NSA_HEADS):
            c = jnp.where(feat_f == 0.0, _slope(h), (_slope(h) * SEL_BLOCK) * feat_f)
            if selected:
                c = c + (sel_lanes[h // NSA_GROUP] - 1.0) * MASK_BIG
            parts.append(jnp.where(own_lanes(h), q[:, h * KV_WIDTH:(h + 1) * KV_WIDTH], c.astype(BF16)))
        return jnp.concatenate(parts, axis=0)

    grp_rows = NSA_GROUP * Q_BLOCK

    def scores(qa, kf):
        return jnp.concatenate([_dot(qa[g * grp_rows:(g + 1) * grp_rows], kf(g)) for g in range(NSA_KV_HEADS)], axis=0)

    def split_sums(h, a):
        own = own_lanes(h)
        total = jnp.where(own, pltpu.roll(a, half, 1), a)
        return jnp.where(own, a / jnp.maximum(total, TINY), 0.0), total

    qa_plain = q_operand(False)

    def compressed_and_select():
        n_c = kcf_ref.shape[2]
        s_all = scores(qa_plain, lambda g: kcf_ref[g])
        tq = lax.broadcasted_iota(jnp.int32, (Q_BLOCK, n_c), 0) + t0
        cend = lax.broadcasted_iota(jnp.int32, (Q_BLOCK, n_c), 1) * CMP_STRIDE + (CMP_BLOCK - 1)
        vis_c = tq >= cend
        lane_tiles = n_c // LANES
        floor_m = jnp.full((Q_BLOCK, LANES), NEG_INF, F32)
        ps = []
        for h in range(NSA_HEADS):
            s = jnp.where(vis_c, s_all[h * Q_BLOCK:(h + 1) * Q_BLOCK], NEG_INF)
            m = jnp.maximum(floor_m, jnp.max(s, axis=-1, keepdims=True))
            ps.append(jnp.where(vis_c, jnp.exp(s - jnp.concatenate([m] * lane_tiles, axis=1)), 0.0))
        p_all = jnp.concatenate([p.astype(BF16) for p in ps], axis=0)
        pv = [_dot(p_all[g * grp_rows:(g + 1) * grp_rows], vc2_ref[g]) for g in range(NSA_KV_HEADS)]
        psum = [None] * NSA_KV_HEADS
        for h in range(NSA_HEADS):
            g = h // NSA_GROUP
            cols = slice(h * KV_WIDTH, (h + 1) * KV_WIDTH)
            o_cmp, total = split_sums(h, pv[g][(h % NSA_GROUP) * Q_BLOCK:(h % NSA_GROUP + 1) * Q_BLOCK])
            p = ps[h] / jnp.concatenate([jnp.maximum(total, TINY)] * lane_tiles, axis=1)
            psum[g] = p if psum[g] is None else psum[g] + p
            o_ref[:, cols] = o_ref[:, cols] + gate_col(0, h) * o_cmp

        s_idx = lax.broadcasted_iota(jnp.int32, (n_sel, Q_BLOCK), 0)
        tq_s = lax.broadcasted_iota(jnp.int32, (n_sel, Q_BLOCK), 1) + t0
        valid = s_idx * SEL_BLOCK <= tq_s
        forced = (s_idx == 0) | (s_idx == tq_s // SEL_BLOCK)
        for g in range(NSA_KV_HEADS):
            hi = psum[g].astype(BF16)
            lo = (psum[g] - hi.astype(F32)).astype(BF16)
            imp_t = _dot_nt(ovt_ref[...], hi) + _dot_nt(ovt_ref[...], lo)
            score_t = jnp.where(valid, imp_t + jnp.where(forced, FORCE_BONUS, 0.0), NEG_INF)
            sel_t = _rank_select_t(score_t, n_sel, topn)
            sel_rows = [sel_t] + ([jnp.zeros((half - n_sel, Q_BLOCK), F32)] if half > n_sel else [])
            other = [jnp.zeros((half, Q_BLOCK), F32)]
            sel_lanes.append(jnp.concatenate(other + sel_rows if g == 0 else sel_rows + other, axis=0).T)

    row_i = lax.broadcasted_iota(jnp.int32, (Q_BLOCK, KEY_BLOCK), 0)
    lane_i = lax.broadcasted_iota(jnp.int32, (Q_BLOCK, KEY_BLOCK), 1)

    def reset():
        m_scr[...] = jnp.full(m_scr.shape, NEG_INF, F32)
        acc_scr[...] = jnp.zeros(acc_scr.shape, F32)

    def flash_step(kb, qa, kf_ref, v_scr, masked):
        k0 = pl.multiple_of(kb * KEY_BLOCK, KEY_BLOCK)
        s_all = scores(qa, lambda g: kf_ref[g, :, pl.ds(k0, KEY_BLOCK)])
        if masked:
            dist = (t0 - k0) + (row_i - lane_i)
            keep = jnp.where(dist >= 0, dist, WINDOW) < WINDOW
        ps = []
        alphas = []
        for h in range(NSA_HEADS):
            rows = slice(h * Q_BLOCK, (h + 1) * Q_BLOCK)
            s = s_all[rows]
            if masked:
                s = jnp.where(keep, s, NEG_INF)
            m_prev = m_scr[rows]
            m_new = jnp.maximum(m_prev, jnp.max(s, axis=-1, keepdims=True))
            p = jnp.exp(s - jnp.concatenate([m_new] * (KEY_BLOCK // LANES), axis=1))
            if masked:
                p = jnp.where(keep, p, 0.0)
            alpha = jnp.exp(m_prev - m_new)
            m_scr[rows] = m_new
            ps.append(p.astype(BF16))
            alphas.append(alpha)
        p_all = jnp.concatenate(ps, axis=0)
        pv = [_dot_nt(p_all[g * grp_rows:(g + 1) * grp_rows], v_scr[g, :, pl.ds(k0, KEY_BLOCK)])
              for g in range(NSA_KV_HEADS)]
        acc_scr[...] = jnp.concatenate(alphas, axis=0) * acc_scr[...] + jnp.concatenate(pv, axis=0)

    def finish(branch, first):
        for h in range(NSA_HEADS):
            rows = slice(h * Q_BLOCK, (h + 1) * Q_BLOCK)
            cols = slice(h * KV_WIDTH, (h + 1) * KV_WIDTH)
            o_b, _ = split_sums(h, acc_scr[rows])
            gated = gate_col(branch, h) * o_b
            o_ref[:, cols] = gated if first else o_ref[:, cols] + gated

    def loop(lo, hi, qa, kf_ref, v_scr):
        n = jnp.maximum(hi - lo, 0)

        def pair(i, c):
            flash_step(lo + 2 * i, qa, kf_ref, v_scr, False)
            flash_step(lo + 2 * i + 1, qa, kf_ref, v_scr, False)
            return c

        lax.fori_loop(0, n // 2, pair, 0)

        @pl.when(n % 2 == 1)
        def _():
            flash_step(hi - 1, qa, kf_ref, v_scr, False)

    per_key_block = KEY_BLOCK // Q_BLOCK
    kb_diag = qi // per_key_block
    kb_lo = jnp.maximum(qi - WINDOW // Q_BLOCK, 0) // per_key_block

    reset()

    @pl.when(kb_lo < kb_diag)
    def _():
        flash_step(kb_lo, qa_plain, kfw_scr, vw_scr, True)

    loop(kb_lo + 1, kb_diag, qa_plain, kfw_scr, vw_scr)
    flash_step(kb_diag, qa_plain, kfw_scr, vw_scr, True)
    finish(2, True)
    compressed_and_select()
    reset()
    qa = q_operand(True)
    loop(0, kb_diag, qa, kfs_scr, vs_scr)
    flash_step(kb_diag, qa, kfs_scr, vs_scr, True)
    finish(1, False)


def prompt_attention(qp, kcf, vc2, kvt_b, r, batch, seq):
    assert seq % KEY_BLOCK == 0 and min(SEL_TOPN, seq // SEL_BLOCK) >= 2
    nqb = seq // Q_BLOCK
    n_c = kcf.shape[3]
    assert n_c % LANES == 0
    n_sel = seq // SEL_BLOCK
    topn = min(SEL_TOPN, n_sel)
    cs = np.arange(n_c)[:, None] * CMP_STRIDE
    ss = np.arange(n_sel)[None, :] * SEL_BLOCK
    ov = np.clip(np.minimum(cs + CMP_BLOCK, ss + SEL_BLOCK) - np.maximum(cs, ss), 0, None) / CMP_BLOCK
    ov[(seq - CMP_BLOCK) // CMP_STRIDE + 1:] = 0.0
    kv_spec = lambda idx: pl.BlockSpec((None, None, KV_WIDTH, seq), lambda b, i: (idx, b, 0, 0))
    return pl.pallas_call(
        functools.partial(_pattn_kernel, n_sel=n_sel, topn=topn),
        grid=(batch, nqb),
        in_specs=[
            pl.BlockSpec((Q_BLOCK, PADDED_Q), lambda b, i: (b * nqb + i, 0)),
            pl.BlockSpec((None, NSA_KV_HEADS, KV_WIDTH, n_c), lambda b, i: (b, 0, 0, 0)),
            pl.BlockSpec((None, NSA_KV_HEADS, n_c, KV_WIDTH), lambda b, i: (b, 0, 0, 0)),
            kv_spec(2), kv_spec(3), kv_spec(4), kv_spec(5),
            pl.BlockSpec((FEAT_ROWS, seq), lambda b, i: (0, 0)),
            pl.BlockSpec((Q_BLOCK, LANES), lambda b, i: (b * nqb + i, R_SMALL // LANES)),
            pl.BlockSpec((n_sel, n_c), lambda b, i: (0, 0)),
        ],
        out_specs=pl.BlockSpec((Q_BLOCK, PADDED_Q), lambda b, i: (b * nqb + i, 0)),
        out_shape=jax.ShapeDtypeStruct((batch * seq, PADDED_Q), F32),
        scratch_shapes=[
            pltpu.VMEM((NSA_KV_HEADS, KV_WIDTH, seq), BF16),
            pltpu.VMEM((NSA_KV_HEADS, KV_WIDTH, seq), BF16),
            pltpu.VMEM((NSA_KV_HEADS, KV_WIDTH, seq), BF16),
            pltpu.VMEM((NSA_KV_HEADS, KV_WIDTH, seq), BF16),
            pltpu.VMEM((NSA_HEADS * Q_BLOCK, LANES), F32),
            pltpu.VMEM((NSA_HEADS * Q_BLOCK, KV_WIDTH), F32),
        ],
        compiler_params=_cparams("parallel", "arbitrary"),
        name="prompt_nsa",
    )(qp, kcf, vc2, kvt_b, kvt_b, kvt_b, kvt_b, jnp.asarray(_feature_rows(np.arange(seq)), BF16), r,
      jnp.asarray(ov.T, BF16))


def _gla_log_decay(sm, aup_ref, ab_ref):
    z = _dot(sm.astype(BF16), aup_ref[...]) + ab_ref[...]
    return _log_sigmoid(z) / GLA_GATE_TAU


def _gla_head_norm(o, gg_ref):
    on = o * lax.rsqrt(jnp.mean(o * o, axis=-1, keepdims=True) + NORM_EPS)
    return on * gg_ref[...]


def _pgla_kernel(qg_ref, kg_ref, vg_ref, sm_ref, aup_ref, ab_ref, gg_ref, og_ref, sfin_ref, st_scr, *, n_tiles):
    ti = pl.program_id(1)
    tt = qg_ref.shape[0]
    width = qg_ref.shape[1]

    @pl.when(ti == 0)
    def _():
        st_scr[...] = jnp.zeros(st_scr.shape, F32)

    la = _gla_log_decay(sm_ref[...], aup_ref, ab_ref)
    rin = lax.broadcasted_iota(jnp.int32, la.shape, 0)
    bg = la
    sh = 1
    while sh < tt:
        bg = bg + jnp.where(rin >= sh, pltpu.roll(bg, sh, 0), 0.0)
        sh *= 2

    def row_of_block(x, block, row):
        x3 = x.reshape(tt // block, block, width)
        return jnp.broadcast_to(x3[:, row:row + 1, :], x3.shape).reshape(tt, width)

    n_chunks = tt // GLA_CHUNK
    ends = bg.reshape(n_chunks, GLA_CHUNK, width)[:, GLA_CHUNK - 1:GLA_CHUNK, :]
    prev_end = jnp.concatenate([jnp.zeros((1, 1, width), F32), ends[:n_chunks - 1]], axis=0)
    b = (bg.reshape(n_chunks, GLA_CHUNK, width) - prev_end).reshape(tt, width)
    bg_end = bg[tt - 1:tt, :]
    scales = [(jnp.exp(b), jnp.exp(-b))]
    block = 2 * GLA_CHUNK
    while block <= tt:
        mid = row_of_block(bg, block, block // 2 - 1)
        scales.append((jnp.exp(jnp.minimum(bg - mid, 0.0)), jnp.exp(jnp.minimum(mid - bg, 0.0))))
        block *= 2
    e_in = jnp.exp(bg)
    e_out = jnp.exp(bg_end - bg)

    r_i = lax.broadcasted_iota(jnp.int32, (tt, tt), 0)
    c_i = lax.broadcasted_iota(jnp.int32, (tt, tt), 1)
    level = jnp.full((tt, tt), len(scales) - 1, jnp.int32)
    block = tt // 2
    lv = len(scales) - 2
    while block >= GLA_CHUNK:
        level = jnp.where(r_i // block == c_i // block, lv, level)
        block //= 2
        lv -= 1
    level = jnp.where(c_i <= r_i, level, -1)

    for h in range(GLA_HEADS):
        ks = slice(h * GLA_DK, (h + 1) * GLA_DK)
        vs = slice(h * GLA_DV, (h + 1) * GLA_DV)
        q = qg_ref[:, ks] * (GLA_DK ** -0.5)
        k = kg_ref[:, ks]
        v = vg_ref[:, vs].astype(BF16)
        a = jnp.zeros((tt, tt), F32)
        for lv, (sq, sk) in enumerate(scales):
            a_lv = _dot_nt((q * sq[:, ks]).astype(BF16), (k * sk[:, ks]).astype(BF16))
            a = jnp.where(level == lv, a_lv, a)
        st = st_scr[h]
        o = _dot(a.astype(BF16), v) + _dot_nt((q * e_in[:, ks]).astype(BF16), st.astype(BF16))
        kv_t = lax.dot_general(v, (k * e_out[:, ks]).astype(BF16), (((0,), (0,)), ((), ())),
                               preferred_element_type=F32)
        st_scr[h] = st * jnp.exp(bg_end[:, ks]) + kv_t
        og_ref[:, vs] = _gla_head_norm(o, gg_ref)

    @pl.when(ti == n_tiles - 1)
    def _():
        for h in range(GLA_HEADS):
            sfin_ref[h] = st_scr[h].T


def prompt_gla(r, aup, ab, gg, batch, seq):
    tt = _tile(seq, 256, GLA_CHUNK)
    nt = seq // tt
    row = lambda b, i: b * nt + i
    return pl.pallas_call(
        functools.partial(_pgla_kernel, n_tiles=nt),
        grid=(batch, nt),
        in_specs=[
            pl.BlockSpec((tt, GLA_KEY_WIDTH), lambda b, i: (row(b, i), R_QG // GLA_KEY_WIDTH)),
            pl.BlockSpec((tt, GLA_KEY_WIDTH), lambda b, i: (row(b, i), R_KG // GLA_KEY_WIDTH)),
            pl.BlockSpec((tt, GLA_VAL_WIDTH), lambda b, i: (row(b, i), R_VG // GLA_VAL_WIDTH)),
            pl.BlockSpec((tt, LANES), lambda b, i: (row(b, i), R_SMALL // LANES)),
            pl.BlockSpec(aup.shape, lambda b, i: (0, 0)),
            pl.BlockSpec(ab.shape, lambda b, i: (0, 0)),
            pl.BlockSpec(gg.shape, lambda b, i: (0, 0)),
        ],
        out_specs=[
            pl.BlockSpec((tt, GLA_VAL_WIDTH), lambda b, i: (row(b, i), 0)),
            pl.BlockSpec((None, GLA_HEADS, GLA_DK, GLA_DV), lambda b, i: (b, 0, 0, 0)),
        ],
        out_shape=[
            jax.ShapeDtypeStruct((batch * seq, GLA_VAL_WIDTH), F32),
            jax.ShapeDtypeStruct((batch, GLA_HEADS, GLA_DK, GLA_DV), F32),
        ],
        scratch_shapes=[pltpu.VMEM((GLA_HEADS, GLA_DV, GLA_DK), F32)],
        compiler_params=_cparams("parallel", "arbitrary"),
        name="prompt_gla",
    )(r, r, r, r, aup, ab, gg)


def _sgla_kernel(qg_ref, kg_ref, vg_ref, sm_ref, aup_ref, ab_ref, gg_ref, s_ref, og_ref, snew_ref, *, ds):
    rows = qg_ref.shape[0]
    nb = rows // ds
    la = _gla_log_decay(sm_ref[...], aup_ref, ab_ref)
    ri = lax.broadcasted_iota(jnp.int32, la.shape, 0) % ds
    b = la
    sh = 1
    while sh < ds:
        b = b + jnp.where(ri >= sh, pltpu.roll(b, sh, 0), 0.0)
        sh *= 2
    b_last = b
    for d in range(1, ds):
        b_last = jnp.where(ri == ds - 1 - d, pltpu.roll(b, rows - d, 0), b_last)
    e_b = jnp.exp(b)
    e_nb = jnp.exp(-b)
    e_tail = jnp.exp(b_last - b)
    e_last = jnp.exp(b_last)

    r_i = lax.broadcasted_iota(jnp.int32, (rows, rows), 0)
    c_i = lax.broadcasted_iota(jnp.int32, (rows, rows), 1)
    causal = (r_i // ds == c_i // ds) & (c_i <= r_i)
    row_b = lax.broadcasted_iota(jnp.int32, (rows, GLA_DV), 0) // ds

    for h in range(GLA_HEADS):
        ks = slice(h * GLA_DK, (h + 1) * GLA_DK)
        vs = slice(h * GLA_DV, (h + 1) * GLA_DV)
        q = qg_ref[:, ks] * (GLA_DK ** -0.5)
        k = kg_ref[:, ks]
        v = vg_ref[:, vs]
        q_dec = (q * e_b[:, ks]).astype(BF16)
        k_inv = (k * e_nb[:, ks]).astype(BF16)
        k_tail_t = (k * e_tail[:, ks]).T
        e_last_t = e_last[:, ks].T
        a = jnp.where(causal, _dot_nt(q_dec, k_inv), 0.0)
        o = _dot(a.astype(BF16), v.astype(BF16))
        for bb in range(nb):
            s_prev = s_ref[bb, h]
            o_inter = _dot(q_dec, s_prev.astype(BF16))
            o = o + jnp.where(row_b == bb, o_inter, 0.0)
            s_new = e_last_t[:, bb * ds:bb * ds + 1] * s_prev
            for j in range(ds):
                rr = bb * ds + j
                s_new = s_new + k_tail_t[:, rr:rr + 1] * v[rr:rr + 1, :]
            snew_ref[bb, h] = s_new
        og_ref[:, vs] = _gla_head_norm(o, gg_ref)


def sample_gla(r, aup, ab, gg, state, layer, dec_batch, ds):
    nb = _tile(dec_batch, 8, 1)
    rows = nb * ds
    return pl.pallas_call(
        functools.partial(_sgla_kernel, ds=ds),
        grid=(dec_batch // nb,),
        in_specs=[
            pl.BlockSpec((rows, GLA_KEY_WIDTH), lambda i: (i, R_QG // GLA_KEY_WIDTH)),
            pl.BlockSpec((rows, GLA_KEY_WIDTH), lambda i: (i, R_KG // GLA_KEY_WIDTH)),
            pl.BlockSpec((rows, GLA_VAL_WIDTH), lambda i: (i, R_VG // GLA_VAL_WIDTH)),
            pl.BlockSpec((rows, LANES), lambda i: (i, R_SMALL // LANES)),
            pl.BlockSpec(aup.shape, lambda i: (0, 0)),
            pl.BlockSpec(ab.shape, lambda i: (0, 0)),
            pl.BlockSpec(gg.shape, lambda i: (0, 0)),
            pl.BlockSpec((None, nb, GLA_HEADS, GLA_DK, GLA_DV), lambda i: (layer, i, 0, 0, 0)),
        ],
        out_specs=[
            pl.BlockSpec((rows, GLA_VAL_WIDTH), lambda i: (i, 0)),
            pl.BlockSpec((nb, GLA_HEADS, GLA_DK, GLA_DV), lambda i: (i, 0, 0, 0)),
        ],
        out_shape=[
            jax.ShapeDtypeStruct((dec_batch * ds, GLA_VAL_WIDTH), F32),
            jax.ShapeDtypeStruct((dec_batch, GLA_HEADS, GLA_DK, GLA_DV), F32),
        ],
        compiler_params=_cparams("parallel"),
        name="sample_gla",
    )(r, r, r, r, aup, ab, gg, state)


def _mix_kernel(x_ref, on_ref, za_ref, og_ref, zb_ref, mg_ref, wba_ref, wbb_ref, wo_ref, fg_ref, *o_refs):
    f32 = lambda ref: ref[...].astype(F32)
    pa = _dot((on_ref[...] * _silu(f32(za_ref))).astype(BF16), wba_ref[...])
    pb = _dot((og_ref[...] * _silu(f32(zb_ref))).astype(BF16), wbb_ref[...])
    mg = f32(mg_ref)
    mix = _sigmoid(mg[:, :D_MODEL]) * pa + _sigmoid(mg[:, D_MODEL:]) * pb
    y = x_ref[...] + _dot(mix.astype(BF16), wo_ref[...])
    o_refs[0][...] = y
    if len(o_refs) > 1:
        yn = y * lax.rsqrt(jnp.mean(y * y, axis=-1, keepdims=True) + NORM_EPS)
        o_refs[1][...] = yn * fg_ref[...]


def mixer_output(x, o_nsa, rz, o_gla, wba, wbb, wo, final_g, with_final_norm):
    n = x.shape[0]
    tm = _tile(n, 256)
    row = lambda w: pl.BlockSpec((tm, w), lambda i: (i, 0))
    wspec = lambda w: pl.BlockSpec(w.shape, lambda i: (0, 0))
    n_out = 2 if with_final_norm else 1
    return pl.pallas_call(
        _mix_kernel,
        grid=(n // tm,),
        in_specs=[
            row(D_MODEL), row(PADDED_Q),
            pl.BlockSpec((tm, PADDED_Q), lambda i: (i, Z_ZA // PADDED_Q)),
            row(GLA_VAL_WIDTH),
            pl.BlockSpec((tm, GLA_VAL_WIDTH), lambda i: (i, Z_ZB // GLA_VAL_WIDTH)),
            pl.BlockSpec((tm, 2 * D_MODEL), lambda i: (i, Z_MG // (2 * D_MODEL))),
            wspec(wba), wspec(wbb), wspec(wo), wspec(final_g),
        ],
        out_specs=[row(D_MODEL)] * n_out,
        out_shape=[jax.ShapeDtypeStruct((n, D_MODEL), F32)] * n_out,
        compiler_params=_cparams("parallel"),
        name="mixer_output",
    )(x, o_nsa, rz, o_gla, rz, rz, wba, wbb, wo, final_g)


def _select_blocks(score, lane_idx, n_cand, topn):
    rank = jnp.zeros(score.shape, jnp.int32)
    for j in range(n_cand):
        col = score[:, j:j + 1]
        rank = rank + jnp.where(col > score, 1, 0) + jnp.where(col == score, jnp.where(lane_idx > j, 1, 0), 0)
    return jnp.where(rank < topn, 1.0, 0.0)


def _softmax_two(parts, keeps):
    m = functools.reduce(jnp.maximum, [jnp.max(s, axis=-1, keepdims=True) for s in parts])
    ps = [jnp.where(kp, jnp.exp(s - m), 0.0) for s, kp in zip(parts, keeps)]
    total = functools.reduce(lambda a, c: a + c, [jnp.sum(p, axis=-1, keepdims=True) for p in ps])
    return ps, jnp.maximum(total, TINY)


def _smain_kernel(pt_ref, q_ref, gate_ref, slope_ref, ab_hbm, kt_hbm, vt_hbm, kw_ref, vw_ref, new_ref,
                  w2k_ref, w2v_ref, ov_ref, exp_ref, exn_ref, o_ref, kw_out_ref, vw_out_ref,
                  ab_buf, kt_buf, vt_buf, sems, *, layer, n_pages, past, ds, n_sel, topn):
    step = pl.program_id(0)
    buf_slot = step % 2
    rows_per_step = q_ref.shape[0]

    def page_copies(st, sl):
        cps = []
        for r in range(rows_per_step):
            for j in range(n_pages):
                pg = pt_ref[st * rows_per_step + r, j]
                cps.append(pltpu.make_async_copy(ab_hbm.at[pg], ab_buf.at[sl, r, j], sems.at[sl, 0]))
                cps.append(pltpu.make_async_copy(kt_hbm.at[layer, pg], kt_buf.at[sl, r, j], sems.at[sl, 1]))
                cps.append(pltpu.make_async_copy(vt_hbm.at[layer, pg], vt_buf.at[sl, r, j], sems.at[sl, 2]))
        return cps

    @pl.when(step == 0)
    def _():
        for cp in page_copies(0, 0):
            cp.start()

    @pl.when(step + 1 < pl.num_programs(0))
    def _():
        for cp in page_copies(step + 1, 1 - buf_slot):
            cp.start()

    for cp in page_copies(step, buf_slot):
        cp.wait()

    active = [_sample_row(step * rows_per_step + r, q_ref.at[r], gate_ref.at[r], slope_ref,
                          ab_buf.at[buf_slot, r], kt_buf.at[buf_slot, r], vt_buf.at[buf_slot, r],
                          kw_ref.at[r], vw_ref.at[r], new_ref, w2k_ref, w2v_ref, ov_ref, exp_ref, exn_ref,
                          o_ref.at[r], kw_out_ref.at[r], vw_out_ref.at[r],
                          n_pages=n_pages, past=past, ds=ds, n_sel=n_sel, topn=topn)
              for r in range(rows_per_step)]
    while active:
        active = [row for row in active if next(row, True) is None]


def _sample_row(b, q_ref, gate_ref, slope_ref, ab_ref, kt_ref, vt_ref, kw_ref, vw_ref, new_ref,
                w2k_ref, w2v_ref, ov_ref, exp_ref, exn_ref, o_ref, kw_out_ref, vw_out_ref,
                *, n_pages, past, ds, n_sel, topn):
    rows = q_ref.shape[0]
    grp_rows = NSA_KV_HEADS * ds
    n_new = new_ref.shape[2]
    b_loc = b % (n_new // ds)
    kt_refs = [kt_ref.at[j] for j in range(n_pages)]
    vt_refs = [vt_ref.at[j] for j in range(n_pages)]

    qs = q_ref[...]
    slope = slope_ref[:, 0:1]
    ab = jnp.concatenate([ab_ref[j] for j in range(n_pages)], axis=0)
    hid = ab.shape[1] // 4
    kc = _summaries(ab[:, 0:hid], _shift_up(ab[:, hid:2 * hid]), w2k_ref[...]).astype(BF16)
    vc = _summaries(ab[:, 2 * hid:3 * hid], _shift_up(ab[:, 3 * hid:]), w2v_ref[...]).astype(BF16)
    n_c = kc.shape[0]

    def irow(shape):
        return lax.broadcasted_iota(jnp.int32, shape, 0) % ds

    yield
    dist_c = (past + irow((rows, n_c))) - (lax.broadcasted_iota(jnp.int32, (rows, n_c), 1) * CMP_STRIDE
                                          + (CMP_BLOCK - 1))
    vis_c = dist_c >= 0
    s_c = jnp.where(vis_c, _dot_nt(qs, kc) - slope * dist_c.astype(F32), NEG_INF)
    m_c = jnp.max(s_c, axis=-1, keepdims=True)
    p_c = jnp.where(vis_c, jnp.exp(s_c - m_c), 0.0)
    p_c = p_c / jnp.maximum(jnp.sum(p_c, axis=-1, keepdims=True), TINY)
    o_cmp = _dot(p_c.astype(BF16), vc)

    yield
    hi = p_c.astype(BF16)
    lo = (p_c - hi.astype(F32)).astype(BF16)
    imp_h = _dot(hi, ov_ref[...]) + _dot(lo, ov_ref[...])
    imp = functools.reduce(lambda a, c: a + c,
                           [imp_h[r * grp_rows:(r + 1) * grp_rows] for r in range(NSA_GROUP)])
    s_idx = lax.broadcasted_iota(jnp.int32, imp.shape, 1)
    qpos = past + irow(imp.shape)
    valid = (s_idx < n_sel) & (s_idx * SEL_BLOCK <= qpos)
    forced = (s_idx == 0) | (s_idx == qpos // SEL_BLOCK)
    score = jnp.where(valid, imp + jnp.where(forced, FORCE_BONUS, 0.0), NEG_INF)
    sel = _select_blocks(score, s_idx, n_sel, topn)
    sel = jnp.concatenate([sel] * NSA_GROUP, axis=0).astype(BF16)
    keep_past = _dot(sel, exp_ref[...]) > 0.5
    sel_new = _dot(sel, exn_ref[...]) > 0.5

    yield
    new_lane = lax.broadcasted_iota(jnp.int32, (rows, n_new), 1)
    dist_n = irow((rows, n_new)) - new_lane % ds
    mine = (new_lane // ds == b_loc) & (dist_n >= 0)
    bias_n = slope * dist_n.astype(F32)

    s_p = jnp.concatenate([_dot(qs, r[...].astype(BF16)) for r in kt_refs], axis=1)
    dist_p = (past + irow(s_p.shape)) - lax.broadcasted_iota(jnp.int32, s_p.shape, 1)
    s_p = jnp.where(keep_past, s_p - slope * dist_p.astype(F32), NEG_INF)
    keep_n = mine & sel_new
    s_n = jnp.where(keep_n, _dot(qs, new_ref[2].astype(BF16)) - bias_n, NEG_INF)
    (p_p, p_n), total = _softmax_two([s_p, s_n], [keep_past, keep_n])
    acc = _dot_nt(p_n.astype(BF16), new_ref[3].astype(BF16))
    for j, r in enumerate(vt_refs):
        acc = acc + _dot_nt(p_p[:, j * LANES:(j + 1) * LANES].astype(BF16), r[...].astype(BF16))
    o_slc = acc / total

    yield
    kbuf = kw_ref[...]
    vbuf = vw_ref[...]
    w_buf = kbuf.shape[1]
    dist_w = (w_buf + irow((rows, w_buf))) - lax.broadcasted_iota(jnp.int32, (rows, w_buf), 1)
    keep_w = dist_w < WINDOW
    s_w = jnp.where(keep_w, _dot(qs, kbuf.astype(BF16)) - slope * dist_w.astype(F32), NEG_INF)
    s_wn = jnp.where(mine, _dot(qs, new_ref[4].astype(BF16)) - bias_n, NEG_INF)
    (p_w, p_wn), total_w = _softmax_two([s_w, s_wn], [keep_w, mine])
    o_win = (_dot_nt(p_w.astype(BF16), vbuf.astype(BF16)) + _dot_nt(p_wn.astype(BF16), new_ref[5].astype(BF16))) / total_w

    gates = _sigmoid(gate_ref[...])
    o_ref[...] = gates[:, 0:1] * o_cmp + gates[:, 1:2] * o_slc + gates[:, 2:3] * o_win

    yield
    lane_new = lax.broadcasted_iota(jnp.int32, (KV_WIDTH, n_new), 1)
    shift = (n_new - ds) - b_loc * ds
    for buf, slab, out_ref in ((kbuf, 4, kw_out_ref), (vbuf, 5, vw_out_ref)):
        rolled = pltpu.roll(buf, w_buf - ds, 1)
        own_last = pltpu.roll(new_ref[slab], shift, 1)
        out_ref[:, 0:w_buf - n_new] = rolled[:, 0:w_buf - n_new]
        out_ref[:, w_buf - n_new:w_buf] = jnp.where(lane_new >= n_new - ds, own_last, rolled[:, w_buf - n_new:w_buf])


def sample_attention(page_table, q_rows, gate_rows, ab_pages, slc_pools_t, win_bufs_t, new_t, w2k, w2v, layer, past, ds):
    assert past % SEL_BLOCK == 0 and past % LANES == 0
    dec_batch, n_pages = page_table.shape
    rows = q_rows.shape[1]
    w_buf = win_bufs_t[0].shape[3]
    n_new = min(LANES, dec_batch * ds)
    assert (dec_batch * ds) % n_new == 0 and n_new % ds == 0 and w_buf >= n_new
    n_c = n_pages * (LANES // CMP_STRIDE)
    n_cmp = (past + ds - CMP_BLOCK) // CMP_STRIDE + 1
    n_sel = -(-(past + ds) // SEL_BLOCK)
    topn = min(SEL_TOPN, n_sel)
    assert n_sel <= LANES and ds <= CMP_STRIDE
    cs = np.arange(n_c)[:, None] * CMP_STRIDE
    ss = np.arange(LANES)[None, :] * SEL_BLOCK
    ov = np.clip(np.minimum(cs + CMP_BLOCK, ss + SEL_BLOCK) - np.maximum(cs, ss), 0, None) / CMP_BLOCK
    ov[n_cmp:] = 0.0
    ov[:, n_sel:] = 0.0
    ex_past = (np.arange(past)[None, :] // SEL_BLOCK == np.arange(LANES)[:, None]).astype(np.float32)
    ex_new = ((past + np.arange(n_new) % ds)[None, :] // SEL_BLOCK == np.arange(LANES)[:, None]).astype(np.float32)
    head = np.arange(rows) // (NSA_KV_HEADS * ds) + NSA_GROUP * ((np.arange(rows) // ds) % NSA_KV_HEADS)
    slope = np.broadcast_to((2.0 ** -(head + 1.0))[:, None], (rows, LANES)).astype(np.float32)

    rps = max(r for r in (2, 1) if dec_batch % r == 0 and (n_new // ds) % r == 0)
    const = lambda shape: pl.BlockSpec(shape, lambda s, pt: (0,) * len(shape))
    in_specs = [pl.BlockSpec((rps, rows, LANES), lambda s, pt: (s, 0, 0)),
                pl.BlockSpec((rps, rows, LANES), lambda s, pt: (s, 0, 0)),
                const((rows, LANES))]
    operands = [q_rows, gate_rows, jnp.asarray(slope)]
    in_specs += [pl.BlockSpec(memory_space=pl.ANY)] * 3
    operands += [ab_pages, slc_pools_t[0], slc_pools_t[1]]
    for arr in win_bufs_t:
        in_specs.append(pl.BlockSpec((None, rps, KV_WIDTH, w_buf), lambda s, pt: (layer, s, 0, 0)))
        operands.append(arr)
    steps_per_tile = n_new // ds // rps
    in_specs.append(pl.BlockSpec((6, None, KV_WIDTH, n_new), lambda s, pt: (0, 0, 0, s // steps_per_tile)))
    operands.append(new_t)
    consts = [w2k, w2v, jnp.asarray(ov, BF16), jnp.asarray(ex_past, BF16), jnp.asarray(ex_new, BF16)]
    in_specs += [const(c.shape) for c in consts]
    operands += consts
    buf_out = pl.BlockSpec((rps, KV_WIDTH, w_buf), lambda s, pt: (s, 0, 0))
    return pl.pallas_call(
        functools.partial(_smain_kernel, layer=layer, n_pages=n_pages, past=past, ds=ds, n_sel=n_sel, topn=topn),
        grid_spec=pltpu.PrefetchScalarGridSpec(
            num_scalar_prefetch=1,
            grid=(dec_batch // rps,),
            in_specs=in_specs,
            out_specs=[pl.BlockSpec((rps, rows, LANES), lambda s, pt: (s, 0, 0)), buf_out, buf_out],
            scratch_shapes=[pltpu.VMEM((2, rps, n_pages) + ab_pages.shape[1:], F32),
                            pltpu.VMEM((2, rps, n_pages, KV_WIDTH, LANES), F32),
                            pltpu.VMEM((2, rps, n_pages, KV_WIDTH, LANES), F32),
                            pltpu.SemaphoreType.DMA((2, 3))],
        ),
        out_shape=[jax.ShapeDtypeStruct((dec_batch, rows, LANES), F32),
                   jax.ShapeDtypeStruct((dec_batch, KV_WIDTH, w_buf), F32),
                   jax.ShapeDtypeStruct((dec_batch, KV_WIDTH, w_buf), F32)],
        compiler_params=_cparams("arbitrary"),
        name="sample_nsa",
    )(page_table, *operands)


def _pad_heads_cols(w):
    k = w.shape[0]
    w4 = w.reshape(k, NSA_KV_HEADS, NSA_GROUP, HEAD_DIM)
    eye = jnp.eye(NSA_KV_HEADS, dtype=w.dtype)
    return jnp.einsum("kgrd,gp->kgrpd", w4, eye).reshape(k, PADDED_Q)


def _layer_weights(w_in, pk_pe, pk_w1, pk_w2, pv_pe, pv_w1, pv_w2, a_up, a_b, gla_g, w_ba, w_bb, w_out):
    o = _OFF
    col = lambda i: w_in[:, o[i]:o[i + 1]]
    small = jnp.concatenate([col(2), col(7), jnp.zeros((D_MODEL, LANES - 3 * NSA_HEADS - GLA_GATE_RANK), F32)], axis=1)
    w_q = _pad_heads_cols(col(0)).astype(BF16)
    w_kv_t = col(1).T.astype(BF16)
    w_g = jnp.concatenate([col(4), col(5), col(6), small], axis=1).astype(BF16)
    w_z = jnp.concatenate([_pad_heads_cols(col(3)), col(8), col(9)], axis=1).astype(BF16)
    half = CMP_BLOCK // 2

    def cmp_weights(pe, w1, w2):
        eye = jnp.eye(NSA_KV_HEADS, dtype=F32)
        w1h = w1.reshape(2, half, HEAD_DIM, CMP_HIDDEN)
        big = lambda w: jnp.einsum("ldh,gk->lgdkh", w, eye).reshape(half * KV_WIDTH, NSA_KV_HEADS * CMP_HIDDEN)
        peh = pe.reshape(2, half, 1, HEAD_DIM)
        pe_row = lambda p: jnp.broadcast_to(p, (half, NSA_KV_HEADS, HEAD_DIM)).reshape(1, half * KV_WIDTH)
        w2bd = jnp.einsum("hd,gk->ghkd", w2, eye).reshape(NSA_KV_HEADS * CMP_HIDDEN, KV_WIDTH)
        return (pe_row(peh[0]), pe_row(peh[1]), big(w1h[0]).astype(BF16), big(w1h[1]).astype(BF16)), w2bd.astype(BF16)

    cw_k, w2k = cmp_weights(pk_pe, pk_w1, pk_w2)
    cw_v, w2v = cmp_weights(pv_pe, pv_w1, pv_w2)
    aup = jnp.zeros((LANES, GLA_KEY_WIDTH), F32).at[ALOW_LANE0:ALOW_LANE0 + GLA_GATE_RANK].set(a_up).astype(BF16)
    w_ba_p = _pad_heads_cols(w_ba.T).T.astype(BF16)
    return dict(w_q=w_q, w_kv_t=w_kv_t, w_g=w_g, w_z=w_z, cw_k=cw_k, cw_v=cw_v, w2k=w2k, w2v=w2v, aup=aup,
                ab=a_b.reshape(1, GLA_KEY_WIDTH), gg=gla_g.reshape(1, GLA_DV),
                w_ba=w_ba_p, w_bb=w_bb.astype(BF16), w_out=w_out.astype(BF16))


def _project(x, g_norm, lw, batch, seq, split_f32):
    h = rms_norm_rows(x, g_norm, BF16)
    qp = matmul_rows(h, lw["w_q"], BF16, tn=512, scale=HEAD_DIM ** -0.5)
    kvt_f, kvt_b = kv_project_t(h, lw["w_kv_t"], batch, seq, split_f32)
    rg = matmul_rows(h, lw["w_g"], F32, tn=RG_WIDTH, tm_target=1024)
    rz = matmul_rows(h, lw["w_z"], BF16, tn=1024)
    return qp, kvt_f, kvt_b, rg, rz


def _tokens_minor(cache):
    lead = cache.shape[:-3]
    n = cache.ndim
    perm = tuple(range(n - 3)) + (n - 2, n - 1, n - 3)
    return cache.transpose(perm).reshape(lead + (KV_WIDTH, cache.shape[-3]))


def _tokens_major(x_t):
    lead = x_t.shape[:-2]
    n = len(lead)
    x5 = x_t.reshape(lead + (NSA_KV_HEADS, HEAD_DIM, x_t.shape[-1]))
    return x5.transpose(tuple(range(n)) + (n + 2, n, n + 1))


def kernel(x_prompt, x_sample, cache_k_cmp, cache_v_cmp, cache_k_slc, cache_v_slc, cache_k_win, cache_v_win, state_gla, page_table, norm_g, w_in, phi_k_pe, phi_k_w1, phi_k_w2, phi_v_pe, phi_v_w1, phi_v_w2, gla_alpha_up, gla_alpha_b, gla_norm_g, w_branch_a, w_branch_b, w_out, final_norm_g):
    batch, seq, _ = x_prompt.shape
    dec_batch, ds, _ = x_sample.shape
    depth = norm_g.shape[0]
    n_pool, page_size = cache_k_cmp.shape[1:3]
    assert page_size == LANES
    n_pages = page_table.shape[1]
    past = n_pages * page_size
    chunks_per_page = page_size // CMP_STRIDE
    final_g = final_norm_g.reshape(1, D_MODEL)

    cmp_pools_t = (_tokens_minor(cache_k_cmp), _tokens_minor(cache_v_cmp))
    slc_pools_t = (_tokens_minor(cache_k_slc), _tokens_minor(cache_v_slc))
    win_bufs_t = (_tokens_minor(cache_k_win), _tokens_minor(cache_v_win))

    y_p = x_prompt.reshape(batch * seq, D_MODEL)
    y_s = x_sample.reshape(dec_batch * ds, D_MODEL)
    outs_p, outs_s = [], []
    for l in range(depth):
        lw = _layer_weights(w_in[l], phi_k_pe[l], phi_k_w1[l], phi_k_w2[l], phi_v_pe[l], phi_v_w1[l], phi_v_w2[l],
                            gla_alpha_up[l], gla_alpha_b[l], gla_norm_g[l], w_branch_a[l], w_branch_b[l], w_out[l])
        last = l == depth - 1

        qp, kvt_f, kvt_b, r, rz = _project(y_p, norm_g[l], lw, batch, seq, True)
        kcf, vc2 = prompt_summaries(kvt_f[0], kvt_f[1], lw["cw_k"], lw["w2k"], lw["cw_v"], lw["w2v"])
        o_nsa = prompt_attention(qp, kcf, vc2, kvt_b, r, batch, seq)
        o_gla, s_gla = prompt_gla(r, lw["aup"], lw["ab"], lw["gg"], batch, seq)
        res = mixer_output(y_p, o_nsa, rz, o_gla, lw["w_ba"], lw["w_bb"], lw["w_out"], final_g, last)
        y_p = res[0]
        if last:
            y_p_out = res[1]
        keep = min(WINDOW, seq)
        kv5 = [_tokens_major(a) for a in kvt_f]
        outs_p.append((kv5[0], kv5[1], kv5[2], kv5[3], kv5[4][:, seq - keep:], kv5[5][:, seq - keep:], s_gla))

        qs, kvs_t, _, rs, rzs = _project(y_s, norm_g[l], lw, 1, dec_batch * ds, False)
        ab_pages = pool_halves(cmp_pools_t[0], cmp_pools_t[1], l, lw["cw_k"], lw["cw_v"])
        ab_pages = ab_pages.reshape(n_pool, chunks_per_page, -1)
        q_rows = qs.reshape(dec_batch, ds, NSA_KV_HEADS, NSA_GROUP, KV_WIDTH).transpose(0, 3, 2, 1, 4)
        q_rows = q_rows.reshape(dec_batch, NSA_HEADS * ds, KV_WIDTH)
        gl = rs[:, R_SMALL + GATE_LANE0:R_SMALL + GATE_LANE0 + 3 * NSA_HEADS]
        gl = gl.reshape(dec_batch, ds, 3, NSA_KV_HEADS, NSA_GROUP).transpose(0, 4, 3, 1, 2)
        gate_rows = jnp.pad(gl.reshape(dec_batch, NSA_HEADS * ds, 3), ((0, 0), (0, 0), (0, LANES - 3)))
        o_rows, kw_new, vw_new = sample_attention(page_table, q_rows, gate_rows, ab_pages, slc_pools_t, win_bufs_t,
                                                  kvs_t, lw["w2k"], lw["w2v"], l, past, ds)
        o_nsa_s = o_rows.reshape(dec_batch, NSA_GROUP, NSA_KV_HEADS, ds, KV_WIDTH).transpose(0, 3, 2, 1, 4)
        o_nsa_s = o_nsa_s.reshape(dec_batch * ds, PADDED_Q)
        o_gla_s, s_gla_s = sample_gla(rs, lw["aup"], lw["ab"], lw["gg"], state_gla, l, dec_batch, ds)
        res = mixer_output(y_s, o_nsa_s, rzs, o_gla_s, lw["w_ba"], lw["w_bb"], lw["w_out"], final_g, last)
        y_s = res[0]
        if last:
            y_s_out = res[1]
        kvs5 = _tokens_major(kvs_t[:, 0]).reshape(6, dec_batch, ds, NSA_KV_HEADS, HEAD_DIM)
        outs_s.append((kvs5[0], kvs5[1], kvs5[2], kvs5[3], _tokens_major(kw_new), _tokens_major(vw_new), s_gla_s))

    stack = lambda outs: [jnp.stack(t) for t in zip(*outs)]
    return (y_p_out.reshape(batch, seq, D_MODEL), y_s_out.reshape(dec_batch, ds, D_MODEL),
            *stack(outs_p), *stack(outs_s))
```

```python
import functools

import numpy as np
import jax
import jax.numpy as jnp
from jax import lax
from jax.experimental import pallas as pl
from jax.experimental.pallas import tpu as pltpu

F32 = jnp.float32
BF16 = jnp.bfloat16

D_MODEL = 1024
NSA_HEADS = 8
NSA_KV_HEADS = 2
NSA_GROUP = NSA_HEADS // NSA_KV_HEADS
HEAD_DIM = 64
NSA_WIDTH = NSA_HEADS * HEAD_DIM
KV_WIDTH = NSA_KV_HEADS * HEAD_DIM
CMP_BLOCK = 32
CMP_STRIDE = 16
CMP_HIDDEN = 2 * HEAD_DIM
SEL_BLOCK = 64
SEL_TOPN = 16
WINDOW = 512
GLA_HEADS = 4
GLA_KEY_WIDTH = D_MODEL // 2
GLA_VAL_WIDTH = D_MODEL
GLA_DK = GLA_KEY_WIDTH // GLA_HEADS
GLA_DV = GLA_VAL_WIDTH // GLA_HEADS
GLA_GATE_RANK = 16
GLA_GATE_TAU = 16.0
GLA_CHUNK = 32
Q_BLOCK = 128
KEY_BLOCK = 256
NORM_EPS = 1e-6
NEG_INF = -1e30
TINY = 1e-30
FORCE_BONUS = 1e4
MASK_BIG = float(2.0 ** 100)
PADDED_Q = NSA_HEADS * KV_WIDTH
LANES = 128

_IN_SIZES = (NSA_WIDTH, 6 * KV_WIDTH, 3 * NSA_HEADS, NSA_WIDTH, GLA_KEY_WIDTH, GLA_KEY_WIDTH,
             GLA_VAL_WIDTH, GLA_GATE_RANK, GLA_VAL_WIDTH, 2 * D_MODEL)
_OFF = tuple(int(o) for o in np.cumsum((0,) + _IN_SIZES))

R_QG, R_KG, R_VG, R_SMALL = 0, 512, 1024, 2048
RG_WIDTH = R_SMALL + LANES
Z_ZA, Z_ZB, Z_MG = 0, 1024, 2048
RZ_WIDTH = Z_MG + 2 * D_MODEL
GATE_LANE0 = 0
ALOW_LANE0 = 3 * NSA_HEADS
FEAT_ROWS = KV_WIDTH // NSA_KV_HEADS

VMEM_LIMIT = 48 * 1024 * 1024


def _cparams(*sem):
    return pltpu.CompilerParams(dimension_semantics=sem, vmem_limit_bytes=VMEM_LIMIT)


def _tile(n, target, mult=8):
    if n <= target:
        return n
    t = (target // mult) * mult
    while t >= mult:
        if n % t == 0:
            return t
        t -= mult
    return n


def _sigmoid(x):
    return 1.0 / (1.0 + jnp.exp(-x))


def _silu(x):
    return x * _sigmoid(x)


def _log_sigmoid(x):
    return -(jnp.maximum(-x, 0.0) + jnp.log1p(jnp.exp(-jnp.abs(x))))


def _dot_nt(a, b):
    return lax.dot_general(a, b, (((1,), (1,)), ((), ())), preferred_element_type=F32)


def _dot(a, b):
    return jnp.dot(a, b, preferred_element_type=F32)


def _slope(h):
    return float(2.0 ** (-(h + 1)))


def _norm_kernel(x_ref, g_ref, o_ref):
    xf = x_ref[...]
    xn = xf * lax.rsqrt(jnp.mean(xf * xf, axis=-1, keepdims=True) + NORM_EPS)
    o_ref[...] = (xn * g_ref[...]).astype(o_ref.dtype)


def rms_norm_rows(x, g, out_dtype):
    n, d = x.shape
    tm = _tile(n, 512)
    return pl.pallas_call(
        _norm_kernel,
        grid=(n // tm,),
        in_specs=[pl.BlockSpec((tm, d), lambda i: (i, 0)), pl.BlockSpec((1, d), lambda i: (0, 0))],
        out_specs=pl.BlockSpec((tm, d), lambda i: (i, 0)),
        out_shape=jax.ShapeDtypeStruct((n, d), out_dtype),
        compiler_params=_cparams("parallel"),
        name="rms_norm",
    )(x, g.reshape(1, d))


def _mm_kernel(x_ref, w_ref, o_ref, *, scale):
    acc = _dot(x_ref[...], w_ref[...])
    if scale != 1.0:
        acc = acc * scale
    o_ref[...] = acc.astype(o_ref.dtype)


def matmul_rows(x, w, out_dtype, *, tn, scale=1.0, tm_target=2048):
    n, k = x.shape
    m = w.shape[1]
    tm = _tile(n, tm_target)
    return pl.pallas_call(
        functools.partial(_mm_kernel, scale=scale),
        grid=(n // tm, m // tn),
        in_specs=[pl.BlockSpec((tm, k), lambda i, j: (i, 0)), pl.BlockSpec((k, tn), lambda i, j: (0, j))],
        out_specs=pl.BlockSpec((tm, tn), lambda i, j: (i, j)),
        out_shape=jax.ShapeDtypeStruct((n, m), out_dtype),
        compiler_params=_cparams("parallel", "parallel"),
        name="proj",
    )(x, w)


def _mm_t_kernel(wt_ref, x_ref, *o_refs):
    ob_ref = o_refs[-1]
    acc = _dot_nt(wt_ref[...], x_ref[...]).reshape(ob_ref.shape)
    if len(o_refs) == 2:
        o_refs[0][...] = acc
    else:
        for j, of_ref in enumerate(o_refs[:-1]):
            of_ref[...] = acc[j]
    ob_ref[...] = acc.astype(ob_ref.dtype)


def kv_project_t(x, wt, batch, seq, split_f32):
    k = x.shape[1]
    tm = _tile(seq, 512, LANES)
    nt = seq // tm
    n_slab = wt.shape[0] // KV_WIDTH
    stacked = pl.BlockSpec((n_slab, None, KV_WIDTH, tm), lambda b, i: (0, b, 0, i))
    if split_f32:
        f_specs = [pl.BlockSpec((None, KV_WIDTH, tm), lambda b, i: (b, 0, i))] * n_slab
        f_shapes = [jax.ShapeDtypeStruct((batch, KV_WIDTH, seq), F32)] * n_slab
    else:
        f_specs = [stacked]
        f_shapes = [jax.ShapeDtypeStruct((n_slab, batch, KV_WIDTH, seq), F32)]
    outs = pl.pallas_call(
        _mm_t_kernel,
        grid=(batch, nt),
        in_specs=[pl.BlockSpec(wt.shape, lambda b, i: (0, 0)), pl.BlockSpec((tm, k), lambda b, i: (b * nt + i, 0))],
        out_specs=f_specs + [stacked],
        out_shape=f_shapes + [jax.ShapeDtypeStruct((n_slab, batch, KV_WIDTH, seq), BF16)],
        compiler_params=_cparams("parallel", "parallel"),
        name="proj_kv_t",
    )(wt, x)
    return (outs[:-1] if split_f32 else outs[0]), outs[-1]


def _chunk_rows(xs_ref, n_chunks):
    return jnp.concatenate([xs_ref[pl.ds(l, n_chunks, stride=CMP_STRIDE), :] for l in range(CMP_STRIDE)], axis=1)


def _halves(chunks, pea_ref, peb_ref, wa_ref, wb_ref):
    a = _dot((chunks + pea_ref[...]).astype(BF16), wa_ref[...])
    b = _dot((chunks + peb_ref[...]).astype(BF16), wb_ref[...])
    return a, b


def _summaries(a, b_next, w2bd):
    return _dot(_silu(a + b_next).astype(BF16), w2bd)


def _shift_up(x):
    n = x.shape[0]
    return pltpu.roll(x, n - 1, 0)


def _transpose_pages(src, xs_ref, n_pages):
    for p in range(n_pages):
        xs_ref[p * LANES:(p + 1) * LANES, :] = src(p).T


def _pool_ab_kernel(xk_ref, xv_ref, kpea, kpeb, kwa, kwb, vpea, vpeb, vwa, vwb, o_ref, xs_scr):
    n_pages = xk_ref.shape[0]
    hid = kwa.shape[1]
    for i, (x_ref, cw) in enumerate(((xk_ref, (kpea, kpeb, kwa, kwb)), (xv_ref, (vpea, vpeb, vwa, vwb)))):
        _transpose_pages(lambda p: x_ref[p], xs_scr, n_pages)
        a, b = _halves(_chunk_rows(xs_scr, n_pages * (LANES // CMP_STRIDE)), *cw)
        o_ref[:, (2 * i) * hid:(2 * i + 1) * hid] = a
        o_ref[:, (2 * i + 1) * hid:(2 * i + 2) * hid] = b


def pool_halves(pool_k_t, pool_v_t, layer, cw_k, cw_v):
    n_pool = pool_k_t.shape[1]
    pg = _tile(n_pool, 64, 1)
    rows = pg * (LANES // CMP_STRIDE)
    hid = cw_k[2].shape[1]
    full = lambda a: pl.BlockSpec(a.shape, lambda i: (0,) * a.ndim)
    page_spec = pl.BlockSpec((None, pg, KV_WIDTH, LANES), lambda i: (layer, i, 0, 0))
    return pl.pallas_call(
        _pool_ab_kernel,
        grid=(n_pool // pg,),
        in_specs=[page_spec, page_spec] + [full(c) for c in (*cw_k, *cw_v)],
        out_specs=pl.BlockSpec((rows, 4 * hid), lambda i: (i, 0)),
        out_shape=jax.ShapeDtypeStruct((n_pool * (LANES // CMP_STRIDE), 4 * hid), F32),
        scratch_shapes=[pltpu.VMEM((pg * LANES, KV_WIDTH), F32)],
        compiler_params=_cparams("parallel"),
        name="pool_halves",
    )(pool_k_t, pool_v_t, *cw_k, *cw_v)


def _pcmp_kernel(kt_ref, vt_ref, kpea, kpeb, kwa, kwb, kw2, vpea, vpeb, vwa, vwb, vw2, featc_ref,
                 kcf_ref, vc2_ref, xs_scr):
    seq = kt_ref.shape[1]
    n_blocks = seq // LANES
    n_chunks = seq // CMP_STRIDE
    half = KV_WIDTH // NSA_KV_HEADS
    for src_ref, (pea, peb, wa, wb, w2), is_key in ((kt_ref, (kpea, kpeb, kwa, kwb, kw2), True),
                                                   (vt_ref, (vpea, vpeb, vwa, vwb, vw2), False)):
        _transpose_pages(lambda p: src_ref[:, p * LANES:(p + 1) * LANES], xs_scr, n_blocks)
        a, b = _halves(_chunk_rows(xs_scr, n_chunks), pea, peb, wa, wb)
        rows = _summaries(a, _shift_up(b), w2[...])
        if is_key:
            rows_t = rows.T.astype(kcf_ref.dtype)
            kcf_ref[0, 0:half, :] = rows_t[0:half]
            kcf_ref[0, half:, :] = featc_ref[...]
            kcf_ref[1, 0:half, :] = featc_ref[...]
            kcf_ref[1, half:, :] = rows_t[half:]
        else:
            lane = lax.broadcasted_iota(jnp.int32, rows.shape, 1)
            vc2_ref[0] = jnp.where(lane < half, rows, 1.0).astype(vc2_ref.dtype)
            vc2_ref[1] = jnp.where(lane >= half, rows, 1.0).astype(vc2_ref.dtype)


def _feature_rows(pos):
    assert pos.max() // SEL_BLOCK < FEAT_ROWS
    f = np.zeros((FEAT_ROWS, pos.shape[0]), np.float32)
    f[1:] = pos[None, :] // SEL_BLOCK == np.arange(1, FEAT_ROWS)[:, None]
    f[0] = pos % SEL_BLOCK
    return f


def prompt_summaries(kt_cmp, vt_cmp, cw_k, w2k, cw_v, w2v):
    batch, _, seq = kt_cmp.shape
    n_c = seq // CMP_STRIDE
    cend = np.minimum(np.arange(n_c) * CMP_STRIDE + (CMP_BLOCK - 1), seq - 1)
    consts = [*cw_k, w2k, *cw_v, w2v, jnp.asarray(_feature_rows(cend), BF16)]
    full = lambda a: pl.BlockSpec(a.shape, lambda b: (0,) * a.ndim)
    return pl.pallas_call(
        _pcmp_kernel,
        grid=(batch,),
        in_specs=[pl.BlockSpec((None, KV_WIDTH, seq), lambda b: (b, 0, 0))] * 2 + [full(c) for c in consts],
        out_specs=[pl.BlockSpec((None, NSA_KV_HEADS, KV_WIDTH, n_c), lambda b: (b, 0, 0, 0)),
                   pl.BlockSpec((None, NSA_KV_HEADS, n_c, KV_WIDTH), lambda b: (b, 0, 0, 0))],
        out_shape=[jax.ShapeDtypeStruct((batch, NSA_KV_HEADS, KV_WIDTH, n_c), BF16),
                   jax.ShapeDtypeStruct((batch, NSA_KV_HEADS, n_c, KV_WIDTH), BF16)],
        scratch_shapes=[pltpu.VMEM((seq, KV_WIDTH), F32)],
        compiler_params=_cparams("parallel"),
        name="prompt_summaries",
    )(kt_cmp, vt_cmp, *consts)


def _rank_select_t(score_t, n_cand, topn):
    idx = lax.broadcasted_iota(jnp.int32, score_t.shape, 0)
    rank = jnp.zeros(score_t.shape, jnp.int32)
    for j in range(n_cand):
        row = score_t[j:j + 1, :]
        rank = rank + jnp.where(idx > j, jnp.where(row >= score_t, 1, 0), jnp.where(row > score_t, 1, 0))
    return jnp.where(rank < topn, 1.0, 0.0)


def _pattn_kernel(q_ref, kcf_ref, vc2_ref, kst_ref, vst_ref, kwt_ref, vwt_ref, feat_ref, sm_ref, ovt_ref,
                  o_ref, kfs_scr, kfw_scr, vs_scr, vw_scr, m_scr, acc_scr, *, n_sel, topn):
    qi = pl.program_id(1)
    t0 = qi * Q_BLOCK
    half = KV_WIDTH // NSA_KV_HEADS

    @pl.when(qi == 0)
    def _():
        ones = jnp.ones((half, kst_ref.shape[1]), BF16)
        for src, dst, fill in ((kst_ref, kfs_scr, feat_ref[...]), (kwt_ref, kfw_scr, feat_ref[...]),
                               (vst_ref, vs_scr, ones), (vwt_ref, vw_scr, ones)):
            dst[0, 0:half, :] = src[0:half, :]
            dst[0, half:, :] = fill
            dst[1, 0:half, :] = fill
            dst[1, half:, :] = src[half:, :]

    q = q_ref[...]
    gates = _sigmoid(sm_ref[...])

    def gate_col(branch, h):
        j = GATE_LANE0 + branch * NSA_HEADS + h
        return gates[:, j:j + 1]

    lane = lax.broadcasted_iota(jnp.int32, (Q_BLOCK, LANES), 1)
    feat_f = jnp.where(lane >= half, lane - half, lane).astype(F32)
    sel_lanes = []

    def own_lanes(h):
        return (lane < half) if h < NSA_GROUP else (lane >= half)

    def q_operand(selected):
        parts = []
        for h in range(NSA_HEADS):
            c = jnp.where(feat_f == 0.0, _slope(h), (_slope(h) * SEL_BLOCK) * feat_f)
            if selected:
                c = c + (sel_lanes[h // NSA_GROUP] - 1.0) * MASK_BIG
            parts.append(jnp.where(own_lanes(h), q[:, h * KV_WIDTH:(h + 1) * KV_WIDTH], c.astype(BF16)))
        return jnp.concatenate(parts, axis=0)

    grp_rows = NSA_GROUP * Q_BLOCK

    def scores(qa, kf):
        return jnp.concatenate([_dot(qa[g * grp_rows:(g + 1) * grp_rows], kf(g)) for g in range(NSA_KV_HEADS)], axis=0)

    def split_sums(h, a):
        own = own_lanes(h)
        total = jnp.where(own, pltpu.roll(a, half, 1), a)
        return jnp.where(own, a / jnp.maximum(total, TINY), 0.0), total

    qa_plain = q_operand(False)

    def compressed_and_select():
        n_c = kcf_ref.shape[2]
        s_all = scores(qa_plain, lambda g: kcf_ref[g])
        tq = lax.broadcasted_iota(jnp.int32, (Q_BLOCK, n_c), 0) + t0
        cend = lax.broadcasted_iota(jnp.int32, (Q_BLOCK, n_c), 1) * CMP_STRIDE + (CMP_BLOCK - 1)
        vis_c = tq >= cend
        lane_tiles = n_c // LANES
        floor_m = jnp.full((Q_BLOCK, LANES), NEG_INF, F32)
        ps = []
        for h in range(NSA_HEADS):
            s = jnp.where(vis_c, s_all[h * Q_BLOCK:(h + 1) * Q_BLOCK], NEG_INF)
            m = jnp.maximum(floor_m, jnp.max(s, axis=-1, keepdims=True))
            ps.append(jnp.where(vis_c, jnp.exp(s - jnp.concatenate([m] * lane_tiles, axis=1)), 0.0))
        p_all = jnp.concatenate([p.astype(BF16) for p in ps], axis=0)
        pv = [_dot(p_all[g * grp_rows:(g + 1) * grp_rows], vc2_ref[g]) for g in range(NSA_KV_HEADS)]
        psum = [None] * NSA_KV_HEADS
        for h in range(NSA_HEADS):
            g = h // NSA_GROUP
            cols = slice(h * KV_WIDTH, (h + 1) * KV_WIDTH)
            o_cmp, total = split_sums(h, pv[g][(h % NSA_GROUP) * Q_BLOCK:(h % NSA_GROUP + 1) * Q_BLOCK])
            p = ps[h] / jnp.concatenate([jnp.maximum(total, TINY)] * lane_tiles, axis=1)
            psum[g] = p if psum[g] is None else psum[g] + p
            o_ref[:, cols] = o_ref[:, cols] + gate_col(0, h) * o_cmp

        s_idx = lax.broadcasted_iota(jnp.int32, (n_sel, Q_BLOCK), 0)
        tq_s = lax.broadcasted_iota(jnp.int32, (n_sel, Q_BLOCK), 1) + t0
        valid = s_idx * SEL_BLOCK <= tq_s
        forced = (s_idx == 0) | (s_idx == tq_s // SEL_BLOCK)
        for g in range(NSA_KV_HEADS):
            hi = psum[g].astype(BF16)
            lo = (psum[g] - hi.astype(F32)).astype(BF16)
            imp_t = _dot_nt(ovt_ref[...], hi) + _dot_nt(ovt_ref[...], lo)
            score_t = jnp.where(valid, imp_t + jnp.where(forced, FORCE_BONUS, 0.0), NEG_INF)
            sel_t = _rank_select_t(score_t, n_sel, topn)
            sel_rows = [sel_t] + ([jnp.zeros((half - n_sel, Q_BLOCK), F32)] if half > n_sel else [])
            other = [jnp.zeros((half, Q_BLOCK), F32)]
            sel_lanes.append(jnp.concatenate(other + sel_rows if g == 0 else sel_rows + other, axis=0).T)

    row_i = lax.broadcasted_iota(jnp.int32, (Q_BLOCK, KEY_BLOCK), 0)
    lane_i = lax.broadcasted_iota(jnp.int32, (Q_BLOCK, KEY_BLOCK), 1)

    def reset():
        m_scr[...] = jnp.full(m_scr.shape, NEG_INF, F32)
        acc_scr[...] = jnp.zeros(acc_scr.shape, F32)

    def flash_step(kb, qa, kf_ref, v_scr, masked):
        k0 = pl.multiple_of(kb * KEY_BLOCK, KEY_BLOCK)
        s_all = scores(qa, lambda g: kf_ref[g, :, pl.ds(k0, KEY_BLOCK)])
        if masked:
            dist = (t0 - k0) + (row_i - lane_i)
            keep = jnp.where(dist >= 0, dist, WINDOW) < WINDOW
        ps = []
        alphas = []
        for h in range(NSA_HEADS):
            rows = slice(h * Q_BLOCK, (h + 1) * Q_BLOCK)
            s = s_all[rows]
            if masked:
                s = jnp.where(keep, s, NEG_INF)
            m_prev = m_scr[rows]
            m_new = jnp.maximum(m_prev, jnp.max(s, axis=-1, keepdims=True))
            p = jnp.exp(s - jnp.concatenate([m_new] * (KEY_BLOCK // LANES), axis=1))
            if masked:
                p = jnp.where(keep, p, 0.0)
            alpha = jnp.exp(m_prev - m_new)
            m_scr[rows] = m_new
            ps.append(p.astype(BF16))
            alphas.append(alpha)
        p_all = jnp.concatenate(ps, axis=0)
        pv = [_dot_nt(p_all[g * grp_rows:(g + 1) * grp_rows], v_scr[g, :, pl.ds(k0, KEY_BLOCK)])
              for g in range(NSA_KV_HEADS)]
        acc_scr[...] = jnp.concatenate(alphas, axis=0) * acc_scr[...] + jnp.concatenate(pv, axis=0)

    def finish(branch, first):
        for h in range(NSA_HEADS):
            rows = slice(h * Q_BLOCK, (h + 1) * Q_BLOCK)
            cols = slice(h * KV_WIDTH, (h + 1) * KV_WIDTH)
            o_b, _ = split_sums(h, acc_scr[rows])
            gated = gate_col(branch, h) * o_b
            o_ref[:, cols] = gated if first else o_ref[:, cols] + gated

    def loop(lo, hi, qa, kf_ref, v_scr):
        n = jnp.maximum(hi - lo, 0)

        def pair(i, c):
            flash_step(lo + 2 * i, qa, kf_ref, v_scr, False)
            flash_step(lo + 2 * i + 1, qa, kf_ref, v_scr, False)
            return c

        lax.fori_loop(0, n // 2, pair, 0)

        @pl.when(n % 2 == 1)
        def _():
            flash_step(hi - 1, qa, kf_ref, v_scr, False)

    per_key_block = KEY_BLOCK // Q_BLOCK
    kb_diag = qi // per_key_block
    kb_lo = jnp.maximum(qi - WINDOW // Q_BLOCK, 0) // per_key_block

    reset()

    @pl.when(kb_lo < kb_diag)
    def _():
        flash_step(kb_lo, qa_plain, kfw_scr, vw_scr, True)

    loop(kb_lo + 1, kb_diag, qa_plain, kfw_scr, vw_scr)
    flash_step(kb_diag, qa_plain, kfw_scr, vw_scr, True)
    finish(2, True)
    compressed_and_select()
    reset()
    qa = q_operand(True)
    loop(0, kb_diag, qa, kfs_scr, vs_scr)
    flash_step(kb_diag, qa, kfs_scr, vs_scr, True)
    finish(1, False)


def prompt_attention(qp, kcf, vc2, kvt_b, r, batch, seq):
    assert seq % KEY_BLOCK == 0 and min(SEL_TOPN, seq // SEL_BLOCK) >= 2
    nqb = seq // Q_BLOCK
    n_c = kcf.shape[3]
    assert n_c % LANES == 0
    n_sel = seq // SEL_BLOCK
    topn = min(SEL_TOPN, n_sel)
    cs = np.arange(n_c)[:, None] * CMP_STRIDE
    ss = np.arange(n_sel)[None, :] * SEL_BLOCK
    ov = np.clip(np.minimum(cs + CMP_BLOCK, ss + SEL_BLOCK) - np.maximum(cs, ss), 0, None) / CMP_BLOCK
    ov[(seq - CMP_BLOCK) // CMP_STRIDE + 1:] = 0.0
    kv_spec = lambda idx: pl.BlockSpec((None, None, KV_WIDTH, seq), lambda b, i: (idx, b, 0, 0))
    return pl.pallas_call(
        functools.partial(_pattn_kernel, n_sel=n_sel, topn=topn),
        grid=(batch, nqb),
        in_specs=[
            pl.BlockSpec((Q_BLOCK, PADDED_Q), lambda b, i: (b * nqb + i, 0)),
            pl.BlockSpec((None, NSA_KV_HEADS, KV_WIDTH, n_c), lambda b, i: (b, 0, 0, 0)),
            pl.BlockSpec((None, NSA_KV_HEADS, n_c, KV_WIDTH), lambda b, i: (b, 0, 0, 0)),
            kv_spec(2), kv_spec(3), kv_spec(4), kv_spec(5),
            pl.BlockSpec((FEAT_ROWS, seq), lambda b, i: (0, 0)),
            pl.BlockSpec((Q_BLOCK, LANES), lambda b, i: (b * nqb + i, R_SMALL // LANES)),
            pl.BlockSpec((n_sel, n_c), lambda b, i: (0, 0)),
        ],
        out_specs=pl.BlockSpec((Q_BLOCK, PADDED_Q), lambda b, i: (b * nqb + i, 0)),
        out_shape=jax.ShapeDtypeStruct((batch * seq, PADDED_Q), F32),
        scratch_shapes=[
            pltpu.VMEM((NSA_KV_HEADS, KV_WIDTH, seq), BF16),
            pltpu.VMEM((NSA_KV_HEADS, KV_WIDTH, seq), BF16),
            pltpu.VMEM((NSA_KV_HEADS, KV_WIDTH, seq), BF16),
            pltpu.VMEM((NSA_KV_HEADS, KV_WIDTH, seq), BF16),
            pltpu.VMEM((NSA_HEADS * Q_BLOCK, LANES), F32),
            pltpu.VMEM((NSA_HEADS * Q_BLOCK, KV_WIDTH), F32),
        ],
        compiler_params=_cparams("parallel", "arbitrary"),
        name="prompt_nsa",
    )(qp, kcf, vc2, kvt_b, kvt_b, kvt_b, kvt_b, jnp.asarray(_feature_rows(np.arange(seq)), BF16), r,
      jnp.asarray(ov.T, BF16))


def _gla_log_decay(sm, aup_ref, ab_ref):
    z = _dot(sm.astype(BF16), aup_ref[...]) + ab_ref[...]
    return _log_sigmoid(z) / GLA_GATE_TAU


def _gla_head_norm(o, gg_ref):
    on = o * lax.rsqrt(jnp.mean(o * o, axis=-1, keepdims=True) + NORM_EPS)
    return on * gg_ref[...]


def _pgla_kernel(qg_ref, kg_ref, vg_ref, sm_ref, aup_ref, ab_ref, gg_ref, og_ref, sfin_ref, st_scr, *, n_tiles):
    ti = pl.program_id(1)
    tt = qg_ref.shape[0]
    width = qg_ref.shape[1]

    @pl.when(ti == 0)
    def _():
        st_scr[...] = jnp.zeros(st_scr.shape, F32)

    la = _gla_log_decay(sm_ref[...], aup_ref, ab_ref)
    rin = lax.broadcasted_iota(jnp.int32, la.shape, 0)
    bg = la
    sh = 1
    while sh < tt:
        bg = bg + jnp.where(rin >= sh, pltpu.roll(bg, sh, 0), 0.0)
        sh *= 2

    def row_of_block(x, block, row):
        x3 = x.reshape(tt // block, block, width)
        return jnp.broadcast_to(x3[:, row:row + 1, :], x3.shape).reshape(tt, width)

    n_chunks = tt // GLA_CHUNK
    ends = bg.reshape(n_chunks, GLA_CHUNK, width)[:, GLA_CHUNK - 1:GLA_CHUNK, :]
    prev_end = jnp.concatenate([jnp.zeros((1, 1, width), F32), ends[:n_chunks - 1]], axis=0)
    b = (bg.reshape(n_chunks, GLA_CHUNK, width) - prev_end).reshape(tt, width)
    bg_end = bg[tt - 1:tt, :]
    scales = [(jnp.exp(b), jnp.exp(-b))]
    block = 2 * GLA_CHUNK
    while block <= tt:
        mid = row_of_block(bg, block, block // 2 - 1)
        scales.append((jnp.exp(jnp.minimum(bg - mid, 0.0)), jnp.exp(jnp.minimum(mid - bg, 0.0))))
        block *= 2
    e_in = jnp.exp(bg)
    e_out = jnp.exp(bg_end - bg)

    r_i = lax.broadcasted_iota(jnp.int32, (tt, tt), 0)
    c_i = lax.broadcasted_iota(jnp.int32, (tt, tt), 1)
    level = jnp.full((tt, tt), len(scales) - 1, jnp.int32)
    block = tt // 2
    lv = len(scales) - 2
    while block >= GLA_CHUNK:
        level = jnp.where(r_i // block == c_i // block, lv, level)
        block //= 2
        lv -= 1
    level = jnp.where(c_i <= r_i, level, -1)

    for h in range(GLA_HEADS):
        ks = slice(h * GLA_DK, (h + 1) * GLA_DK)
        vs = slice(h * GLA_DV, (h + 1) * GLA_DV)
        q = qg_ref[:, ks] * (GLA_DK ** -0.5)
        k = kg_ref[:, ks]
        v = vg_ref[:, vs].astype(BF16)
        a = jnp.zeros((tt, tt), F32)
        for lv, (sq, sk) in enumerate(scales):
            a_lv = _dot_nt((q * sq[:, ks]).astype(BF16), (k * sk[:, ks]).astype(BF16))
            a = jnp.where(level == lv, a_lv, a)
        st = st_scr[h]
        o = _dot(a.astype(BF16), v) + _dot_nt((q * e_in[:, ks]).astype(BF16), st.astype(BF16))
        kv_t = lax.dot_general(v, (k * e_out[:, ks]).astype(BF16), (((0,), (0,)), ((), ())),
                               preferred_element_type=F32)
        st_scr[h] = st * jnp.exp(bg_end[:, ks]) + kv_t
        og_ref[:, vs] = _gla_head_norm(o, gg_ref)

    @pl.when(ti == n_tiles - 1)
    def _():
        for h in range(GLA_HEADS):
            sfin_ref[h] = st_scr[h].T


def prompt_gla(r, aup, ab, gg, batch, seq):
    tt = _tile(seq, 256, GLA_CHUNK)
    nt = seq // tt
    row = lambda b, i: b * nt + i
    return pl.pallas_call(
        functools.partial(_pgla_kernel, n_tiles=nt),
        grid=(batch, nt),
        in_specs=[
            pl.BlockSpec((tt, GLA_KEY_WIDTH), lambda b, i: (row(b, i), R_QG // GLA_KEY_WIDTH)),
            pl.BlockSpec((tt, GLA_KEY_WIDTH), lambda b, i: (row(b, i), R_KG // GLA_KEY_WIDTH)),
            pl.BlockSpec((tt, GLA_VAL_WIDTH), lambda b, i: (row(b, i), R_VG // GLA_VAL_WIDTH)),
            pl.BlockSpec((tt, LANES), lambda b, i: (row(b, i), R_SMALL // LANES)),
            pl.BlockSpec(aup.shape, lambda b, i: (0, 0)),
            pl.BlockSpec(ab.shape, lambda b, i: (0, 0)),
            pl.BlockSpec(gg.shape, lambda b, i: (0, 0)),
        ],
        out_specs=[
            pl.BlockSpec((tt, GLA_VAL_WIDTH), lambda b, i: (row(b, i), 0)),
            pl.BlockSpec((None, GLA_HEADS, GLA_DK, GLA_DV), lambda b, i: (b, 0, 0, 0)),
        ],
        out_shape=[
            jax.ShapeDtypeStruct((batch * seq, GLA_VAL_WIDTH), F32),
            jax.ShapeDtypeStruct((batch, GLA_HEADS, GLA_DK, GLA_DV), F32),
        ],
        scratch_shapes=[pltpu.VMEM((GLA_HEADS, GLA_DV, GLA_DK), F32)],
        compiler_params=_cparams("parallel", "arbitrary"),
        name="prompt_gla",
    )(r, r, r, r, aup, ab, gg)


def _sgla_kernel(qg_ref, kg_ref, vg_ref, sm_ref, aup_ref, ab_ref, gg_ref, s_ref, all_layers_ref, og_ref, snew_ref,
                 *, ds):
    del all_layers_ref
    rows = qg_ref.shape[0]
    nb = rows // ds
    la = _gla_log_decay(sm_ref[...], aup_ref, ab_ref)
    ri = lax.broadcasted_iota(jnp.int32, la.shape, 0) % ds
    b = la
    sh = 1
    while sh < ds:
        b = b + jnp.where(ri >= sh, pltpu.roll(b, sh, 0), 0.0)
        sh *= 2
    b_last = b
    for d in range(1, ds):
        b_last = jnp.where(ri == ds - 1 - d, pltpu.roll(b, rows - d, 0), b_last)
    e_b = jnp.exp(b)
    e_nb = jnp.exp(-b)
    e_tail = jnp.exp(b_last - b)
    e_last = jnp.exp(b_last)

    r_i = lax.broadcasted_iota(jnp.int32, (rows, rows), 0)
    c_i = lax.broadcasted_iota(jnp.int32, (rows, rows), 1)
    causal = (r_i // ds == c_i // ds) & (c_i <= r_i)
    row_b = lax.broadcasted_iota(jnp.int32, (rows, GLA_DV), 0) // ds

    for h in range(GLA_HEADS):
        ks = slice(h * GLA_DK, (h + 1) * GLA_DK)
        vs = slice(h * GLA_DV, (h + 1) * GLA_DV)
        q = qg_ref[:, ks] * (GLA_DK ** -0.5)
        k = kg_ref[:, ks]
        v = vg_ref[:, vs]
        q_dec = (q * e_b[:, ks]).astype(BF16)
        k_inv = (k * e_nb[:, ks]).astype(BF16)
        k_tail_t = (k * e_tail[:, ks]).T
        e_last_t = e_last[:, ks].T
        a = jnp.where(causal, _dot_nt(q_dec, k_inv), 0.0)
        o = _dot(a.astype(BF16), v.astype(BF16))
        for bb in range(nb):
            s_prev = s_ref[bb, h]
            o_inter = _dot(q_dec, s_prev.astype(BF16))
            o = o + jnp.where(row_b == bb, o_inter, 0.0)
            s_new = e_last_t[:, bb * ds:bb * ds + 1] * s_prev
            for j in range(ds):
                rr = bb * ds + j
                s_new = s_new + k_tail_t[:, rr:rr + 1] * v[rr:rr + 1, :]
            snew_ref[bb, h] = s_new
        og_ref[:, vs] = _gla_head_norm(o, gg_ref)


def sample_gla(r, aup, ab, gg, state, new_states, layer, dec_batch, ds):
    nb = _tile(dec_batch, 8, 1)
    rows = nb * ds
    return pl.pallas_call(
        functools.partial(_sgla_kernel, ds=ds),
        grid=(dec_batch // nb,),
        in_specs=[
            pl.BlockSpec((rows, GLA_KEY_WIDTH), lambda i: (i, R_QG // GLA_KEY_WIDTH)),
            pl.BlockSpec((rows, GLA_KEY_WIDTH), lambda i: (i, R_KG // GLA_KEY_WIDTH)),
            pl.BlockSpec((rows, GLA_VAL_WIDTH), lambda i: (i, R_VG // GLA_VAL_WIDTH)),
            pl.BlockSpec((rows, LANES), lambda i: (i, R_SMALL // LANES)),
            pl.BlockSpec(aup.shape, lambda i: (0, 0)),
            pl.BlockSpec(ab.shape, lambda i: (0, 0)),
            pl.BlockSpec(gg.shape, lambda i: (0, 0)),
            pl.BlockSpec((None, nb, GLA_HEADS, GLA_DK, GLA_DV), lambda i: (layer, i, 0, 0, 0)),
            pl.BlockSpec(memory_space=pl.ANY),
        ],
        out_specs=[
            pl.BlockSpec((rows, GLA_VAL_WIDTH), lambda i: (i, 0)),
            pl.BlockSpec((None, nb, GLA_HEADS, GLA_DK, GLA_DV), lambda i: (layer, i, 0, 0, 0)),
        ],
        out_shape=[
            jax.ShapeDtypeStruct((dec_batch * ds, GLA_VAL_WIDTH), F32),
            jax.ShapeDtypeStruct(new_states.shape, F32),
        ],
        input_output_aliases={8: 1},
        compiler_params=_cparams("parallel"),
        name="sample_gla",
    )(r, r, r, r, aup, ab, gg, state, new_states)


def _mix_kernel(x_ref, on_ref, za_ref, og_ref, zb_ref, mg_ref, wba_ref, wbb_ref, wo_ref, fg_ref, *o_refs):
    f32 = lambda ref: ref[...].astype(F32)
    pa = _dot((on_ref[...] * _silu(f32(za_ref))).astype(BF16), wba_ref[...])
    pb = _dot((og_ref[...] * _silu(f32(zb_ref))).astype(BF16), wbb_ref[...])
    mg = f32(mg_ref)
    mix = _sigmoid(mg[:, :D_MODEL]) * pa + _sigmoid(mg[:, D_MODEL:]) * pb
    y = x_ref[...] + _dot(mix.astype(BF16), wo_ref[...])
    o_refs[0][...] = y
    if len(o_refs) > 1:
        yn = y * lax.rsqrt(jnp.mean(y * y, axis=-1, keepdims=True) + NORM_EPS)
        o_refs[1][...] = yn * fg_ref[...]


def mixer_output(x, o_nsa, rz, o_gla, wba, wbb, wo, final_g, with_final_norm):
    n = x.shape[0]
    tm = _tile(n, 256)
    row = lambda w: pl.BlockSpec((tm, w), lambda i: (i, 0))
    wspec = lambda w: pl.BlockSpec(w.shape, lambda i: (0, 0))
    n_out = 2 if with_final_norm else 1
    return pl.pallas_call(
        _mix_kernel,
        grid=(n // tm,),
        in_specs=[
            row(D_MODEL), row(PADDED_Q),
            pl.BlockSpec((tm, PADDED_Q), lambda i: (i, Z_ZA // PADDED_Q)),
            row(GLA_VAL_WIDTH),
            pl.BlockSpec((tm, GLA_VAL_WIDTH), lambda i: (i, Z_ZB // GLA_VAL_WIDTH)),
            pl.BlockSpec((tm, 2 * D_MODEL), lambda i: (i, Z_MG // (2 * D_MODEL))),
            wspec(wba), wspec(wbb), wspec(wo), wspec(final_g),
        ],
        out_specs=[row(D_MODEL)] * n_out,
        out_shape=[jax.ShapeDtypeStruct((n, D_MODEL), F32)] * n_out,
        compiler_params=_cparams("parallel"),
        name="mixer_output",
    )(x, o_nsa, rz, o_gla, rz, rz, wba, wbb, wo, final_g)


def _select_blocks(score, lane_idx, n_cand, topn):
    rank = jnp.zeros(score.shape, jnp.int32)
    for j in range(n_cand):
        col = score[:, j:j + 1]
        rank = rank + jnp.where(col > score, 1, 0) + jnp.where(col == score, jnp.where(lane_idx > j, 1, 0), 0)
    return jnp.where(rank < topn, 1.0, 0.0)


def _softmax_two(parts, keeps):
    m = functools.reduce(jnp.maximum, [jnp.max(s, axis=-1, keepdims=True) for s in parts])
    ps = [jnp.where(kp, jnp.exp(s - m), 0.0) for s, kp in zip(parts, keeps)]
    total = functools.reduce(lambda a, c: a + c, [jnp.sum(p, axis=-1, keepdims=True) for p in ps])
    return ps, jnp.maximum(total, TINY)


def _smain_kernel(pt_ref, q_ref, gate_ref, slope_ref, ab_hbm, kt_hbm, vt_hbm, kw_ref, vw_ref, new_ref,
                  w2k_ref, w2v_ref, ov_ref, exp_ref, exn_ref, kw_all_ref, vw_all_ref, o_ref, kw_out_ref, vw_out_ref,
                  ab_buf, kt_buf, vt_buf, sems, *, layer, n_pages, past, ds, n_sel, topn):
    del kw_all_ref, vw_all_ref
    step = pl.program_id(0)
    buf_slot = step % 2
    rows_per_step = q_ref.shape[0]

    def page_copies(st, sl):
        cps = []
        for r in range(rows_per_step):
            for j in range(n_pages):
                pg = pt_ref[st * rows_per_step + r, j]
                cps.append(pltpu.make_async_copy(ab_hbm.at[pg], ab_buf.at[sl, r, j], sems.at[sl, 0]))
                cps.append(pltpu.make_async_copy(kt_hbm.at[layer, pg], kt_buf.at[sl, r, j], sems.at[sl, 1]))
                cps.append(pltpu.make_async_copy(vt_hbm.at[layer, pg], vt_buf.at[sl, r, j], sems.at[sl, 2]))
        return cps

    @pl.when(step == 0)
    def _():
        for cp in page_copies(0, 0):
            cp.start()

    @pl.when(step + 1 < pl.num_programs(0))
    def _():
        for cp in page_copies(step + 1, 1 - buf_slot):
            cp.start()

    for cp in page_copies(step, buf_slot):
        cp.wait()

    active = [_sample_row(step * rows_per_step + r, q_ref.at[r], gate_ref.at[r], slope_ref,
                          ab_buf.at[buf_slot, r], kt_buf.at[buf_slot, r], vt_buf.at[buf_slot, r],
                          kw_ref.at[r], vw_ref.at[r], new_ref, w2k_ref, w2v_ref, ov_ref, exp_ref, exn_ref,
                          o_ref.at[r], kw_out_ref.at[r], vw_out_ref.at[r],
                          n_pages=n_pages, past=past, ds=ds, n_sel=n_sel, topn=topn)
              for r in range(rows_per_step)]
    while active:
        active = [row for row in active if next(row, True) is None]


def _sample_row(b, q_ref, gate_ref, slope_ref, ab_ref, kt_ref, vt_ref, kw_ref, vw_ref, new_ref,
                w2k_ref, w2v_ref, ov_ref, exp_ref, exn_ref, o_ref, kw_out_ref, vw_out_ref,
                *, n_pages, past, ds, n_sel, topn):
    rows = q_ref.shape[0]
    grp_rows = NSA_KV_HEADS * ds
    n_new = new_ref.shape[2]
    b_loc = b % (n_new // ds)
    kt_refs = [kt_ref.at[j] for j in range(n_pages)]
    vt_refs = [vt_ref.at[j] for j in range(n_pages)]

    qs = q_ref[...]
    slope = slope_ref[:, 0:1]
    ab = jnp.concatenate([ab_ref[j] for j in range(n_pages)], axis=0)
    hid = ab.shape[1] // 4
    kc = _summaries(ab[:, 0:hid], _shift_up(ab[:, hid:2 * hid]), w2k_ref[...]).astype(BF16)
    vc = _summaries(ab[:, 2 * hid:3 * hid], _shift_up(ab[:, 3 * hid:]), w2v_ref[...]).astype(BF16)
    n_c = kc.shape[0]

    def irow(shape):
        return lax.broadcasted_iota(jnp.int32, shape, 0) % ds

    yield
    dist_c = (past + irow((rows, n_c))) - (lax.broadcasted_iota(jnp.int32, (rows, n_c), 1) * CMP_STRIDE
                                          + (CMP_BLOCK - 1))
    vis_c = dist_c >= 0
    s_c = jnp.where(vis_c, _dot_nt(qs, kc) - slope * dist_c.astype(F32), NEG_INF)
    m_c = jnp.max(s_c, axis=-1, keepdims=True)
    p_c = jnp.where(vis_c, jnp.exp(s_c - m_c), 0.0)
    p_c = p_c / jnp.maximum(jnp.sum(p_c, axis=-1, keepdims=True), TINY)
    o_cmp = _dot(p_c.astype(BF16), vc)

    yield
    hi = p_c.astype(BF16)
    lo = (p_c - hi.astype(F32)).astype(BF16)
    imp_h = _dot(hi, ov_ref[...]) + _dot(lo, ov_ref[...])
    imp = functools.reduce(lambda a, c: a + c,
                           [imp_h[r * grp_rows:(r + 1) * grp_rows] for r in range(NSA_GROUP)])
    s_idx = lax.broadcasted_iota(jnp.int32, imp.shape, 1)
    qpos = past + irow(imp.shape)
    valid = (s_idx < n_sel) & (s_idx * SEL_BLOCK <= qpos)
    forced = (s_idx == 0) | (s_idx == qpos // SEL_BLOCK)
    score = jnp.where(valid, imp + jnp.where(forced, FORCE_BONUS, 0.0), NEG_INF)
    sel = _select_blocks(score, s_idx, n_sel, topn)
    sel = jnp.concatenate([sel] * NSA_GROUP, axis=0).astype(BF16)
    keep_past = _dot(sel, exp_ref[...]) > 0.5
    sel_new = _dot(sel, exn_ref[...]) > 0.5

    yield
    new_lane = lax.broadcasted_iota(jnp.int32, (rows, n_new), 1)
    dist_n = irow((rows, n_new)) - new_lane % ds
    mine = (new_lane // ds == b_loc) & (dist_n >= 0)
    bias_n = slope * dist_n.astype(F32)

    s_p = jnp.concatenate([_dot(qs, r[...].astype(BF16)) for r in kt_refs], axis=1)
    dist_p = (past + irow(s_p.shape)) - lax.broadcasted_iota(jnp.int32, s_p.shape, 1)
    s_p = jnp.where(keep_past, s_p - slope * dist_p.astype(F32), NEG_INF)
    keep_n = mine & sel_new
    s_n = jnp.where(keep_n, _dot(qs, new_ref[2].astype(BF16)) - bias_n, NEG_INF)
    (p_p, p_n), total = _softmax_two([s_p, s_n], [keep_past, keep_n])
    acc = _dot_nt(p_n.astype(BF16), new_ref[3].astype(BF16))
    for j, r in enumerate(vt_refs):
        acc = acc + _dot_nt(p_p[:, j * LANES:(j + 1) * LANES].astype(BF16), r[...].astype(BF16))
    o_slc = acc / total

    yield
    kbuf = kw_ref[...]
    vbuf = vw_ref[...]
    w_buf = kbuf.shape[1]
    dist_w = (w_buf + irow((rows, w_buf))) - lax.broadcasted_iota(jnp.int32, (rows, w_buf), 1)
    keep_w = dist_w < WINDOW
    s_w = jnp.where(keep_w, _dot(qs, kbuf.astype(BF16)) - slope * dist_w.astype(F32), NEG_INF)
    s_wn = jnp.where(mine, _dot(qs, new_ref[4].astype(BF16)) - bias_n, NEG_INF)
    (p_w, p_wn), total_w = _softmax_two([s_w, s_wn], [keep_w, mine])
    o_win = (_dot_nt(p_w.astype(BF16), vbuf.astype(BF16)) + _dot_nt(p_wn.astype(BF16), new_ref[5].astype(BF16))) / total_w

    gates = _sigmoid(gate_ref[...])
    o_ref[...] = gates[:, 0:1] * o_cmp + gates[:, 1:2] * o_slc + gates[:, 2:3] * o_win

    yield
    lane_new = lax.broadcasted_iota(jnp.int32, (KV_WIDTH, n_new), 1)
    shift = (n_new - ds) - b_loc * ds
    for buf, slab, out_ref in ((kbuf, 4, kw_out_ref), (vbuf, 5, vw_out_ref)):
        rolled = pltpu.roll(buf, w_buf - ds, 1)
        own_last = pltpu.roll(new_ref[slab], shift, 1)
        out_ref[:, 0:w_buf - n_new] = rolled[:, 0:w_buf - n_new]
        out_ref[:, w_buf - n_new:w_buf] = jnp.where(lane_new >= n_new - ds, own_last, rolled[:, w_buf - n_new:w_buf])


def sample_attention(page_table, q_rows, gate_rows, ab_pages, slc_pools_t, win_bufs_t, new_t, new_windows, w2k, w2v,
                     layer, past, ds):
    assert past % SEL_BLOCK == 0 and past % LANES == 0
    dec_batch, n_pages = page_table.shape
    rows = q_rows.shape[1]
    w_buf = win_bufs_t[0].shape[3]
    n_new = min(LANES, dec_batch * ds)
    assert (dec_batch * ds) % n_new == 0 and n_new % ds == 0 and w_buf >= n_new
    n_c = n_pages * (LANES // CMP_STRIDE)
    n_cmp = (past + ds - CMP_BLOCK) // CMP_STRIDE + 1
    n_sel = -(-(past + ds) // SEL_BLOCK)
    topn = min(SEL_TOPN, n_sel)
    assert n_sel <= LANES and ds <= CMP_STRIDE
    cs = np.arange(n_c)[:, None] * CMP_STRIDE
    ss = np.arange(LANES)[None, :] * SEL_BLOCK
    ov = np.clip(np.minimum(cs + CMP_BLOCK, ss + SEL_BLOCK) - np.maximum(cs, ss), 0, None) / CMP_BLOCK
    ov[n_cmp:] = 0.0
    ov[:, n_sel:] = 0.0
    ex_past = (np.arange(past)[None, :] // SEL_BLOCK == np.arange(LANES)[:, None]).astype(np.float32)
    ex_new = ((past + np.arange(n_new) % ds)[None, :] // SEL_BLOCK == np.arange(LANES)[:, None]).astype(np.float32)
    head = np.arange(rows) // (NSA_KV_HEADS * ds) + NSA_GROUP * ((np.arange(rows) // ds) % NSA_KV_HEADS)
    slope = np.broadcast_to((2.0 ** -(head + 1.0))[:, None], (rows, LANES)).astype(np.float32)

    rps = max(r for r in (2, 1) if dec_batch % r == 0 and (n_new // ds) % r == 0)
    const = lambda shape: pl.BlockSpec(shape, lambda s, pt: (0,) * len(shape))
    in_specs = [pl.BlockSpec((rps, rows, LANES), lambda s, pt: (s, 0, 0)),
                pl.BlockSpec((rps, rows, LANES), lambda s, pt: (s, 0, 0)),
                const((rows, LANES))]
    operands = [q_rows, gate_rows, jnp.asarray(slope)]
    in_specs += [pl.BlockSpec(memory_space=pl.ANY)] * 3
    operands += [ab_pages, slc_pools_t[0], slc_pools_t[1]]
    for arr in win_bufs_t:
        in_specs.append(pl.BlockSpec((None, rps, KV_WIDTH, w_buf), lambda s, pt: (layer, s, 0, 0)))
        operands.append(arr)
    steps_per_tile = n_new // ds // rps
    in_specs.append(pl.BlockSpec((6, None, KV_WIDTH, n_new), lambda s, pt: (0, 0, 0, s // steps_per_tile)))
    operands.append(new_t)
    consts = [w2k, w2v, jnp.asarray(ov, BF16), jnp.asarray(ex_past, BF16), jnp.asarray(ex_new, BF16)]
    in_specs += [const(c.shape) for c in consts]
    operands += consts
    in_specs += [pl.BlockSpec(memory_space=pl.ANY)] * 2
    operands += list(new_windows)
    aliases = {1 + len(operands) - 2: 1, 1 + len(operands) - 1: 2}
    buf_out = pl.BlockSpec((None, rps, KV_WIDTH, w_buf), lambda s, pt: (layer, s, 0, 0))
    return pl.pallas_call(
        functools.partial(_smain_kernel, layer=layer, n_pages=n_pages, past=past, ds=ds, n_sel=n_sel, topn=topn),
        grid_spec=pltpu.PrefetchScalarGridSpec(
            num_scalar_prefetch=1,
            grid=(dec_batch // rps,),
            in_specs=in_specs,
            out_specs=[pl.BlockSpec((rps, rows, LANES), lambda s, pt: (s, 0, 0)), buf_out, buf_out],
            scratch_shapes=[pltpu.VMEM((2, rps, n_pages) + ab_pages.shape[1:], F32),
                            pltpu.VMEM((2, rps, n_pages, KV_WIDTH, LANES), F32),
                            pltpu.VMEM((2, rps, n_pages, KV_WIDTH, LANES), F32),
                            pltpu.SemaphoreType.DMA((2, 3))],
        ),
        out_shape=[jax.ShapeDtypeStruct((dec_batch, rows, LANES), F32),
                   jax.ShapeDtypeStruct(new_windows[0].shape, F32),
                   jax.ShapeDtypeStruct(new_windows[1].shape, F32)],
        input_output_aliases=aliases,
        compiler_params=_cparams("arbitrary"),
        name="sample_nsa",
    )(page_table, *operands)


def _pad_heads_cols(w):
    k = w.shape[0]
    w4 = w.reshape(k, NSA_KV_HEADS, NSA_GROUP, HEAD_DIM)
    eye = jnp.eye(NSA_KV_HEADS, dtype=w.dtype)
    return jnp.einsum("kgrd,gp->kgrpd", w4, eye).reshape(k, PADDED_Q)


def _layer_weights(w_in, pk_pe, pk_w1, pk_w2, pv_pe, pv_w1, pv_w2, a_up, a_b, gla_g, w_ba, w_bb, w_out):
    o = _OFF
    col = lambda i: w_in[:, o[i]:o[i + 1]]
    small = jnp.concatenate([col(2), col(7), jnp.zeros((D_MODEL, LANES - 3 * NSA_HEADS - GLA_GATE_RANK), F32)], axis=1)
    w_q = _pad_heads_cols(col(0)).astype(BF16)
    w_kv_t = col(1).T.astype(BF16)
    w_g = jnp.concatenate([col(4), col(5), col(6), small], axis=1).astype(BF16)
    w_z = jnp.concatenate([_pad_heads_cols(col(3)), col(8), col(9)], axis=1).astype(BF16)
    half = CMP_BLOCK // 2

    def cmp_weights(pe, w1, w2):
        eye = jnp.eye(NSA_KV_HEADS, dtype=F32)
        w1h = w1.reshape(2, half, HEAD_DIM, CMP_HIDDEN)
        big = lambda w: jnp.einsum("ldh,gk->lgdkh", w, eye).reshape(half * KV_WIDTH, NSA_KV_HEADS * CMP_HIDDEN)
        peh = pe.reshape(2, half, 1, HEAD_DIM)
        pe_row = lambda p: jnp.broadcast_to(p, (half, NSA_KV_HEADS, HEAD_DIM)).reshape(1, half * KV_WIDTH)
        w2bd = jnp.einsum("hd,gk->ghkd", w2, eye).reshape(NSA_KV_HEADS * CMP_HIDDEN, KV_WIDTH)
        return (pe_row(peh[0]), pe_row(peh[1]), big(w1h[0]).astype(BF16), big(w1h[1]).astype(BF16)), w2bd.astype(BF16)

    cw_k, w2k = cmp_weights(pk_pe, pk_w1, pk_w2)
    cw_v, w2v = cmp_weights(pv_pe, pv_w1, pv_w2)
    aup = jnp.zeros((LANES, GLA_KEY_WIDTH), F32).at[ALOW_LANE0:ALOW_LANE0 + GLA_GATE_RANK].set(a_up).astype(BF16)
    w_ba_p = _pad_heads_cols(w_ba.T).T.astype(BF16)
    return dict(w_q=w_q, w_kv_t=w_kv_t, w_g=w_g, w_z=w_z, cw_k=cw_k, cw_v=cw_v, w2k=w2k, w2v=w2v, aup=aup,
                ab=a_b.reshape(1, GLA_KEY_WIDTH), gg=gla_g.reshape(1, GLA_DV),
                w_ba=w_ba_p, w_bb=w_bb.astype(BF16), w_out=w_out.astype(BF16))


def _project(x, g_norm, lw, batch, seq, split_f32):
    h = rms_norm_rows(x, g_norm, BF16)
    qp = matmul_rows(h, lw["w_q"], BF16, tn=512, scale=HEAD_DIM ** -0.5)
    kvt_f, kvt_b = kv_project_t(h, lw["w_kv_t"], batch, seq, split_f32)
    rg = matmul_rows(h, lw["w_g"], F32, tn=RG_WIDTH, tm_target=1024)
    rz = matmul_rows(h, lw["w_z"], BF16, tn=1024)
    return qp, kvt_f, kvt_b, rg, rz


def _tokens_minor(cache):
    lead = cache.shape[:-3]
    n = cache.ndim
    perm = tuple(range(n - 3)) + (n - 2, n - 1, n - 3)
    return cache.transpose(perm).reshape(lead + (KV_WIDTH, cache.shape[-3]))


def _tokens_major(x_t):
    lead = x_t.shape[:-2]
    n = len(lead)
    x5 = x_t.reshape(lead + (NSA_KV_HEADS, HEAD_DIM, x_t.shape[-1]))
    return x5.transpose(tuple(range(n)) + (n + 2, n, n + 1))


def kernel(x_prompt, x_sample, cache_k_cmp, cache_v_cmp, cache_k_slc, cache_v_slc, cache_k_win, cache_v_win, state_gla, page_table, norm_g, w_in, phi_k_pe, phi_k_w1, phi_k_w2, phi_v_pe, phi_v_w1, phi_v_w2, gla_alpha_up, gla_alpha_b, gla_norm_g, w_branch_a, w_branch_b, w_out, final_norm_g):
    batch, seq, _ = x_prompt.shape
    dec_batch, ds, _ = x_sample.shape
    depth = norm_g.shape[0]
    n_pool, page_size = cache_k_cmp.shape[1:3]
    assert page_size == LANES
    n_pages = page_table.shape[1]
    past = n_pages * page_size
    chunks_per_page = page_size // CMP_STRIDE
    final_g = final_norm_g.reshape(1, D_MODEL)

    cmp_pools_t = (_tokens_minor(cache_k_cmp), _tokens_minor(cache_v_cmp))
    slc_pools_t = (_tokens_minor(cache_k_slc), _tokens_minor(cache_v_slc))
    win_bufs_t = (_tokens_minor(cache_k_win), _tokens_minor(cache_v_win))

    y_p = x_prompt.reshape(batch * seq, D_MODEL)
    y_s = x_sample.reshape(dec_batch * ds, D_MODEL)
    outs_p, outs_s = [], []
    new_windows = [jnp.zeros(w.shape, F32) for w in win_bufs_t]
    new_states = jnp.zeros(state_gla.shape, F32)
    for l in range(depth):
        lw = _layer_weights(w_in[l], phi_k_pe[l], phi_k_w1[l], phi_k_w2[l], phi_v_pe[l], phi_v_w1[l], phi_v_w2[l],
                            gla_alpha_up[l], gla_alpha_b[l], gla_norm_g[l], w_branch_a[l], w_branch_b[l], w_out[l])
        last = l == depth - 1

        qp, kvt_f, kvt_b, r, rz = _project(y_p, norm_g[l], lw, batch, seq, True)
        kcf, vc2 = prompt_summaries(kvt_f[0], kvt_f[1], lw["cw_k"], lw["w2k"], lw["cw_v"], lw["w2v"])
        o_nsa = prompt_attention(qp, kcf, vc2, kvt_b, r, batch, seq)
        o_gla, s_gla = prompt_gla(r, lw["aup"], lw["ab"], lw["gg"], batch, seq)
        res = mixer_output(y_p, o_nsa, rz, o_gla, lw["w_ba"], lw["w_bb"], lw["w_out"], final_g, last)
        y_p = res[0]
        if last:
            y_p_out = res[1]
        keep = min(WINDOW, seq)
        kv5 = [_tokens_major(a) for a in kvt_f]
        outs_p.append((kv5[0], kv5[1], kv5[2], kv5[3], kv5[4][:, seq - keep:], kv5[5][:, seq - keep:], s_gla))

        qs, kvs_t, _, rs, rzs = _project(y_s, norm_g[l], lw, 1, dec_batch * ds, False)
        ab_pages = pool_halves(cmp_pools_t[0], cmp_pools_t[1], l, lw["cw_k"], lw["cw_v"])
        ab_pages = ab_pages.reshape(n_pool, chunks_per_page, -1)
        q_rows = qs.reshape(dec_batch, ds, NSA_KV_HEADS, NSA_GROUP, KV_WIDTH).transpose(0, 3, 2, 1, 4)
        q_rows = q_rows.reshape(dec_batch, NSA_HEADS * ds, KV_WIDTH)
        gl = rs[:, R_SMALL + GATE_LANE0:R_SMALL + GATE_LANE0 + 3 * NSA_HEADS]
        gl = gl.reshape(dec_batch, ds, 3, NSA_KV_HEADS, NSA_GROUP).transpose(0, 4, 3, 1, 2)
        gate_rows = jnp.pad(gl.reshape(dec_batch, NSA_HEADS * ds, 3), ((0, 0), (0, 0), (0, LANES - 3)))
        o_rows, *new_windows = sample_attention(page_table, q_rows, gate_rows, ab_pages, slc_pools_t, win_bufs_t,
                                                kvs_t, new_windows, lw["w2k"], lw["w2v"], l, past, ds)
        o_nsa_s = o_rows.reshape(dec_batch, NSA_GROUP, NSA_KV_HEADS, ds, KV_WIDTH).transpose(0, 3, 2, 1, 4)
        o_nsa_s = o_nsa_s.reshape(dec_batch * ds, PADDED_Q)
        o_gla_s, new_states = sample_gla(rs, lw["aup"], lw["ab"], lw["gg"], state_gla, new_states, l, dec_batch, ds)
        res = mixer_output(y_s, o_nsa_s, rzs, o_gla_s, lw["w_ba"], lw["w_bb"], lw["w_out"], final_g, last)
        y_s = res[0]
        if last:
            y_s_out = res[1]
        kvs5 = _tokens_major(kvs_t[:, 0]).reshape(6, dec_batch, ds, NSA_KV_HEADS, HEAD_DIM)
        outs_s.append((kvs5[0], kvs5[1], kvs5[2], kvs5[3]))

    stack = lambda outs: [jnp.stack(t) for t in zip(*outs)]
    return (y_p_out.reshape(batch, seq, D_MODEL), y_s_out.reshape(dec_batch, ds, D_MODEL),
            *stack(outs_p), *stack(outs_s), _tokens_major(new_windows[0]), _tokens_major(new_windows[1]), new_states)
```

```python
import functools

import numpy as np
import jax
import jax.numpy as jnp
from jax import lax
from jax.experimental import pallas as pl
from jax.experimental.pallas import tpu as pltpu

F32 = jnp.float32
BF16 = jnp.bfloat16

D_MODEL = 1024
NSA_HEADS = 8
NSA_KV_HEADS = 2
NSA_GROUP = NSA_HEADS // NSA_KV_HEADS
HEAD_DIM = 64
NSA_WIDTH = NSA_HEADS * HEAD_DIM
KV_WIDTH = NSA_KV_HEADS * HEAD_DIM
CMP_BLOCK = 32
CMP_STRIDE = 16
CMP_HIDDEN = 2 * HEAD_DIM
SEL_BLOCK = 64
SEL_TOPN = 16
WINDOW = 512
GLA_HEADS = 4
GLA_KEY_WIDTH = D_MODEL // 2
GLA_VAL_WIDTH = D_MODEL
GLA_DK = GLA_KEY_WIDTH // GLA_HEADS
GLA_DV = GLA_VAL_WIDTH // GLA_HEADS
GLA_GATE_RANK = 16
GLA_GATE_TAU = 16.0
GLA_CHUNK = 32
Q_BLOCK = 128
KEY_BLOCK = 256
NORM_EPS = 1e-6
NEG_INF = -1e30
TINY = 1e-30
FORCE_BONUS = 1e4
MASK_BIG = float(2.0 ** 100)
PADDED_Q = NSA_HEADS * KV_WIDTH
LANES = 128

_IN_SIZES = (NSA_WIDTH, 6 * KV_WIDTH, 3 * NSA_HEADS, NSA_WIDTH, GLA_KEY_WIDTH, GLA_KEY_WIDTH,
             GLA_VAL_WIDTH, GLA_GATE_RANK, GLA_VAL_WIDTH, 2 * D_MODEL)
_OFF = tuple(int(o) for o in np.cumsum((0,) + _IN_SIZES))

R_QG, R_KG, R_VG, R_SMALL = 0, 512, 1024, 2048
RG_WIDTH = R_SMALL + LANES
Z_ZA, Z_ZB, Z_MG = 0, 1024, 2048
RZ_WIDTH = Z_MG + 2 * D_MODEL
GATE_LANE0 = 0
ALOW_LANE0 = 3 * NSA_HEADS
FEAT_ROWS = KV_WIDTH // NSA_KV_HEADS

VMEM_LIMIT = 48 * 1024 * 1024


def _cparams(*sem):
    return pltpu.CompilerParams(dimension_semantics=sem, vmem_limit_bytes=VMEM_LIMIT)


def _tile(n, target, mult=8):
    if n <= target:
        return n
    t = (target // mult) * mult
    while t >= mult:
        if n % t == 0:
            return t
        t -= mult
    return n


def _sigmoid(x):
    return 1.0 / (1.0 + jnp.exp(-x))


def _silu(x):
    return x * _sigmoid(x)


def _log_sigmoid(x):
    return -(jnp.maximum(-x, 0.0) + jnp.log1p(jnp.exp(-jnp.abs(x))))


def _dot_nt(a, b):
    return lax.dot_general(a, b, (((1,), (1,)), ((), ())), preferred_element_type=F32)


def _dot(a, b):
    return jnp.dot(a, b, preferred_element_type=F32)


def _slope(h):
    return float(2.0 ** (-(h + 1)))


def _norm_kernel(x_ref, g_ref, o_ref):
    xf = x_ref[...]
    xn = xf * lax.rsqrt(jnp.mean(xf * xf, axis=-1, keepdims=True) + NORM_EPS)
    o_ref[...] = (xn * g_ref[...]).astype(o_ref.dtype)


def rms_norm_rows(x, g, out_dtype):
    n, d = x.shape
    tm = _tile(n, 2048)
    return pl.pallas_call(
        _norm_kernel,
        grid=(n // tm,),
        in_specs=[pl.BlockSpec((tm, d), lambda i: (i, 0)), pl.BlockSpec((1, d), lambda i: (0, 0))],
        out_specs=pl.BlockSpec((tm, d), lambda i: (i, 0)),
        out_shape=jax.ShapeDtypeStruct((n, d), out_dtype),
        compiler_params=_cparams("parallel"),
        name="rms_norm",
    )(x, g.reshape(1, d))


def _mm_kernel(x_ref, w_ref, o_ref, *, scale):
    acc = _dot(x_ref[...], w_ref[...])
    if scale != 1.0:
        acc = acc * scale
    o_ref[...] = acc.astype(o_ref.dtype)


def matmul_rows(x, w, out_dtype, *, tn, scale=1.0, tm_target=2048):
    n, k = x.shape
    m = w.shape[1]
    tm = _tile(n, tm_target)
    return pl.pallas_call(
        functools.partial(_mm_kernel, scale=scale),
        grid=(n // tm, m // tn),
        in_specs=[pl.BlockSpec((tm, k), lambda i, j: (i, 0)), pl.BlockSpec((k, tn), lambda i, j: (0, j))],
        out_specs=pl.BlockSpec((tm, tn), lambda i, j: (i, j)),
        out_shape=jax.ShapeDtypeStruct((n, m), out_dtype),
        compiler_params=_cparams("parallel", "parallel"),
        name="proj",
    )(x, w)


def _mm_t_kernel(wt_ref, x_ref, *o_refs):
    ob_ref = o_refs[-1]
    acc = _dot_nt(wt_ref[...], x_ref[...]).reshape(ob_ref.shape)
    if len(o_refs) == 2:
        o_refs[0][...] = acc
    else:
        for j, of_ref in enumerate(o_refs[:-1]):
            of_ref[...] = acc[j]
    ob_ref[...] = acc.astype(ob_ref.dtype)


def kv_project_t(x, wt, batch, seq, split_f32):
    k = x.shape[1]
    tm = _tile(seq, 1024, LANES)
    nt = seq // tm
    n_slab = wt.shape[0] // KV_WIDTH
    stacked = pl.BlockSpec((n_slab, None, KV_WIDTH, tm), lambda b, i: (0, b, 0, i))
    if split_f32:
        f_specs = [pl.BlockSpec((None, KV_WIDTH, tm), lambda b, i: (b, 0, i))] * n_slab
        f_shapes = [jax.ShapeDtypeStruct((batch, KV_WIDTH, seq), F32)] * n_slab
    else:
        f_specs = [stacked]
        f_shapes = [jax.ShapeDtypeStruct((n_slab, batch, KV_WIDTH, seq), F32)]
    outs = pl.pallas_call(
        _mm_t_kernel,
        grid=(batch, nt),
        in_specs=[pl.BlockSpec(wt.shape, lambda b, i: (0, 0)), pl.BlockSpec((tm, k), lambda b, i: (b * nt + i, 0))],
        out_specs=f_specs + [stacked],
        out_shape=f_shapes + [jax.ShapeDtypeStruct((n_slab, batch, KV_WIDTH, seq), BF16)],
        compiler_params=_cparams("parallel", "parallel"),
        name="proj_kv_t",
    )(wt, x)
    return (outs[:-1] if split_f32 else outs[0]), outs[-1]


def _chunk_rows(xs_ref, n_chunks):
    return jnp.concatenate([xs_ref[pl.ds(l, n_chunks, stride=CMP_STRIDE), :] for l in range(CMP_STRIDE)], axis=1)


def _halves(chunks, pea_ref, peb_ref, wa_ref, wb_ref):
    a = _dot((chunks + pea_ref[...]).astype(BF16), wa_ref[...])
    b = _dot((chunks + peb_ref[...]).astype(BF16), wb_ref[...])
    return a, b


def _summaries(a, b_next, w2bd):
    return _dot(_silu(a + b_next).astype(BF16), w2bd)


def _shift_up(x):
    n = x.shape[0]
    return pltpu.roll(x, n - 1, 0)


def _transpose_pages(src, xs_ref, n_pages):
    for p in range(n_pages):
        xs_ref[p * LANES:(p + 1) * LANES, :] = src(p).T


def _pool_ab_kernel(xk_ref, xv_ref, kpea, kpeb, kwa, kwb, vpea, vpeb, vwa, vwb, o_ref, xs_scr):
    n_pages = xk_ref.shape[0]
    hid = kwa.shape[1]
    for i, (x_ref, cw) in enumerate(((xk_ref, (kpea, kpeb, kwa, kwb)), (xv_ref, (vpea, vpeb, vwa, vwb)))):
        _transpose_pages(lambda p: x_ref[p], xs_scr, n_pages)
        a, b = _halves(_chunk_rows(xs_scr, n_pages * (LANES // CMP_STRIDE)), *cw)
        o_ref[:, (2 * i) * hid:(2 * i + 1) * hid] = a
        o_ref[:, (2 * i + 1) * hid:(2 * i + 2) * hid] = b


def pool_halves(pool_k_t, pool_v_t, layer, cw_k, cw_v):
    n_pool = pool_k_t.shape[1]
    pg = _tile(n_pool, 64, 1)
    rows = pg * (LANES // CMP_STRIDE)
    hid = cw_k[2].shape[1]
    full = lambda a: pl.BlockSpec(a.shape, lambda i: (0,) * a.ndim)
    page_spec = pl.BlockSpec((None, pg, KV_WIDTH, LANES), lambda i: (layer, i, 0, 0))
    return pl.pallas_call(
        _pool_ab_kernel,
        grid=(n_pool // pg,),
        in_specs=[page_spec, page_spec] + [full(c) for c in (*cw_k, *cw_v)],
        out_specs=pl.BlockSpec((rows, 4 * hid), lambda i: (i, 0)),
        out_shape=jax.ShapeDtypeStruct((n_pool * (LANES // CMP_STRIDE), 4 * hid), F32),
        scratch_shapes=[pltpu.VMEM((pg * LANES, KV_WIDTH), F32)],
        compiler_params=_cparams("parallel"),
        name="pool_halves",
    )(pool_k_t, pool_v_t, *cw_k, *cw_v)


def _pcmp_kernel(kt_ref, vt_ref, kpea, kpeb, kwa, kwb, kw2, vpea, vpeb, vwa, vwb, vw2, featc_ref,
                 kcf_ref, vc2_ref, xs_scr):
    seq = kt_ref.shape[1]
    n_blocks = seq // LANES
    n_chunks = seq // CMP_STRIDE
    half = KV_WIDTH // NSA_KV_HEADS
    for src_ref, (pea, peb, wa, wb, w2), is_key in ((kt_ref, (kpea, kpeb, kwa, kwb, kw2), True),
                                                   (vt_ref, (vpea, vpeb, vwa, vwb, vw2), False)):
        _transpose_pages(lambda p: src_ref[:, p * LANES:(p + 1) * LANES], xs_scr, n_blocks)
        a, b = _halves(_chunk_rows(xs_scr, n_chunks), pea, peb, wa, wb)
        rows = _summaries(a, _shift_up(b), w2[...])
        if is_key:
            rows_t = rows.T.astype(kcf_ref.dtype)
            kcf_ref[0, 0:half, :] = rows_t[0:half]
            kcf_ref[0, half:, :] = featc_ref[...]
            kcf_ref[1, 0:half, :] = featc_ref[...]
            kcf_ref[1, half:, :] = rows_t[half:]
        else:
            lane = lax.broadcasted_iota(jnp.int32, rows.shape, 1)
            vc2_ref[0] = jnp.where(lane < half, rows, 1.0).astype(vc2_ref.dtype)
            vc2_ref[1] = jnp.where(lane >= half, rows, 1.0).astype(vc2_ref.dtype)


def _feature_rows(pos):
    assert pos.max() // SEL_BLOCK < FEAT_ROWS
    f = np.zeros((FEAT_ROWS, pos.shape[0]), np.float32)
    f[1:] = pos[None, :] // SEL_BLOCK == np.arange(1, FEAT_ROWS)[:, None]
    f[0] = pos % SEL_BLOCK
    return f


def prompt_summaries(kt_cmp, vt_cmp, cw_k, w2k, cw_v, w2v):
    batch, _, seq = kt_cmp.shape
    n_c = seq // CMP_STRIDE
    cend = np.minimum(np.arange(n_c) * CMP_STRIDE + (CMP_BLOCK - 1), seq - 1)
    consts = [*cw_k, w2k, *cw_v, w2v, jnp.asarray(_feature_rows(cend), BF16)]
    full = lambda a: pl.BlockSpec(a.shape, lambda b: (0,) * a.ndim)
    return pl.pallas_call(
        _pcmp_kernel,
        grid=(batch,),
        in_specs=[pl.BlockSpec((None, KV_WIDTH, seq), lambda b: (b, 0, 0))] * 2 + [full(c) for c in consts],
        out_specs=[pl.BlockSpec((None, NSA_KV_HEADS, KV_WIDTH, n_c), lambda b: (b, 0, 0, 0)),
                   pl.BlockSpec((None, NSA_KV_HEADS, n_c, KV_WIDTH), lambda b: (b, 0, 0, 0))],
        out_shape=[jax.ShapeDtypeStruct((batch, NSA_KV_HEADS, KV_WIDTH, n_c), BF16),
                   jax.ShapeDtypeStruct((batch, NSA_KV_HEADS, n_c, KV_WIDTH), BF16)],
        scratch_shapes=[pltpu.VMEM((seq, KV_WIDTH), F32)],
        compiler_params=_cparams("parallel"),
        name="prompt_summaries",
    )(kt_cmp, vt_cmp, *consts)


def _rank_select_t(score_t, n_cand, topn):
    idx = lax.broadcasted_iota(jnp.int32, score_t.shape, 0)
    rank = jnp.zeros(score_t.shape, jnp.int32)
    for j in range(n_cand):
        row = score_t[j:j + 1, :]
        rank = rank + jnp.where(idx > j, jnp.where(row >= score_t, 1, 0), jnp.where(row > score_t, 1, 0))
    return jnp.where(rank < topn, 1.0, 0.0)


def _pattn_kernel(q_ref, kcf_ref, vc2_ref, kst_ref, vst_ref, kwt_ref, vwt_ref, feat_ref, sm_ref, ovt_ref,
                  o_ref, kfs_scr, kfw_scr, vs_scr, vw_scr, m_scr, acc_scr, *, n_sel, topn):
    qi = pl.program_id(1)
    t0 = qi * Q_BLOCK
    half = KV_WIDTH // NSA_KV_HEADS

    @pl.when(qi == 0)
    def _():
        ones = jnp.ones((half, kst_ref.shape[1]), BF16)
        for src, dst, fill in ((kst_ref, kfs_scr, feat_ref[...]), (kwt_ref, kfw_scr, feat_ref[...]),
                               (vst_ref, vs_scr, ones), (vwt_ref, vw_scr, ones)):
            dst[0, 0:half, :] = src[0:half, :]
            dst[0, half:, :] = fill
            dst[1, 0:half, :] = fill
            dst[1, half:, :] = src[half:, :]

    q = q_ref[...]
    gates = _sigmoid(sm_ref[...])

    def gate_col(branch, h):
        j = GATE_LANE0 + branch * NSA_HEADS + h
        return gates[:, j:j + 1]

    lane = lax.broadcasted_iota(jnp.int32, (Q_BLOCK, LANES), 1)
    feat_f = jnp.where(lane >= half, lane - half, lane).astype(F32)
    sel_lanes = []

    def own_lanes(h):
        return (lane < half) if h < NSA_GROUP else (lane >= half)

    def q_operand(selected):
        parts = []
        for h in range(NSA_HEADS):
            c = jnp.where(feat_f == 0.0, _slope(h), (_slope(h) * SEL_BLOCK) * feat_f)
            if selected:
                c = c + (sel_lanes[h // NSA_GROUP] - 1.0) * MASK_BIG
            parts.append(jnp.where(own_lanes(h), q[:, h * KV_WIDTH:(h + 1) * KV_WIDTH], c.astype(BF16)))
        return jnp.concatenate(parts, axis=0)

    grp_rows = NSA_GROUP * Q_BLOCK

    def scores(qa, kf):
        return jnp.concatenate([_dot(qa[g * grp_rows:(g + 1) * grp_rows], kf(g)) for g in range(NSA_KV_HEADS)], axis=0)

    def split_sums(h, a):
        own = own_lanes(h)
        total = jnp.where(own, pltpu.roll(a, half, 1), a)
        return jnp.where(own, a / jnp.maximum(total, TINY), 0.0), total

    qa_plain = q_operand(False)

    def compressed_and_select():
        n_c = kcf_ref.shape[2]
        s_all = scores(qa_plain, lambda g: kcf_ref[g])
        tq = lax.broadcasted_iota(jnp.int32, (Q_BLOCK, n_c), 0) + t0
        cend = lax.broadcasted_iota(jnp.int32, (Q_BLOCK, n_c), 1) * CMP_STRIDE + (CMP_BLOCK - 1)
        vis_c = tq >= cend
        lane_tiles = n_c // LANES
        floor_m = jnp.full((Q_BLOCK, LANES), NEG_INF, F32)
        ps = []
        for h in range(NSA_HEADS):
            s = jnp.where(vis_c, s_all[h * Q_BLOCK:(h + 1) * Q_BLOCK], NEG_INF)
            m = jnp.maximum(floor_m, jnp.max(s, axis=-1, keepdims=True))
            ps.append(jnp.where(vis_c, jnp.exp(s - jnp.concatenate([m] * lane_tiles, axis=1)), 0.0))
        p_all = jnp.concatenate([p.astype(BF16) for p in ps], axis=0)
        pv = [_dot(p_all[g * grp_rows:(g + 1) * grp_rows], vc2_ref[g]) for g in range(NSA_KV_HEADS)]
        psum = [None] * NSA_KV_HEADS
        for h in range(NSA_HEADS):
            g = h // NSA_GROUP
            cols = slice(h * KV_WIDTH, (h + 1) * KV_WIDTH)
            o_cmp, total = split_sums(h, pv[g][(h % NSA_GROUP) * Q_BLOCK:(h % NSA_GROUP + 1) * Q_BLOCK])
            p = ps[h] / jnp.concatenate([jnp.maximum(total, TINY)] * lane_tiles, axis=1)
            psum[g] = p if psum[g] is None else psum[g] + p
            o_ref[:, cols] = o_ref[:, cols] + gate_col(0, h) * o_cmp

        s_idx = lax.broadcasted_iota(jnp.int32, (n_sel, Q_BLOCK), 0)
        tq_s = lax.broadcasted_iota(jnp.int32, (n_sel, Q_BLOCK), 1) + t0
        valid = s_idx * SEL_BLOCK <= tq_s
        forced = (s_idx == 0) | (s_idx == tq_s // SEL_BLOCK)
        for g in range(NSA_KV_HEADS):
            hi = psum[g].astype(BF16)
            lo = (psum[g] - hi.astype(F32)).astype(BF16)
            imp_t = _dot_nt(ovt_ref[...], hi) + _dot_nt(ovt_ref[...], lo)
            score_t = jnp.where(valid, imp_t + jnp.where(forced, FORCE_BONUS, 0.0), NEG_INF)
            sel_t = _rank_select_t(score_t, n_sel, topn)
            sel_rows = [sel_t] + ([jnp.zeros((half - n_sel, Q_BLOCK), F32)] if half > n_sel else [])
            other = [jnp.zeros((half, Q_BLOCK), F32)]
            sel_lanes.append(jnp.concatenate(other + sel_rows if g == 0 else sel_rows + other, axis=0).T)

    row_i = lax.broadcasted_iota(jnp.int32, (Q_BLOCK, KEY_BLOCK), 0)
    lane_i = lax.broadcasted_iota(jnp.int32, (Q_BLOCK, KEY_BLOCK), 1)

    def reset():
        m_scr[...] = jnp.full(m_scr.shape, NEG_INF, F32)
        acc_scr[...] = jnp.zeros(acc_scr.shape, F32)

    def flash_step(kb, qa, kf_ref, v_scr, masked):
        k0 = pl.multiple_of(kb * KEY_BLOCK, KEY_BLOCK)
        s_all = scores(qa, lambda g: kf_ref[g, :, pl.ds(k0, KEY_BLOCK)])
        if masked:
            dist = (t0 - k0) + (row_i - lane_i)
            keep = jnp.where(dist >= 0, dist, WINDOW) < WINDOW
        ps = []
        alphas = []
        for h in range(NSA_HEADS):
            rows = slice(h * Q_BLOCK, (h + 1) * Q_BLOCK)
            s = s_all[rows]
            if masked:
                s = jnp.where(keep, s, NEG_INF)
            m_prev = m_scr[rows]
            m_new = jnp.maximum(m_prev, jnp.max(s, axis=-1, keepdims=True))
            p = jnp.exp(s - jnp.concatenate([m_new] * (KEY_BLOCK // LANES), axis=1))
            if masked:
                p = jnp.where(keep, p, 0.0)
            alpha = jnp.exp(m_prev - m_new)
            m_scr[rows] = m_new
            ps.append(p.astype(BF16))
            alphas.append(alpha)
        p_all = jnp.concatenate(ps, axis=0)
        pv = [_dot_nt(p_all[g * grp_rows:(g + 1) * grp_rows], v_scr[g, :, pl.ds(k0, KEY_BLOCK)])
              for g in range(NSA_KV_HEADS)]
        acc_scr[...] = jnp.concatenate(alphas, axis=0) * acc_scr[...] + jnp.concatenate(pv, axis=0)

    def finish(branch, first):
        for h in range(NSA_HEADS):
            rows = slice(h * Q_BLOCK, (h + 1) * Q_BLOCK)
            cols = slice(h * KV_WIDTH, (h + 1) * KV_WIDTH)
            o_b, _ = split_sums(h, acc_scr[rows])
            gated = gate_col(branch, h) * o_b
            o_ref[:, cols] = gated if first else o_ref[:, cols] + gated

    def loop(lo, hi, qa, kf_ref, v_scr):
        n = jnp.maximum(hi - lo, 0)

        def pair(i, c):
            flash_step(lo + 2 * i, qa, kf_ref, v_scr, False)
            flash_step(lo + 2 * i + 1, qa, kf_ref, v_scr, False)
            return c

        lax.fori_loop(0, n // 2, pair, 0)

        @pl.when(n % 2 == 1)
        def _():
            flash_step(hi - 1, qa, kf_ref, v_scr, False)

    per_key_block = KEY_BLOCK // Q_BLOCK
    kb_diag = qi // per_key_block
    kb_lo = jnp.maximum(qi - WINDOW // Q_BLOCK, 0) // per_key_block

    reset()

    @pl.when(kb_lo < kb_diag)
    def _():
        flash_step(kb_lo, qa_plain, kfw_scr, vw_scr, True)

    loop(kb_lo + 1, kb_diag, qa_plain, kfw_scr, vw_scr)
    flash_step(kb_diag, qa_plain, kfw_scr, vw_scr, True)
    finish(2, True)
    compressed_and_select()
    reset()
    qa = q_operand(True)
    loop(0, kb_diag, qa, kfs_scr, vs_scr)
    flash_step(kb_diag, qa, kfs_scr, vs_scr, True)
    finish(1, False)


def prompt_attention(qp, kcf, vc2, kvt_b, r, batch, seq):
    assert seq % KEY_BLOCK == 0 and min(SEL_TOPN, seq // SEL_BLOCK) >= 2
    nqb = seq // Q_BLOCK
    n_c = kcf.shape[3]
    assert n_c % LANES == 0
    n_sel = seq // SEL_BLOCK
    topn = min(SEL_TOPN, n_sel)
    cs = np.arange(n_c)[:, None] * CMP_STRIDE
    ss = np.arange(n_sel)[None, :] * SEL_BLOCK
    ov = np.clip(np.minimum(cs + CMP_BLOCK, ss + SEL_BLOCK) - np.maximum(cs, ss), 0, None) / CMP_BLOCK
    ov[(seq - CMP_BLOCK) // CMP_STRIDE + 1:] = 0.0
    kv_spec = lambda idx: pl.BlockSpec((None, None, KV_WIDTH, seq), lambda b, i: (idx, b, 0, 0))
    return pl.pallas_call(
        functools.partial(_pattn_kernel, n_sel=n_sel, topn=topn),
        grid=(batch, nqb),
        in_specs=[
            pl.BlockSpec((Q_BLOCK, PADDED_Q), lambda b, i: (b * nqb + i, 0)),
            pl.BlockSpec((None, NSA_KV_HEADS, KV_WIDTH, n_c), lambda b, i: (b, 0, 0, 0)),
            pl.BlockSpec((None, NSA_KV_HEADS, n_c, KV_WIDTH), lambda b, i: (b, 0, 0, 0)),
            kv_spec(2), kv_spec(3), kv_spec(4), kv_spec(5),
            pl.BlockSpec((FEAT_ROWS, seq), lambda b, i: (0, 0)),
            pl.BlockSpec((Q_BLOCK, LANES), lambda b, i: (b * nqb + i, R_SMALL // LANES)),
            pl.BlockSpec((n_sel, n_c), lambda b, i: (0, 0)),
        ],
        out_specs=pl.BlockSpec((Q_BLOCK, PADDED_Q), lambda b, i: (b * nqb + i, 0)),
        out_shape=jax.ShapeDtypeStruct((batch * seq, PADDED_Q), F32),
        scratch_shapes=[
            pltpu.VMEM((NSA_KV_HEADS, KV_WIDTH, seq), BF16),
            pltpu.VMEM((NSA_KV_HEADS, KV_WIDTH, seq), BF16),
            pltpu.VMEM((NSA_KV_HEADS, KV_WIDTH, seq), BF16),
            pltpu.VMEM((NSA_KV_HEADS, KV_WIDTH, seq), BF16),
            pltpu.VMEM((NSA_HEADS * Q_BLOCK, LANES), F32),
            pltpu.VMEM((NSA_HEADS * Q_BLOCK, KV_WIDTH), F32),
        ],
        compiler_params=_cparams("parallel", "arbitrary"),
        name="prompt_nsa",
    )(qp, kcf, vc2, kvt_b, kvt_b, kvt_b, kvt_b, jnp.asarray(_feature_rows(np.arange(seq)), BF16), r,
      jnp.asarray(ov.T, BF16))


def _gla_log_decay(sm, aup_ref, ab_ref):
    z = _dot(sm.astype(BF16), aup_ref[...]) + ab_ref[...]
    return _log_sigmoid(z) / GLA_GATE_TAU


def _gla_head_norm(o, gg_ref):
    on = o * lax.rsqrt(jnp.mean(o * o, axis=-1, keepdims=True) + NORM_EPS)
    return on * gg_ref[...]


def _pgla_kernel(qg_ref, kg_ref, vg_ref, sm_ref, aup_ref, ab_ref, gg_ref, og_ref, sfin_ref, st_scr, *, n_tiles):
    ti = pl.program_id(1)
    tt = qg_ref.shape[0]
    width = qg_ref.shape[1]

    @pl.when(ti == 0)
    def _():
        st_scr[...] = jnp.zeros(st_scr.shape, F32)

    la = _gla_log_decay(sm_ref[...], aup_ref, ab_ref)
    rin = lax.broadcasted_iota(jnp.int32, la.shape, 0)
    bg = la
    sh = 1
    while sh < tt:
        bg = bg + jnp.where(rin >= sh, pltpu.roll(bg, sh, 0), 0.0)
        sh *= 2

    def row_of_block(x, block, row):
        x3 = x.reshape(tt // block, block, width)
        return jnp.broadcast_to(x3[:, row:row + 1, :], x3.shape).reshape(tt, width)

    n_chunks = tt // GLA_CHUNK
    ends = bg.reshape(n_chunks, GLA_CHUNK, width)[:, GLA_CHUNK - 1:GLA_CHUNK, :]
    prev_end = jnp.concatenate([jnp.zeros((1, 1, width), F32), ends[:n_chunks - 1]], axis=0)
    b = (bg.reshape(n_chunks, GLA_CHUNK, width) - prev_end).reshape(tt, width)
    bg_end = bg[tt - 1:tt, :]
    scales = [(jnp.exp(b), jnp.exp(-b))]
    block = 2 * GLA_CHUNK
    while block <= tt:
        mid = row_of_block(bg, block, block // 2 - 1)
        scales.append((jnp.exp(jnp.minimum(bg - mid, 0.0)), jnp.exp(jnp.minimum(mid - bg, 0.0))))
        block *= 2
    e_in = jnp.exp(bg)
    e_out = jnp.exp(bg_end - bg)

    r_i = lax.broadcasted_iota(jnp.int32, (tt, tt), 0)
    c_i = lax.broadcasted_iota(jnp.int32, (tt, tt), 1)
    level = jnp.full((tt, tt), len(scales) - 1, jnp.int32)
    block = tt // 2
    lv = len(scales) - 2
    while block >= GLA_CHUNK:
        level = jnp.where(r_i // block == c_i // block, lv, level)
        block //= 2
        lv -= 1
    level = jnp.where(c_i <= r_i, level, -1)

    for h in range(GLA_HEADS):
        ks = slice(h * GLA_DK, (h + 1) * GLA_DK)
        vs = slice(h * GLA_DV, (h + 1) * GLA_DV)
        q = qg_ref[:, ks] * (GLA_DK ** -0.5)
        k = kg_ref[:, ks]
        v = vg_ref[:, vs].astype(BF16)
        a = jnp.zeros((tt, tt), F32)
        for lv, (sq, sk) in enumerate(scales):
            a_lv = _dot_nt((q * sq[:, ks]).astype(BF16), (k * sk[:, ks]).astype(BF16))
            a = jnp.where(level == lv, a_lv, a)
        st = st_scr[h]
        o = _dot(a.astype(BF16), v) + _dot_nt((q * e_in[:, ks]).astype(BF16), st.astype(BF16))
        kv_t = lax.dot_general(v, (k * e_out[:, ks]).astype(BF16), (((0,), (0,)), ((), ())),
                               preferred_element_type=F32)
        st_scr[h] = st * jnp.exp(bg_end[:, ks]) + kv_t
        og_ref[:, vs] = _gla_head_norm(o, gg_ref)

    @pl.when(ti == n_tiles - 1)
    def _():
        for h in range(GLA_HEADS):
            sfin_ref[h] = st_scr[h].T


def prompt_gla(r, aup, ab, gg, batch, seq):
    tt = _tile(seq, 256, GLA_CHUNK)
    nt = seq // tt
    row = lambda b, i: b * nt + i
    return pl.pallas_call(
        functools.partial(_pgla_kernel, n_tiles=nt),
        grid=(batch, nt),
        in_specs=[
            pl.BlockSpec((tt, GLA_KEY_WIDTH), lambda b, i: (row(b, i), R_QG // GLA_KEY_WIDTH)),
            pl.BlockSpec((tt, GLA_KEY_WIDTH), lambda b, i: (row(b, i), R_KG // GLA_KEY_WIDTH)),
            pl.BlockSpec((tt, GLA_VAL_WIDTH), lambda b, i: (row(b, i), R_VG // GLA_VAL_WIDTH)),
            pl.BlockSpec((tt, LANES), lambda b, i: (row(b, i), R_SMALL // LANES)),
            pl.BlockSpec(aup.shape, lambda b, i: (0, 0)),
            pl.BlockSpec(ab.shape, lambda b, i: (0, 0)),
            pl.BlockSpec(gg.shape, lambda b, i: (0, 0)),
        ],
        out_specs=[
            pl.BlockSpec((tt, GLA_VAL_WIDTH), lambda b, i: (row(b, i), 0)),
            pl.BlockSpec((None, GLA_HEADS, GLA_DK, GLA_DV), lambda b, i: (b, 0, 0, 0)),
        ],
        out_shape=[
            jax.ShapeDtypeStruct((batch * seq, GLA_VAL_WIDTH), F32),
            jax.ShapeDtypeStruct((batch, GLA_HEADS, GLA_DK, GLA_DV), F32),
        ],
        scratch_shapes=[pltpu.VMEM((GLA_HEADS, GLA_DV, GLA_DK), F32)],
        compiler_params=_cparams("parallel", "arbitrary"),
        name="prompt_gla",
    )(r, r, r, r, aup, ab, gg)


def _sgla_kernel(qg_ref, kg_ref, vg_ref, sm_ref, aup_ref, ab_ref, gg_ref, s_ref, all_layers_ref, og_ref, snew_ref,
                 *, ds):
    del all_layers_ref
    rows = qg_ref.shape[0]
    nb = rows // ds
    la = _gla_log_decay(sm_ref[...], aup_ref, ab_ref)
    ri = lax.broadcasted_iota(jnp.int32, la.shape, 0) % ds
    b = la
    sh = 1
    while sh < ds:
        b = b + jnp.where(ri >= sh, pltpu.roll(b, sh, 0), 0.0)
        sh *= 2
    b_last = b
    for d in range(1, ds):
        b_last = jnp.where(ri == ds - 1 - d, pltpu.roll(b, rows - d, 0), b_last)
    e_b = jnp.exp(b)
    e_nb = jnp.exp(-b)
    e_tail = jnp.exp(b_last - b)
    e_last = jnp.exp(b_last)

    r_i = lax.broadcasted_iota(jnp.int32, (rows, rows), 0)
    c_i = lax.broadcasted_iota(jnp.int32, (rows, rows), 1)
    causal = (r_i // ds == c_i // ds) & (c_i <= r_i)
    row_b = lax.broadcasted_iota(jnp.int32, (rows, GLA_DV), 0) // ds

    for h in range(GLA_HEADS):
        ks = slice(h * GLA_DK, (h + 1) * GLA_DK)
        vs = slice(h * GLA_DV, (h + 1) * GLA_DV)
        q = qg_ref[:, ks] * (GLA_DK ** -0.5)
        k = kg_ref[:, ks]
        v = vg_ref[:, vs]
        q_dec = (q * e_b[:, ks]).astype(BF16)
        k_inv = (k * e_nb[:, ks]).astype(BF16)
        k_tail_t = (k * e_tail[:, ks]).T
        e_last_t = e_last[:, ks].T
        a = jnp.where(causal, _dot_nt(q_dec, k_inv), 0.0)
        o = _dot(a.astype(BF16), v.astype(BF16))
        for bb in range(nb):
            s_prev = s_ref[bb, h]
            o_inter = _dot(q_dec, s_prev.astype(BF16))
            o = o + jnp.where(row_b == bb, o_inter, 0.0)
            s_new = e_last_t[:, bb * ds:bb * ds + 1] * s_prev
            for j in range(ds):
                rr = bb * ds + j
                s_new = s_new + k_tail_t[:, rr:rr + 1] * v[rr:rr + 1, :]
            snew_ref[bb, h] = s_new
        og_ref[:, vs] = _gla_head_norm(o, gg_ref)


def sample_gla(r, aup, ab, gg, state, new_states, layer, dec_batch, ds):
    nb = _tile(dec_batch, 8, 1)
    rows = nb * ds
    return pl.pallas_call(
        functools.partial(_sgla_kernel, ds=ds),
        grid=(dec_batch // nb,),
        in_specs=[
            pl.BlockSpec((rows, GLA_KEY_WIDTH), lambda i: (i, R_QG // GLA_KEY_WIDTH)),
            pl.BlockSpec((rows, GLA_KEY_WIDTH), lambda i: (i, R_KG // GLA_KEY_WIDTH)),
            pl.BlockSpec((rows, GLA_VAL_WIDTH), lambda i: (i, R_VG // GLA_VAL_WIDTH)),
            pl.BlockSpec((rows, LANES), lambda i: (i, R_SMALL // LANES)),
            pl.BlockSpec(aup.shape, lambda i: (0, 0)),
            pl.BlockSpec(ab.shape, lambda i: (0, 0)),
            pl.BlockSpec(gg.shape, lambda i: (0, 0)),
            pl.BlockSpec((None, nb, GLA_HEADS, GLA_DK, GLA_DV), lambda i: (layer, i, 0, 0, 0)),
            pl.BlockSpec(memory_space=pl.ANY),
        ],
        out_specs=[
            pl.BlockSpec((rows, GLA_VAL_WIDTH), lambda i: (i, 0)),
            pl.BlockSpec((None, nb, GLA_HEADS, GLA_DK, GLA_DV), lambda i: (layer, i, 0, 0, 0)),
        ],
        out_shape=[
            jax.ShapeDtypeStruct((dec_batch * ds, GLA_VAL_WIDTH), F32),
            jax.ShapeDtypeStruct(new_states.shape, F32),
        ],
        input_output_aliases={8: 1},
        compiler_params=_cparams("parallel"),
        name="sample_gla",
    )(r, r, r, r, aup, ab, gg, state, new_states)


def _mix_kernel(x_ref, on_ref, za_ref, og_ref, zb_ref, mg_ref, wba_ref, wbb_ref, wo_ref, fg_ref, *o_refs):
    f32 = lambda ref: ref[...].astype(F32)
    pa = _dot((on_ref[...] * _silu(f32(za_ref))).astype(BF16), wba_ref[...])
    pb = _dot((og_ref[...] * _silu(f32(zb_ref))).astype(BF16), wbb_ref[...])
    mg = f32(mg_ref)
    mix = _sigmoid(mg[:, :D_MODEL]) * pa + _sigmoid(mg[:, D_MODEL:]) * pb
    y = x_ref[...] + _dot(mix.astype(BF16), wo_ref[...])
    o_refs[0][...] = y
    if len(o_refs) > 1:
        yn = y * lax.rsqrt(jnp.mean(y * y, axis=-1, keepdims=True) + NORM_EPS)
        o_refs[1][...] = yn * fg_ref[...]


def mixer_output(x, o_nsa, rz, o_gla, wba, wbb, wo, final_g, with_final_norm):
    n = x.shape[0]
    tm = _tile(n, 256)
    row = lambda w: pl.BlockSpec((tm, w), lambda i: (i, 0))
    wspec = lambda w: pl.BlockSpec(w.shape, lambda i: (0, 0))
    n_out = 2 if with_final_norm else 1
    return pl.pallas_call(
        _mix_kernel,
        grid=(n // tm,),
        in_specs=[
            row(D_MODEL), row(PADDED_Q),
            pl.BlockSpec((tm, PADDED_Q), lambda i: (i, Z_ZA // PADDED_Q)),
            row(GLA_VAL_WIDTH),
            pl.BlockSpec((tm, GLA_VAL_WIDTH), lambda i: (i, Z_ZB // GLA_VAL_WIDTH)),
            pl.BlockSpec((tm, 2 * D_MODEL), lambda i: (i, Z_MG // (2 * D_MODEL))),
            wspec(wba), wspec(wbb), wspec(wo), wspec(final_g),
        ],
        out_specs=[row(D_MODEL)] * n_out,
        out_shape=[jax.ShapeDtypeStruct((n, D_MODEL), F32)] * n_out,
        compiler_params=_cparams("parallel"),
        name="mixer_output",
    )(x, o_nsa, rz, o_gla, rz, rz, wba, wbb, wo, final_g)


def _select_blocks(score, lane_idx, n_cand, topn):
    rank = jnp.zeros(score.shape, jnp.int32)
    for j in range(n_cand):
        col = score[:, j:j + 1]
        rank = rank + jnp.where(col > score, 1, 0) + jnp.where(col == score, jnp.where(lane_idx > j, 1, 0), 0)
    return jnp.where(rank < topn, 1.0, 0.0)


def _softmax_two(parts, keeps):
    m = functools.reduce(jnp.maximum, [jnp.max(s, axis=-1, keepdims=True) for s in parts])
    ps = [jnp.where(kp, jnp.exp(s - m), 0.0) for s, kp in zip(parts, keeps)]
    total = functools.reduce(lambda a, c: a + c, [jnp.sum(p, axis=-1, keepdims=True) for p in ps])
    return ps, jnp.maximum(total, TINY)


def _smain_kernel(pt_ref, q_ref, gate_ref, slope_ref, ab_hbm, kt_hbm, vt_hbm, kw_ref, vw_ref, new_ref,
                  w2k_ref, w2v_ref, ov_ref, exp_ref, exn_ref, kw_all_ref, vw_all_ref, o_ref, kw_out_ref, vw_out_ref,
                  ab_buf, kt_buf, vt_buf, sems, *, layer, n_pages, past, ds, n_sel, topn):
    del kw_all_ref, vw_all_ref
    step = pl.program_id(0)
    buf_slot = step % 2
    rows_per_step = q_ref.shape[0]

    def page_copies(st, sl):
        cps = []
        for r in range(rows_per_step):
            for j in range(n_pages):
                pg = pt_ref[st * rows_per_step + r, j]
                cps.append(pltpu.make_async_copy(ab_hbm.at[pg], ab_buf.at[sl, r, j], sems.at[sl, 0]))
                cps.append(pltpu.make_async_copy(kt_hbm.at[layer, pg], kt_buf.at[sl, r, j], sems.at[sl, 1]))
                cps.append(pltpu.make_async_copy(vt_hbm.at[layer, pg], vt_buf.at[sl, r, j], sems.at[sl, 2]))
        return cps

    @pl.when(step == 0)
    def _():
        for cp in page_copies(0, 0):
            cp.start()

    @pl.when(step + 1 < pl.num_programs(0))
    def _():
        for cp in page_copies(step + 1, 1 - buf_slot):
            cp.start()

    for cp in page_copies(step, buf_slot):
        cp.wait()

    active = [_sample_row(step * rows_per_step + r, q_ref.at[r], gate_ref.at[r], slope_ref,
                          ab_buf.at[buf_slot, r], kt_buf.at[buf_slot, r], vt_buf.at[buf_slot, r],
                          kw_ref.at[r], vw_ref.at[r], new_ref, w2k_ref, w2v_ref, ov_ref, exp_ref, exn_ref,
                          o_ref.at[r], kw_out_ref.at[r], vw_out_ref.at[r],
                          n_pages=n_pages, past=past, ds=ds, n_sel=n_sel, topn=topn)
              for r in range(rows_per_step)]
    while active:
        active = [row for row in active if next(row, True) is None]


def _sample_row(b, q_ref, gate_ref, slope_ref, ab_ref, kt_ref, vt_ref, kw_ref, vw_ref, new_ref,
                w2k_ref, w2v_ref, ov_ref, exp_ref, exn_ref, o_ref, kw_out_ref, vw_out_ref,
                *, n_pages, past, ds, n_sel, topn):
    rows = q_ref.shape[0]
    grp_rows = NSA_KV_HEADS * ds
    n_new = new_ref.shape[2]
    b_loc = b % (n_new // ds)
    kt_refs = [kt_ref.at[j] for j in range(n_pages)]
    vt_refs = [vt_ref.at[j] for j in range(n_pages)]

    qs = q_ref[...]
    slope = slope_ref[:, 0:1]
    ab = jnp.concatenate([ab_ref[j] for j in range(n_pages)], axis=0)
    hid = ab.shape[1] // 4
    kc = _summaries(ab[:, 0:hid], _shift_up(ab[:, hid:2 * hid]), w2k_ref[...]).astype(BF16)
    vc = _summaries(ab[:, 2 * hid:3 * hid], _shift_up(ab[:, 3 * hid:]), w2v_ref[...]).astype(BF16)
    n_c = kc.shape[0]

    def irow(shape):
        return lax.broadcasted_iota(jnp.int32, shape, 0) % ds

    yield
    dist_c = (past + irow((rows, n_c))) - (lax.broadcasted_iota(jnp.int32, (rows, n_c), 1) * CMP_STRIDE
                                          + (CMP_BLOCK - 1))
    vis_c = dist_c >= 0
    s_c = jnp.where(vis_c, _dot_nt(qs, kc) - slope * dist_c.astype(F32), NEG_INF)
    m_c = jnp.max(s_c, axis=-1, keepdims=True)
    p_c = jnp.where(vis_c, jnp.exp(s_c - m_c), 0.0)
    p_c = p_c / jnp.maximum(jnp.sum(p_c, axis=-1, keepdims=True), TINY)
    o_cmp = _dot(p_c.astype(BF16), vc)

    yield
    hi = p_c.astype(BF16)
    lo = (p_c - hi.astype(F32)).astype(BF16)
    imp_h = _dot(hi, ov_ref[...]) + _dot(lo, ov_ref[...])
    imp = functools.reduce(lambda a, c: a + c,
                           [imp_h[r * grp_rows:(r + 1) * grp_rows] for r in range(NSA_GROUP)])
    s_idx = lax.broadcasted_iota(jnp.int32, imp.shape, 1)
    qpos = past + irow(imp.shape)
    valid = (s_idx < n_sel) & (s_idx * SEL_BLOCK <= qpos)
    forced = (s_idx == 0) | (s_idx == qpos // SEL_BLOCK)
    score = jnp.where(valid, imp + jnp.where(forced, FORCE_BONUS, 0.0), NEG_INF)
    sel = _select_blocks(score, s_idx, n_sel, topn)
    sel = jnp.concatenate([sel] * NSA_GROUP, axis=0).astype(BF16)
    keep_past = _dot(sel, exp_ref[...]) > 0.5
    sel_new = _dot(sel, exn_ref[...]) > 0.5

    yield
    new_lane = lax.broadcasted_iota(jnp.int32, (rows, n_new), 1)
    dist_n = irow((rows, n_new)) - new_lane % ds
    mine = (new_lane // ds == b_loc) & (dist_n >= 0)
    bias_n = slope * dist_n.astype(F32)

    s_p = jnp.concatenate([_dot(qs, r[...].astype(BF16)) for r in kt_refs], axis=1)
    dist_p = (past + irow(s_p.shape)) - lax.broadcasted_iota(jnp.int32, s_p.shape, 1)
    s_p = jnp.where(keep_past, s_p - slope * dist_p.astype(F32), NEG_INF)
    keep_n = mine & sel_new
    s_n = jnp.where(keep_n, _dot(qs, new_ref[2].astype(BF16)) - bias_n, NEG_INF)
    (p_p, p_n), total = _softmax_two([s_p, s_n], [keep_past, keep_n])
    acc = _dot_nt(p_n.astype(BF16), new_ref[3].astype(BF16))
    for j, r in enumerate(vt_refs):
        acc = acc + _dot_nt(p_p[:, j * LANES:(j + 1) * LANES].astype(BF16), r[...].astype(BF16))
    o_slc = acc / total

    yield
    kbuf = kw_ref[...]
    vbuf = vw_ref[...]
    w_buf = kbuf.shape[1]
    dist_w = (w_buf + irow((rows, w_buf))) - lax.broadcasted_iota(jnp.int32, (rows, w_buf), 1)
    keep_w = dist_w < WINDOW
    s_w = jnp.where(keep_w, _dot(qs, kbuf.astype(BF16)) - slope * dist_w.astype(F32), NEG_INF)
    s_wn = jnp.where(mine, _dot(qs, new_ref[4].astype(BF16)) - bias_n, NEG_INF)
    (p_w, p_wn), total_w = _softmax_two([s_w, s_wn], [keep_w, mine])
    o_win = (_dot_nt(p_w.astype(BF16), vbuf.astype(BF16)) + _dot_nt(p_wn.astype(BF16), new_ref[5].astype(BF16))) / total_w

    gates = _sigmoid(gate_ref[...])
    o_ref[...] = gates[:, 0:1] * o_cmp + gates[:, 1:2] * o_slc + gates[:, 2:3] * o_win

    yield
    lane_new = lax.broadcasted_iota(jnp.int32, (KV_WIDTH, n_new), 1)
    shift = (n_new - ds) - b_loc * ds
    for buf, slab, out_ref in ((kbuf, 4, kw_out_ref), (vbuf, 5, vw_out_ref)):
        rolled = pltpu.roll(buf, w_buf - ds, 1)
        own_last = pltpu.roll(new_ref[slab], shift, 1)
        out_ref[:, 0:w_buf - n_new] = rolled[:, 0:w_buf - n_new]
        out_ref[:, w_buf - n_new:w_buf] = jnp.where(lane_new >= n_new - ds, own_last, rolled[:, w_buf - n_new:w_buf])


def sample_attention(page_table, q_rows, gate_rows, ab_pages, slc_pools_t, win_bufs_t, new_t, new_windows, w2k, w2v,
                     layer, past, ds):
    assert past % SEL_BLOCK == 0 and past % LANES == 0
    dec_batch, n_pages = page_table.shape
    rows = q_rows.shape[1]
    w_buf = win_bufs_t[0].shape[3]
    n_new = min(LANES, dec_batch * ds)
    assert (dec_batch * ds) % n_new == 0 and n_new % ds == 0 and w_buf >= n_new
    n_c = n_pages * (LANES // CMP_STRIDE)
    n_cmp = (past + ds - CMP_BLOCK) // CMP_STRIDE + 1
    n_sel = -(-(past + ds) // SEL_BLOCK)
    topn = min(SEL_TOPN, n_sel)
    assert n_sel <= LANES and ds <= CMP_STRIDE
    cs = np.arange(n_c)[:, None] * CMP_STRIDE
    ss = np.arange(LANES)[None, :] * SEL_BLOCK
    ov = np.clip(np.minimum(cs + CMP_BLOCK, ss + SEL_BLOCK) - np.maximum(cs, ss), 0, None) / CMP_BLOCK
    ov[n_cmp:] = 0.0
    ov[:, n_sel:] = 0.0
    ex_past = (np.arange(past)[None, :] // SEL_BLOCK == np.arange(LANES)[:, None]).astype(np.float32)
    ex_new = ((past + np.arange(n_new) % ds)[None, :] // SEL_BLOCK == np.arange(LANES)[:, None]).astype(np.float32)
    head = np.arange(rows) // (NSA_KV_HEADS * ds) + NSA_GROUP * ((np.arange(rows) // ds) % NSA_KV_HEADS)
    slope = np.broadcast_to((2.0 ** -(head + 1.0))[:, None], (rows, LANES)).astype(np.float32)

    rps = max(r for r in (2, 1) if dec_batch % r == 0 and (n_new // ds) % r == 0)
    const = lambda shape: pl.BlockSpec(shape, lambda s, pt: (0,) * len(shape))
    in_specs = [pl.BlockSpec((rps, rows, LANES), lambda s, pt: (s, 0, 0)),
                pl.BlockSpec((rps, rows, LANES), lambda s, pt: (s, 0, 0)),
                const((rows, LANES))]
    operands = [q_rows, gate_rows, jnp.asarray(slope)]
    in_specs += [pl.BlockSpec(memory_space=pl.ANY)] * 3
    operands += [ab_pages, slc_pools_t[0], slc_pools_t[1]]
    for arr in win_bufs_t:
        in_specs.append(pl.BlockSpec((None, rps, KV_WIDTH, w_buf), lambda s, pt: (layer, s, 0, 0)))
        operands.append(arr)
    steps_per_tile = n_new // ds // rps
    in_specs.append(pl.BlockSpec((6, None, KV_WIDTH, n_new), lambda s, pt: (0, 0, 0, s // steps_per_tile)))
    operands.append(new_t)
    consts = [w2k, w2v, jnp.asarray(ov, BF16), jnp.asarray(ex_past, BF16), jnp.asarray(ex_new, BF16)]
    in_specs += [const(c.shape) for c in consts]
    operands += consts
    in_specs += [pl.BlockSpec(memory_space=pl.ANY)] * 2
    operands += list(new_windows)
    aliases = {1 + len(operands) - 2: 1, 1 + len(operands) - 1: 2}
    buf_out = pl.BlockSpec((None, rps, KV_WIDTH, w_buf), lambda s, pt: (layer, s, 0, 0))
    return pl.pallas_call(
        functools.partial(_smain_kernel, layer=layer, n_pages=n_pages, past=past, ds=ds, n_sel=n_sel, topn=topn),
        grid_spec=pltpu.PrefetchScalarGridSpec(
            num_scalar_prefetch=1,
            grid=(dec_batch // rps,),
            in_specs=in_specs,
            out_specs=[pl.BlockSpec((rps, rows, LANES), lambda s, pt: (s, 0, 0)), buf_out, buf_out],
            scratch_shapes=[pltpu.VMEM((2, rps, n_pages) + ab_pages.shape[1:], F32),
                            pltpu.VMEM((2, rps, n_pages, KV_WIDTH, LANES), F32),
                            pltpu.VMEM((2, rps, n_pages, KV_WIDTH, LANES), F32),
                            pltpu.SemaphoreType.DMA((2, 3))],
        ),
        out_shape=[jax.ShapeDtypeStruct((dec_batch, rows, LANES), F32),
                   jax.ShapeDtypeStruct(new_windows[0].shape, F32),
                   jax.ShapeDtypeStruct(new_windows[1].shape, F32)],
        input_output_aliases=aliases,
        compiler_params=_cparams("arbitrary"),
        name="sample_nsa",
    )(page_table, *operands)


def _pad_heads_cols(w):
    k = w.shape[0]
    w4 = w.reshape(k, NSA_KV_HEADS, NSA_GROUP, HEAD_DIM)
    eye = jnp.eye(NSA_KV_HEADS, dtype=w.dtype)
    return jnp.einsum("kgrd,gp->kgrpd", w4, eye).reshape(k, PADDED_Q)


def _layer_weights(w_in, pk_pe, pk_w1, pk_w2, pv_pe, pv_w1, pv_w2, a_up, a_b, gla_g, w_ba, w_bb, w_out):
    o = _OFF
    col = lambda i: w_in[:, o[i]:o[i + 1]]
    small = jnp.concatenate([col(2), col(7), jnp.zeros((D_MODEL, LANES - 3 * NSA_HEADS - GLA_GATE_RANK), F32)], axis=1)
    w_q = _pad_heads_cols(col(0)).astype(BF16)
    w_kv_t = col(1).T.astype(BF16)
    w_g = jnp.concatenate([col(4), col(5), col(6), small], axis=1).astype(BF16)
    w_z = jnp.concatenate([_pad_heads_cols(col(3)), col(8), col(9)], axis=1).astype(BF16)
    half = CMP_BLOCK // 2

    def cmp_weights(pe, w1, w2):
        eye = jnp.eye(NSA_KV_HEADS, dtype=F32)
        w1h = w1.reshape(2, half, HEAD_DIM, CMP_HIDDEN)
        big = lambda w: jnp.einsum("ldh,gk->lgdkh", w, eye).reshape(half * KV_WIDTH, NSA_KV_HEADS * CMP_HIDDEN)
        peh = pe.reshape(2, half, 1, HEAD_DIM)
        pe_row = lambda p: jnp.broadcast_to(p, (half, NSA_KV_HEADS, HEAD_DIM)).reshape(1, half * KV_WIDTH)
        w2bd = jnp.einsum("hd,gk->ghkd", w2, eye).reshape(NSA_KV_HEADS * CMP_HIDDEN, KV_WIDTH)
        return (pe_row(peh[0]), pe_row(peh[1]), big(w1h[0]).astype(BF16), big(w1h[1]).astype(BF16)), w2bd.astype(BF16)

    cw_k, w2k = cmp_weights(pk_pe, pk_w1, pk_w2)
    cw_v, w2v = cmp_weights(pv_pe, pv_w1, pv_w2)
    aup = jnp.zeros((LANES, GLA_KEY_WIDTH), F32).at[ALOW_LANE0:ALOW_LANE0 + GLA_GATE_RANK].set(a_up).astype(BF16)
    w_ba_p = _pad_heads_cols(w_ba.T).T.astype(BF16)
    return dict(w_q=w_q, w_kv_t=w_kv_t, w_g=w_g, w_z=w_z, cw_k=cw_k, cw_v=cw_v, w2k=w2k, w2v=w2v, aup=aup,
                ab=a_b.reshape(1, GLA_KEY_WIDTH), gg=gla_g.reshape(1, GLA_DV),
                w_ba=w_ba_p, w_bb=w_bb.astype(BF16), w_out=w_out.astype(BF16))


def _project(x, g_norm, lw, batch, seq, split_f32):
    h = rms_norm_rows(x, g_norm, BF16)
    qp = matmul_rows(h, lw["w_q"], BF16, tn=512, scale=HEAD_DIM ** -0.5)
    kvt_f, kvt_b = kv_project_t(h, lw["w_kv_t"], batch, seq, split_f32)
    rg = matmul_rows(h, lw["w_g"], F32, tn=RG_WIDTH, tm_target=1024)
    rz = matmul_rows(h, lw["w_z"], BF16, tn=1024)
    return qp, kvt_f, kvt_b, rg, rz


def _tokens_minor(cache):
    lead = cache.shape[:-3]
    n = cache.ndim
    perm = tuple(range(n - 3)) + (n - 2, n - 1, n - 3)
    return cache.transpose(perm).reshape(lead + (KV_WIDTH, cache.shape[-3]))


def _tokens_major(x_t):
    lead = x_t.shape[:-2]
    n = len(lead)
    x5 = x_t.reshape(lead + (NSA_KV_HEADS, HEAD_DIM, x_t.shape[-1]))
    return x5.transpose(tuple(range(n)) + (n + 2, n, n + 1))


def kernel(x_prompt, x_sample, cache_k_cmp, cache_v_cmp, cache_k_slc, cache_v_slc, cache_k_win, cache_v_win, state_gla, page_table, norm_g, w_in, phi_k_pe, phi_k_w1, phi_k_w2, phi_v_pe, phi_v_w1, phi_v_w2, gla_alpha_up, gla_alpha_b, gla_norm_g, w_branch_a, w_branch_b, w_out, final_norm_g):
    batch, seq, _ = x_prompt.shape
    dec_batch, ds, _ = x_sample.shape
    depth = norm_g.shape[0]
    n_pool, page_size = cache_k_cmp.shape[1:3]
    assert page_size == LANES
    n_pages = page_table.shape[1]
    past = n_pages * page_size
    chunks_per_page = page_size // CMP_STRIDE
    final_g = final_norm_g.reshape(1, D_MODEL)

    cmp_pools_t = (_tokens_minor(cache_k_cmp), _tokens_minor(cache_v_cmp))
    slc_pools_t = (_tokens_minor(cache_k_slc), _tokens_minor(cache_v_slc))
    win_bufs_t = (_tokens_minor(cache_k_win), _tokens_minor(cache_v_win))

    y_p = x_prompt.reshape(batch * seq, D_MODEL)
    y_s = x_sample.reshape(dec_batch * ds, D_MODEL)
    outs_p, outs_s = [], []
    new_windows = [jnp.zeros(w.shape, F32) for w in win_bufs_t]
    new_states = jnp.zeros(state_gla.shape, F32)
    for l in range(depth):
        lw = _layer_weights(w_in[l], phi_k_pe[l], phi_k_w1[l], phi_k_w2[l], phi_v_pe[l], phi_v_w1[l], phi_v_w2[l],
                            gla_alpha_up[l], gla_alpha_b[l], gla_norm_g[l], w_branch_a[l], w_branch_b[l], w_out[l])
        last = l == depth - 1

        qp, kvt_f, kvt_b, r, rz = _project(y_p, norm_g[l], lw, batch, seq, True)
        kcf, vc2 = prompt_summaries(kvt_f[0], kvt_f[1], lw["cw_k"], lw["w2k"], lw["cw_v"], lw["w2v"])
        o_nsa = prompt_attention(qp, kcf, vc2, kvt_b, r, batch, seq)
        o_gla, s_gla = prompt_gla(r, lw["aup"], lw["ab"], lw["gg"], batch, seq)
        res = mixer_output(y_p, o_nsa, rz, o_gla, lw["w_ba"], lw["w_bb"], lw["w_out"], final_g, last)
        y_p = res[0]
        if last:
            y_p_out = res[1]
        keep = min(WINDOW, seq)
        kv5 = [_tokens_major(a) for a in kvt_f]
        outs_p.append((kv5[0], kv5[1], kv5[2], kv5[3], kv5[4][:, seq - keep:], kv5[5][:, seq - keep:], s_gla))

        qs, kvs_t, _, rs, rzs = _project(y_s, norm_g[l], lw, 1, dec_batch * ds, False)
        ab_pages = pool_halves(cmp_pools_t[0], cmp_pools_t[1], l, lw["cw_k"], lw["cw_v"])
        ab_pages = ab_pages.reshape(n_pool, chunks_per_page, -1)
        q_rows = qs.reshape(dec_batch, ds, NSA_KV_HEADS, NSA_GROUP, KV_WIDTH).transpose(0, 3, 2, 1, 4)
        q_rows = q_rows.reshape(dec_batch, NSA_HEADS * ds, KV_WIDTH)
        gl = rs[:, R_SMALL + GATE_LANE0:R_SMALL + GATE_LANE0 + 3 * NSA_HEADS]
        gl = gl.reshape(dec_batch, ds, 3, NSA_KV_HEADS, NSA_GROUP).transpose(0, 4, 3, 1, 2)
        gate_rows = jnp.pad(gl.reshape(dec_batch, NSA_HEADS * ds, 3), ((0, 0), (0, 0), (0, LANES - 3)))
        o_rows, *new_windows = sample_attention(page_table, q_rows, gate_rows, ab_pages, slc_pools_t, win_bufs_t,
                                                kvs_t, new_windows, lw["w2k"], lw["w2v"], l, past, ds)
        o_nsa_s = o_rows.reshape(dec_batch, NSA_GROUP, NSA_KV_HEADS, ds, KV_WIDTH).transpose(0, 3, 2, 1, 4)
        o_nsa_s = o_nsa_s.reshape(dec_batch * ds, PADDED_Q)
        o_gla_s, new_states = sample_gla(rs, lw["aup"], lw["ab"], lw["gg"], state_gla, new_states, l, dec_batch, ds)
        res = mixer_output(y_s, o_nsa_s, rzs, o_gla_s, lw["w_ba"], lw["w_bb"], lw["w_out"], final_g, last)
        y_s = res[0]
        if last:
            y_s_out = res[1]
        kvs5 = _tokens_major(kvs_t[:, 0]).reshape(6, dec_batch, ds, NSA_KV_HEADS, HEAD_DIM)
        outs_s.append((kvs5[0], kvs5[1], kvs5[2], kvs5[3]))

    stack = lambda outs: [jnp.stack(t) for t in zip(*outs)]
    return (y_p_out.reshape(batch, seq, D_MODEL), y_s_out.reshape(dec_batch, ds, D_MODEL),
            *stack(outs_p), *stack(outs_s), _tokens_major(new_windows[0]), _tokens_major(new_windows[1]), new_states)
```
